```python
import jax
import jax.numpy as jnp
from jax import lax
import numpy as np

D_MODEL = 1024
BATCH = 8
SEQ = 8192
DEPTH = 4
DEC_BATCH = 4
DEC_SEQ = 4096
PAST_LEN = 128

HEAD_DIM = 128
N_HEADS = 8
N_KV_HEADS = 2
GROUP = N_HEADS // N_KV_HEADS
ATTN_DIM = N_HEADS * HEAD_DIM
KV_DIM = N_KV_HEADS * HEAD_DIM
WINDOW = 128
BLOCK = 128
ROPE_THETA = 10000.0
D_RNN = 3 * D_MODEL // 2
N_RNN_BLOCKS = 12
RNN_BLOCK = D_RNN // N_RNN_BLOCKS
CONV_WIDTH = 4
CONV_LEFT = 2
RGLRU_C = 8.0
EPS = 1e-6
D_IN = 2 * ATTN_DIM + 2 * KV_DIM + 2 * D_RNN

kernel_name = 'griffin_bidir_hybrid_encoder'


def rms_norm(x, gain):
    xf = x.astype(jnp.float32)
    y = xf * lax.rsqrt(jnp.mean(xf * xf, axis=-1, keepdims=True) + EPS)
    return (y * gain.astype(jnp.float32)).astype(x.dtype)


def rotary(x, pos):
    inv_freq = ROPE_THETA ** (-jnp.arange(0, HEAD_DIM, 2, dtype=jnp.float32) / HEAD_DIM)
    ang = pos[:, None] * inv_freq[None, :]
    cos = jnp.cos(ang)[None, :, None, :]
    sin = jnp.sin(ang)[None, :, None, :]
    xf = x.astype(jnp.float32)
    x1, x2 = jnp.split(xf, 2, axis=-1)
    out = jnp.concatenate([x1 * cos - x2 * sin, x2 * cos + x1 * sin], axis=-1)
    return out.astype(x.dtype)


def window_attention(q, k, v, sink):
    B, S = q.shape[0], q.shape[1]
    nb = S // BLOCK
    qb = q.reshape(B, nb, BLOCK, N_KV_HEADS, GROUP, HEAD_DIM)

    def band(t):
        tp = jnp.pad(t, ((0, 0), (BLOCK, BLOCK), (0, 0), (0, 0)))
        views = [tp[:, i * BLOCK:i * BLOCK + S].reshape(B, nb, BLOCK, N_KV_HEADS, HEAD_DIM)
                 for i in range(3)]
        return jnp.concatenate(views, axis=2)

    kb = band(k)
    vb = band(v)
    scores = jnp.einsum('bnqhgd,bnjhd->bnhgqj', qb, kb,
                        preferred_element_type=jnp.float32) * (HEAD_DIM ** -0.5)
    n_idx = jnp.arange(nb)[:, None, None]
    q_pos = n_idx * BLOCK + jnp.arange(BLOCK)[None, :, None]
    k_pos = (n_idx - 1) * BLOCK + jnp.arange(3 * BLOCK)[None, None, :]
    valid = (jnp.abs(k_pos - q_pos) <= WINDOW) & (k_pos >= 0) & (k_pos < S)
    scores = jnp.where(valid[None, :, None, None], scores, -1e30)
    sink_l = sink.astype(jnp.float32).reshape(N_KV_HEADS, GROUP)[None, None, :, :, None, None]
    m = jnp.maximum(jnp.max(scores, axis=-1, keepdims=True), sink_l)
    p = jnp.exp(scores - m)
    denom = jnp.sum(p, axis=-1, keepdims=True) + jnp.exp(sink_l - m)
    out = jnp.einsum('bnhgqj,bnjhd->bnqhgd', (p / denom).astype(v.dtype), vb)
    return out.reshape(B, S, ATTN_DIM)


def centred_depthwise_conv(x, w, b):
    S = x.shape[1]
    xp = jnp.pad(x, ((0, 0), (CONV_LEFT, CONV_WIDTH - 1 - CONV_LEFT), (0, 0)))
    y = xp[:, 0:S] * w[0]
    for t in range(1, CONV_WIDTH):
        y = y + xp[:, t:t + S] * w[t]
    return y + b


def _linear_recurrence_combine(left, right):
    a1, b1 = left
    a2, b2 = right
    return a1 * a2, a2 * b1 + b2


def rglru(xc, w_a, b_a, w_x, b_x, lam, reverse):
    B, S = xc.shape[0], xc.shape[1]
    xb = xc.reshape(B, S, N_RNN_BLOCKS, RNN_BLOCK)
    gate_a = jnp.einsum('bshi,hij->bshj', xb, w_a).reshape(B, S, D_RNN) + b_a
    gate_x = jnp.einsum('bshi,hij->bshj', xb, w_x).reshape(B, S, D_RNN) + b_x
    r = jax.nn.sigmoid(gate_a.astype(jnp.float32))
    i = jax.nn.sigmoid(gate_x.astype(jnp.float32))
    log_a = -RGLRU_C * jax.nn.softplus(-lam.astype(jnp.float32)) * r
    a = jnp.exp(log_a)
    b = jnp.sqrt(-jnp.expm1(2.0 * log_a)) * (i * xc.astype(jnp.float32))
    _, h = lax.associative_scan(_linear_recurrence_combine, (a, b), reverse=reverse, axis=1)
    return h


def hybrid_layer(x, cond, pos, norm_gain, w_ada, b_ada, w_in, attn_sink, conv_w, conv_b,
                 rg_w_a, rg_b_a, rg_w_x, rg_b_x, rg_lambda, w_attn_proj, w_rnn_proj,
                 w_merge, b_merge, w_out):
    B, S = x.shape[0], x.shape[1]
    mod = jax.nn.silu(cond) @ w_ada + b_ada
    shift, scale, gate = jnp.split(mod, 3, axis=-1)
    h = rms_norm(x, norm_gain) * (1.0 + scale[:, None, :]) + shift[:, None, :]
    proj = h @ w_in
    offs = [ATTN_DIM, ATTN_DIM + KV_DIM, ATTN_DIM + 2 * KV_DIM,
            2 * ATTN_DIM + 2 * KV_DIM, 2 * ATTN_DIM + 2 * KV_DIM + D_RNN]
    q, k, v, attn_g, rnn_x, rnn_g = jnp.split(proj, offs, axis=-1)
    q = rotary(q.reshape(B, S, N_HEADS, HEAD_DIM), pos)
    k = rotary(k.reshape(B, S, N_KV_HEADS, HEAD_DIM), pos)
    v = v.reshape(B, S, N_KV_HEADS, HEAD_DIM)
    attn = window_attention(q, k, v, attn_sink) * jax.nn.silu(attn_g)
    xc = centred_depthwise_conv(rnn_x, conv_w, conv_b)
    h_fwd = rglru(xc, rg_w_a[0], rg_b_a[0], rg_w_x[0], rg_b_x[0], rg_lambda[0], False)
    h_bwd = rglru(xc, rg_w_a[1], rg_b_a[1], rg_w_x[1], rg_b_x[1], rg_lambda[1], True)
    rnn = (h_fwd + h_bwd).astype(x.dtype) * jax.nn.silu(rnn_g)
    merge = jax.nn.sigmoid((h @ w_merge + b_merge).astype(jnp.float32)).astype(x.dtype)
    g_attn, g_rnn = jnp.split(merge, 2, axis=-1)
    mixed = g_attn * (attn @ w_attn_proj) + g_rnn * (rnn @ w_rnn_proj)
    return x + gate[:, None, :] * (mixed @ w_out)


def encoder_trunk(x, cond, norm_gain, w_ada, b_ada, w_in, attn_sink, conv_w, conv_b,
                  rg_w_a, rg_b_a, rg_w_x, rg_b_x, rg_lambda, w_attn_proj, w_rnn_proj,
                  w_merge, b_merge, w_out, final_gain):
    pos = jnp.arange(x.shape[1], dtype=jnp.float32)
    for l in range(DEPTH):
        x = hybrid_layer(x, cond, pos, norm_gain[l], w_ada[l], b_ada[l], w_in[l], attn_sink[l],
                         conv_w[l], conv_b[l], rg_w_a[l], rg_b_a[l], rg_w_x[l], rg_b_x[l],
                         rg_lambda[l], w_attn_proj[l], w_rnn_proj[l], w_merge[l], b_merge[l],
                         w_out[l])
    return rms_norm(x, final_gain)


def setup_inputs(seed: int = 0) -> dict:
    key = jax.random.key(seed)
    ks = jax.random.split(key, 24)
    f32 = jnp.float32

    def nrm(k, shape, s):
        return jax.random.normal(k, shape, f32) * s

    a_init = jax.random.uniform(ks[14], (DEPTH, 2, D_RNN), f32, 0.9, 0.999)
    return {
        'x_prompt': nrm(ks[0], (BATCH, SEQ, D_MODEL), 1.0),
        'x_sample': nrm(ks[1], (DEC_BATCH, DEC_SEQ, D_MODEL), 1.0),
        'c_prompt': nrm(ks[2], (BATCH, D_MODEL), 1.0),
        'c_sample': nrm(ks[3], (DEC_BATCH, D_MODEL), 1.0),
        'norm_gain': 1.0 + nrm(ks[4], (DEPTH, D_MODEL), 0.05),
        'w_ada': nrm(ks[5], (DEPTH, D_MODEL, 3 * D_MODEL), 0.5 * D_MODEL ** -0.5),
        'b_ada': nrm(ks[6], (DEPTH, 3 * D_MODEL), 0.01),
        'w_in': nrm(ks[7], (DEPTH, D_MODEL, D_IN), D_MODEL ** -0.5),
        'attn_sink': nrm(ks[8], (DEPTH, N_HEADS), 1.0),
        'conv_w': nrm(ks[9], (DEPTH, CONV_WIDTH, D_RNN), CONV_WIDTH ** -0.5),
        'conv_b': nrm(ks[10], (DEPTH, D_RNN), 0.01),
        'rg_w_a': nrm(ks[11], (DEPTH, 2, N_RNN_BLOCKS, RNN_BLOCK, RNN_BLOCK), RNN_BLOCK ** -0.5),
        'rg_b_a': nrm(ks[12], (DEPTH, 2, D_RNN), 0.01),
        'rg_w_x': nrm(ks[13], (DEPTH, 2, N_RNN_BLOCKS, RNN_BLOCK, RNN_BLOCK), RNN_BLOCK ** -0.5),
        'rg_b_x': nrm(ks[15], (DEPTH, 2, D_RNN), 0.01),
        'rg_lambda': jnp.log(a_init) - jnp.log1p(-a_init),
        'w_attn_proj': nrm(ks[16], (DEPTH, ATTN_DIM, D_MODEL), ATTN_DIM ** -0.5),
        'w_rnn_proj': nrm(ks[17], (DEPTH, D_RNN, D_MODEL), D_RNN ** -0.5),
        'w_merge': nrm(ks[18], (DEPTH, D_MODEL, 2 * D_MODEL), D_MODEL ** -0.5),
        'b_merge': nrm(ks[19], (DEPTH, 2 * D_MODEL), 0.01),
        'w_out': nrm(ks[20], (DEPTH, D_MODEL, D_MODEL), D_MODEL ** -0.5),
        'final_gain': 1.0 + nrm(ks[21], (D_MODEL,), 0.05),
    }


def reference(x_prompt, x_sample, c_prompt, c_sample, norm_gain, w_ada, b_ada, w_in, attn_sink,
              conv_w, conv_b, rg_w_a, rg_b_a, rg_w_x, rg_b_x, rg_lambda, w_attn_proj,
              w_rnn_proj, w_merge, b_merge, w_out, final_gain):
    y_prompt = encoder_trunk(x_prompt, c_prompt, norm_gain, w_ada, b_ada, w_in, attn_sink,
                             conv_w, conv_b, rg_w_a, rg_b_a, rg_w_x, rg_b_x, rg_lambda,
                             w_attn_proj, w_rnn_proj, w_merge, b_merge, w_out, final_gain)
    y_sample = encoder_trunk(x_sample, c_sample, norm_gain, w_ada, b_ada, w_in, attn_sink,
                             conv_w, conv_b, rg_w_a, rg_b_a, rg_w_x, rg_b_x, rg_lambda,
                             w_attn_proj, w_rnn_proj, w_merge, b_merge, w_out, final_gain)
    return (y_prompt, y_sample)
```

```python
import functools
import math

import jax
import jax.numpy as jnp
from jax import lax
from jax.experimental import pallas as pl
from jax.experimental.pallas import tpu as pltpu

D_MODEL = 1024
HEAD_DIM = 128
N_HEADS = 8
N_KV_HEADS = 2
GROUP = N_HEADS // N_KV_HEADS
ATTN_DIM = N_HEADS * HEAD_DIM
KV_DIM = N_KV_HEADS * HEAD_DIM
BLOCK = 128
ROPE_THETA = 10000.0
D_RNN = 3 * D_MODEL // 2
N_RNN_BLOCKS = 12
RNN_BLOCK = D_RNN // N_RNN_BLOCKS
CONV_WIDTH = 4
CONV_LEFT = 2
RGLRU_C = 8.0
EPS = 1e-6
MASK_VALUE = -1e30

SUBLANES = 8
LANES = 128
MXU_COLS = 256
VMEM_LIMIT_BYTES = 56 * 1024 * 1024

F32 = jnp.float32
BF16 = jnp.bfloat16


def _sigmoid(x):
    return 0.5 * jnp.tanh(0.5 * x) + 0.5


def _silu(x):
    return x * _sigmoid(x)


def _resident(shape):
    zeros = (0,) * len(shape)
    return pl.BlockSpec(shape, lambda *_: zeros, pipeline_mode=pl.Buffered(1))


def _params(n_axes):
    return pltpu.CompilerParams(dimension_semantics=("arbitrary",) * n_axes,
                                vmem_limit_bytes=VMEM_LIMIT_BYTES)


def _mod_kernel(c_ref, w_ref, b_ref, o_ref):
    c = c_ref[...]
    s = c * jax.nn.sigmoid(c)
    o_ref[...] = jnp.dot(s, w_ref[...], precision=lax.Precision.HIGHEST,
                         preferred_element_type=F32) + b_ref[...]


def _modulation(cond, w_ada, b_ada):
    depth = w_ada.shape[0]
    nb = cond.shape[0]
    n_col = 3 * D_MODEL // D_MODEL
    return pl.pallas_call(
        _mod_kernel,
        grid=(depth, n_col),
        in_specs=[
            pl.BlockSpec((nb, D_MODEL), lambda l, j: (0, 0)),
            pl.BlockSpec((None, D_MODEL, D_MODEL), lambda l, j: (l, 0, j)),
            pl.BlockSpec((None, 1, D_MODEL), lambda l, j: (l, 0, j)),
        ],
        out_specs=pl.BlockSpec((None, nb, D_MODEL), lambda l, j: (l, 0, j)),
        out_shape=jax.ShapeDtypeStruct((depth, nb, 3 * D_MODEL), F32),
        compiler_params=_params(2),
        name="adaln_mod",
    )(cond, w_ada, b_ada.reshape(depth, 1, 3 * D_MODEL))


def _rope_kernel(cos_ref, sin_ref):
    rows = cos_ref.shape[0]
    t = lax.broadcasted_iota(jnp.int32, (rows, HEAD_DIM), 0) + pl.program_id(0) * rows
    j = lax.broadcasted_iota(jnp.int32, (rows, HEAD_DIM), 1)
    half = HEAD_DIM // 2
    jj = jnp.where(j < half, j, j - half).astype(F32)
    inv_freq = jnp.exp(jj * (-2.0 / HEAD_DIM * math.log(ROPE_THETA)))
    ang = t.astype(F32) * inv_freq
    s = jnp.sin(ang)
    cos_ref[...] = jnp.cos(ang)
    sin_ref[...] = jnp.where(j < half, -s, s)


def _rope_tables(seq):
    rows = min(seq, 512)
    return pl.pallas_call(
        _rope_kernel,
        grid=(seq // rows,),
        out_specs=[pl.BlockSpec((rows, HEAD_DIM), lambda i: (i, 0))] * 2,
        out_shape=[jax.ShapeDtypeStruct((seq, HEAD_DIM), F32)] * 2,
        compiler_params=_params(1),
        name="rope_tables",
    )()


NORM_ROWS = 32


def _proj_kernel(x_ref, mod_ref, ng_ref, win_ref, wm_ref, bm_ref, cos_ref, sin_ref,
                 q_ref, k_ref, v_ref, ga_ref, rx_ref, gr_ref, mg_ref, h_scr):
    ts = x_ref.shape[0]
    gain = ng_ref[...]
    shift = mod_ref[:, 0:D_MODEL]
    scale1 = 1.0 + mod_ref[:, D_MODEL:2 * D_MODEL]

    def norm_rows(r, carry):
        rows = pl.ds(pl.multiple_of(r * NORM_ROWS, NORM_ROWS), NORM_ROWS)
        x = x_ref[rows, :]
        ms = jnp.mean(x * x, axis=-1, keepdims=True)
        y = x * lax.rsqrt(ms + EPS) * gain
        h_scr[rows, :] = (y * scale1 + shift).astype(BF16)
        return carry

    lax.fori_loop(0, ts // NORM_ROWS, norm_rows, 0)

    cos = cos_ref[...]
    sin = sin_ref[...]

    def rotary(p):
        return p * cos + pltpu.roll(p, HEAD_DIM // 2, axis=1) * sin

    def chunk(w_ref, c0):
        return jnp.dot(h_scr[...], w_ref[:, c0:c0 + MXU_COLS], preferred_element_type=F32)

    qk_scale = HEAD_DIM ** -0.5
    col = 0
    for c in range(ATTN_DIM // MXU_COLS):
        p = chunk(win_ref, col + c * MXU_COLS)
        for hh in range(MXU_COLS // HEAD_DIM):
            dst = slice(c * MXU_COLS + hh * HEAD_DIM, c * MXU_COLS + (hh + 1) * HEAD_DIM)
            q_ref[:, dst] = (rotary(p[:, hh * HEAD_DIM:(hh + 1) * HEAD_DIM]) * qk_scale).astype(BF16)
    col += ATTN_DIM
    p = chunk(win_ref, col)
    for hh in range(N_KV_HEADS):
        dst = slice(hh * HEAD_DIM, (hh + 1) * HEAD_DIM)
        k_ref[:, dst] = rotary(p[:, dst]).astype(BF16)
    col += KV_DIM
    v_ref[...] = chunk(win_ref, col).astype(BF16)
    col += KV_DIM
    for c in range(ATTN_DIM // MXU_COLS):
        dst = slice(c * MXU_COLS, (c + 1) * MXU_COLS)
        ga_ref[:, dst] = _silu(chunk(win_ref, col + c * MXU_COLS)).astype(BF16)
    col += ATTN_DIM
    for c in range(D_RNN // MXU_COLS):
        dst = slice(c * MXU_COLS, (c + 1) * MXU_COLS)
        rx_ref[:, dst] = chunk(win_ref, col + c * MXU_COLS)
    col += D_RNN
    for c in range(D_RNN // MXU_COLS):
        dst = slice(c * MXU_COLS, (c + 1) * MXU_COLS)
        gr_ref[:, dst] = _silu(chunk(win_ref, col + c * MXU_COLS)).astype(BF16)
    for c in range(2 * D_MODEL // MXU_COLS):
        dst = slice(c * MXU_COLS, (c + 1) * MXU_COLS)
        mg_ref[:, dst] = _sigmoid(chunk(wm_ref, c * MXU_COLS) + bm_ref[:, dst]).astype(BF16)


def _seq_tile(seq, want):
    return want if seq % want == 0 else seq


def _project(x, mod, norm_gain, w_in, w_merge, b_merge, cos, sin, batch_off):
    bsz, seq, _ = x.shape
    ts = _seq_tile(seq, 512)
    d_in = w_in.shape[1]

    def tile(width):
        return pl.BlockSpec((None, ts, width), lambda b, t: (b, t, 0))

    outs = [(ATTN_DIM, BF16), (KV_DIM, BF16), (KV_DIM, BF16), (ATTN_DIM, BF16),
            (D_RNN, F32), (D_RNN, BF16), (2 * D_MODEL, BF16)]
    return pl.pallas_call(
        _proj_kernel,
        grid=(bsz, seq // ts),
        in_specs=[
            tile(D_MODEL),
            pl.BlockSpec((None, 1, 3 * D_MODEL), lambda b, t: (b + batch_off, 0, 0)),
            _resident((1, D_MODEL)),
            _resident((D_MODEL, d_in)),
            _resident((D_MODEL, 2 * D_MODEL)),
            _resident((1, 2 * D_MODEL)),
            pl.BlockSpec((ts, HEAD_DIM), lambda b, t: (t, 0)),
            pl.BlockSpec((ts, HEAD_DIM), lambda b, t: (t, 0)),
        ],
        out_specs=[tile(w) for w, _ in outs],
        out_shape=[jax.ShapeDtypeStruct((bsz, seq, w), dt) for w, dt in outs],
        scratch_shapes=[pltpu.VMEM((ts, D_MODEL), BF16)],
        compiler_params=_params(2),
        name="in_proj",
    )(x, mod, norm_gain, w_in, w_merge, b_merge, cos, sin)


HALO = SUBLANES


def _rglru_kernel(rx_ref, prev_ref, next_ref, cw_ref, cb_ref, wg_ref, bg_ref, lam_ref, *rest,
                  reverse):
    if reverse:
        out_ref, ext, a_scr, b_scr, carry = rest
    else:
        hb_ref, gr_ref, out_ref, ext, a_scr, b_scr, carry = rest
    ts = rx_ref.shape[0]
    t = pl.program_id(1)
    nt = pl.num_programs(1)
    pos = (nt - 1 - t) if reverse else t

    @pl.when(t == 0)
    def _():
        carry[...] = jnp.zeros_like(carry)

    ext[0:HALO, :] = jnp.where(pos > 0, prev_ref[...], 0.0)
    ext[HALO:HALO + ts, :] = rx_ref[...]
    ext[HALO + ts:, :] = jnp.where(pos < nt - 1, next_ref[...], 0.0)

    for c in range(N_RNN_BLOCKS):
        cols = slice(c * RNN_BLOCK, (c + 1) * RNN_BLOCK)
        xc = cb_ref[:, cols]
        for tap in range(CONV_WIDTH):
            start = HALO - CONV_LEFT + tap
            xc = xc + ext[start:start + ts, cols] * cw_ref[tap:tap + 1, cols]
        g = jnp.dot(xc.astype(BF16), wg_ref[c], preferred_element_type=F32) + bg_ref[:, c, :]
        r = _sigmoid(g[:, :RNN_BLOCK])
        i = _sigmoid(g[:, RNN_BLOCK:])
        lam = lam_ref[:, cols]
        neg = -lam
        softplus = jnp.maximum(neg, 0.0) + jnp.log(1.0 + jnp.exp(-jnp.abs(neg)))
        log_a = (-RGLRU_C * softplus) * r
        a = jnp.exp(log_a)
        a_scr[:, cols] = a
        b_scr[:, cols] = jnp.sqrt(1.0 - a * a) * (i * xc)

    row = lax.broadcasted_iota(jnp.int32, (SUBLANES, D_RNN), 0)
    steps = (1, 2, 4)
    if reverse:
        masks = [row < SUBLANES - d for d in steps]
        shifts = [SUBLANES - d for d in steps]
        carry_row = 0
    else:
        masks = [row >= d for d in steps]
        shifts = list(steps)
        carry_row = SUBLANES - 1
    n_groups = ts // SUBLANES

    def group(gi, h_in):
        g_idx = (n_groups - 1 - gi) if reverse else gi
        rows = pl.ds(pl.multiple_of(g_idx * SUBLANES, SUBLANES), SUBLANES)
        a = a_scr[rows, :]
        b = b_scr[rows, :]
        for m, s in zip(masks, shifts):
            a_s = pltpu.roll(a, s, axis=0)
            b_s = pltpu.roll(b, s, axis=0)
            b = jnp.where(m, a * b_s + b, b)
            a = jnp.where(m, a * a_s, a)
        h = a * h_in + b
        if reverse:
            out_ref[rows, :] = h
        else:
            out_ref[rows, :] = ((h + hb_ref[rows, :]) * gr_ref[rows, :].astype(F32)).astype(BF16)
        return jnp.broadcast_to(h[carry_row:carry_row + 1, :], (SUBLANES, D_RNN))

    carry[...] = lax.fori_loop(0, n_groups, group, carry[...])


def _rglru(rnn_x, conv_w, conv_b, w_gate, b_gate, lam, *, reverse, h_bwd=None, gate=None):
    bsz, seq, _ = rnn_x.shape
    ts = _seq_tile(seq, 512)
    nt = seq // ts
    per = ts // HALO
    n_halo = seq // HALO

    def tpos(t):
        return (nt - 1 - t) if reverse else t

    tile = pl.BlockSpec((None, ts, D_RNN), lambda b, t: (b, tpos(t), 0))
    in_specs = [
        tile,
        pl.BlockSpec((None, HALO, D_RNN), lambda b, t: (b, jnp.maximum(tpos(t) * per - 1, 0), 0)),
        pl.BlockSpec((None, HALO, D_RNN),
                     lambda b, t: (b, jnp.minimum((tpos(t) + 1) * per, n_halo - 1), 0)),
        _resident((CONV_WIDTH, D_RNN)),
        _resident((1, D_RNN)),
        _resident((N_RNN_BLOCKS, RNN_BLOCK, 2 * RNN_BLOCK)),
        _resident((1, N_RNN_BLOCKS, 2 * RNN_BLOCK)),
        _resident((1, D_RNN)),
    ]
    args = [rnn_x, rnn_x, rnn_x, conv_w, conv_b, w_gate, b_gate, lam]
    if reverse:
        out_dtype = F32
    else:
        in_specs += [tile, tile]
        args += [h_bwd, gate]
        out_dtype = BF16
    return pl.pallas_call(
        functools.partial(_rglru_kernel, reverse=reverse),
        grid=(bsz, nt),
        in_specs=in_specs,
        out_specs=tile,
        out_shape=jax.ShapeDtypeStruct((bsz, seq, D_RNN), out_dtype),
        scratch_shapes=[
            pltpu.VMEM((ts + 2 * HALO, D_RNN), F32),
            pltpu.VMEM((ts, D_RNN), F32),
            pltpu.VMEM((ts, D_RNN), F32),
            pltpu.VMEM((SUBLANES, D_RNN), F32),
        ],
        compiler_params=_params(2),
        name="rglru_bwd" if reverse else "rglru_fwd",
    )(*args)


def _attn_out_kernel(sink_ref, q_ref, k_ref, kp_ref, kn_ref, v_ref, vp_ref, vn_ref, ga_ref,
                     rnn_ref, mg_ref, x_ref, mod_ref, wa_ref, wr_ref, wo_ref, fg_ref,
                     o_ref, kext, vext, attn_scr, *, final):
    tq = q_ref.shape[0]
    t = pl.program_id(1)
    nt = pl.num_programs(1)
    n_blk = tq // BLOCK

    kext[0:BLOCK, :] = kp_ref[...]
    kext[BLOCK:BLOCK + tq, :] = k_ref[...]
    kext[BLOCK + tq:, :] = kn_ref[...]
    vext[0:BLOCK, :] = vp_ref[...]
    vext[BLOCK:BLOCK + tq, :] = v_ref[...]
    vext[BLOCK + tq:, :] = vn_ref[...]

    band = 3 * BLOCK
    qi = lax.broadcasted_iota(jnp.int32, (BLOCK, band), 0)
    kj = lax.broadcasted_iota(jnp.int32, (BLOCK, band), 1)
    in_window = (kj >= qi) & (kj <= qi + 2 * BLOCK)

    for blk in range(n_blk):
        rows = slice(blk * BLOCK, (blk + 1) * BLOCK)
        band_rows = slice(blk * BLOCK, blk * BLOCK + band)
        first = (t == 0) if blk == 0 else False
        last = (t == nt - 1) if blk == n_blk - 1 else False
        lo = jnp.where(first, BLOCK, 0)
        hi = jnp.where(last, 2 * BLOCK, band)
        valid = in_window & (kj >= lo) & (kj < hi)
        for g in range(N_KV_HEADS):
            kv_cols = slice(g * HEAD_DIM, (g + 1) * HEAD_DIM)
            qg = jnp.concatenate(
                [q_ref[rows, (g * GROUP + h) * HEAD_DIM:(g * GROUP + h + 1) * HEAD_DIM]
                 for h in range(GROUP)], axis=0)
            s = lax.dot_general(qg, kext[band_rows, kv_cols], (((1,), (1,)), ((), ())),
                                preferred_element_type=F32)
            s = s.reshape(GROUP, BLOCK, band)
            s = jnp.where(valid[None], s, MASK_VALUE)
            sink = jnp.stack([jnp.full((BLOCK, 1), sink_ref[g * GROUP + h], F32)
                              for h in range(GROUP)], axis=0)
            m = jnp.maximum(jnp.max(s, axis=-1, keepdims=True), sink)
            p = jnp.exp(s - m)
            denom = jnp.sum(p, axis=-1, keepdims=True) + jnp.exp(sink - m)
            o = jnp.dot(p.reshape(GROUP * BLOCK, band).astype(BF16), vext[band_rows, kv_cols],
                        preferred_element_type=F32)
            o = o.reshape(GROUP, BLOCK, HEAD_DIM) / denom
            for h in range(GROUP):
                cols = slice((g * GROUP + h) * HEAD_DIM, (g * GROUP + h + 1) * HEAD_DIM)
                attn_scr[rows, cols] = (o[h] * ga_ref[rows, cols].astype(F32)).astype(BF16)

    a = jnp.dot(attn_scr[...], wa_ref[...], preferred_element_type=F32)
    r = jnp.dot(rnn_ref[...], wr_ref[...], preferred_element_type=F32)
    mixed = (mg_ref[:, 0:D_MODEL].astype(F32) * a + mg_ref[:, D_MODEL:].astype(F32) * r)
    y = jnp.dot(mixed.astype(BF16), wo_ref[...], preferred_element_type=F32)
    out = x_ref[...] + mod_ref[:, 2 * D_MODEL:] * y
    if final:
        ms = jnp.mean(out * out, axis=-1, keepdims=True)
        out = out * lax.rsqrt(ms + EPS) * fg_ref[...]
    o_ref[...] = out


def _attend_and_mix(sink, q, k, v, ga, rnn, mg, x, mod, w_attn, w_rnn, w_out, final_gain,
                    batch_off, final):
    bsz, seq, _ = x.shape
    tq = _seq_tile(seq, 512)
    per = tq // BLOCK
    n_blocks = seq // BLOCK

    def tile(width):
        return pl.BlockSpec((None, tq, width), lambda b, t: (b, t, 0))

    prev_kv = pl.BlockSpec((None, BLOCK, KV_DIM), lambda b, t: (b, jnp.maximum(t * per - 1, 0), 0))
    next_kv = pl.BlockSpec((None, BLOCK, KV_DIM),
                           lambda b, t: (b, jnp.minimum((t + 1) * per, n_blocks - 1), 0))
    return pl.pallas_call(
        functools.partial(_attn_out_kernel, final=final),
        grid=(bsz, seq // tq),
        in_specs=[
            pl.BlockSpec(memory_space=pltpu.SMEM),
            tile(ATTN_DIM),
            tile(KV_DIM), prev_kv, next_kv,
            tile(KV_DIM), prev_kv, next_kv,
            tile(ATTN_DIM),
            tile(D_RNN),
            tile(2 * D_MODEL),
            tile(D_MODEL),
            pl.BlockSpec((None, 1, 3 * D_MODEL), lambda b, t: (b + batch_off, 0, 0)),
            _resident((ATTN_DIM, D_MODEL)),
            _resident((D_RNN, D_MODEL)),
            _resident((D_MODEL, D_MODEL)),
            _resident((1, D_MODEL)),
        ],
        out_specs=tile(D_MODEL),
        out_shape=jax.ShapeDtypeStruct((bsz, seq, D_MODEL), F32),
        scratch_shapes=[
            pltpu.VMEM((tq + 2 * BLOCK, KV_DIM), BF16),
            pltpu.VMEM((tq + 2 * BLOCK, KV_DIM), BF16),
            pltpu.VMEM((tq, ATTN_DIM), BF16),
        ],
        compiler_params=_params(2),
        name="attn_out",
    )(sink, q, k, k, k, v, v, v, ga, rnn, mg, x, mod, w_attn, w_rnn, w_out, final_gain)


def _trunk(x, batch_off, mods, tables, layers, final_gain):
    cos, sin = tables
    seq = x.shape[1]
    cos, sin = cos[:seq], sin[:seq]
    depth = len(layers)
    for l, p in enumerate(layers):
        mod = mods[l]
        q, k, v, ga, rx, gr, mg = _project(x, mod, p["norm_gain"], p["w_in"], p["w_merge"],
                                           p["b_merge"], cos, sin, batch_off)
        h_bwd = _rglru(rx, p["conv_w"], p["conv_b"], p["wg_bwd"], p["bg_bwd"], p["lam_bwd"],
                       reverse=True)
        rnn = _rglru(rx, p["conv_w"], p["conv_b"], p["wg_fwd"], p["bg_fwd"], p["lam_fwd"],
                     reverse=False, h_bwd=h_bwd, gate=gr)
        x = _attend_and_mix(p["sink"], q, k, v, ga, rnn, mg, x, mod, p["w_attn"], p["w_rnn"],
                            p["w_out"], final_gain, batch_off, final=(l == depth - 1))
    return x


def kernel(x_prompt, x_sample, c_prompt, c_sample, norm_gain, w_ada, b_ada, w_in, attn_sink, conv_w, conv_b, rg_w_a, rg_b_a, rg_w_x, rg_b_x, rg_lambda, w_attn_proj, w_rnn_proj, w_merge, b_merge, w_out, final_gain):
    depth = w_in.shape[0]
    n_prompt = x_prompt.shape[0]
    cond = jnp.concatenate([c_prompt, c_sample], axis=0)
    mods = _modulation(cond, w_ada, b_ada)
    mods = mods.reshape(depth, cond.shape[0], 1, 3 * D_MODEL)
    tables = _rope_tables(max(x_prompt.shape[1], x_sample.shape[1]))

    w_gate = jnp.concatenate([rg_w_a, rg_w_x], axis=-1).astype(BF16)
    b_gate = jnp.concatenate(
        [rg_b_a.reshape(depth, 2, N_RNN_BLOCKS, RNN_BLOCK),
         rg_b_x.reshape(depth, 2, N_RNN_BLOCKS, RNN_BLOCK)], axis=-1)
    layers = []
    for l in range(depth):
        layers.append(dict(
            norm_gain=norm_gain[l].reshape(1, D_MODEL),
            w_in=w_in[l].astype(BF16),
            w_merge=w_merge[l].astype(BF16),
            b_merge=b_merge[l].reshape(1, 2 * D_MODEL),
            sink=attn_sink[l],
            conv_w=conv_w[l],
            conv_b=conv_b[l].reshape(1, D_RNN),
            wg_fwd=w_gate[l, 0], wg_bwd=w_gate[l, 1],
            bg_fwd=b_gate[l, 0][None], bg_bwd=b_gate[l, 1][None],
            lam_fwd=rg_lambda[l, 0].reshape(1, D_RNN), lam_bwd=rg_lambda[l, 1].reshape(1, D_RNN),
            w_attn=w_attn_proj[l].astype(BF16),
            w_rnn=w_rnn_proj[l].astype(BF16),
            w_out=w_out[l].astype(BF16),
        ))
    fg = final_gain.reshape(1, D_MODEL)
    y_prompt = _trunk(x_prompt, 0, mods, tables, layers, fg)
    y_sample = _trunk(x_sample, n_prompt, mods, tables, layers, fg)
    return (y_prompt, y_sample)
```

```python
import functools
import math

import jax
import jax.numpy as jnp
from jax import lax
from jax.experimental import pallas as pl
from jax.experimental.pallas import tpu as pltpu

D_MODEL = 1024
HEAD_DIM = 128
N_HEADS = 8
N_KV_HEADS = 2
GROUP = N_HEADS // N_KV_HEADS
ATTN_DIM = N_HEADS * HEAD_DIM
KV_DIM = N_KV_HEADS * HEAD_DIM
BLOCK = 128
ROPE_THETA = 10000.0
D_RNN = 3 * D_MODEL // 2
N_RNN_BLOCKS = 12
RNN_BLOCK = D_RNN // N_RNN_BLOCKS
CONV_WIDTH = 4
CONV_LEFT = 2
RGLRU_C = 8.0
EPS = 1e-6
MASK_VALUE = -1e30

SUBLANES = 8
LANES = 128
MXU_COLS = 256
VMEM_LIMIT_BYTES = 56 * 1024 * 1024

F32 = jnp.float32
BF16 = jnp.bfloat16


def _sigmoid(x):
    return 0.5 * jnp.tanh(0.5 * x) + 0.5


def _silu(x):
    return x * _sigmoid(x)


def _resident(shape):
    zeros = (0,) * len(shape)
    return pl.BlockSpec(shape, lambda *_: zeros, pipeline_mode=pl.Buffered(1))


def _params(n_axes):
    return pltpu.CompilerParams(dimension_semantics=("arbitrary",) * n_axes,
                                vmem_limit_bytes=VMEM_LIMIT_BYTES)


def _mod_kernel(c_ref, w_ref, b_ref, o_ref):
    c = c_ref[...]
    s = c * jax.nn.sigmoid(c)
    o_ref[...] = jnp.dot(s, w_ref[...], precision=lax.Precision.HIGHEST,
                         preferred_element_type=F32) + b_ref[...]


def _modulation(cond, w_ada, b_ada):
    depth = w_ada.shape[0]
    nb = cond.shape[0]
    n_col = 3 * D_MODEL // D_MODEL
    return pl.pallas_call(
        _mod_kernel,
        grid=(depth, n_col),
        in_specs=[
            pl.BlockSpec((nb, D_MODEL), lambda l, j: (0, 0)),
            pl.BlockSpec((None, D_MODEL, D_MODEL), lambda l, j: (l, 0, j)),
            pl.BlockSpec((None, 1, D_MODEL), lambda l, j: (l, 0, j)),
        ],
        out_specs=pl.BlockSpec((None, nb, D_MODEL), lambda l, j: (l, 0, j)),
        out_shape=jax.ShapeDtypeStruct((depth, nb, 3 * D_MODEL), F32),
        compiler_params=_params(2),
        name="adaln_mod",
    )(cond, w_ada, b_ada.reshape(depth, 1, 3 * D_MODEL))


def _rope_kernel(cos_ref, sin_ref):
    rows = cos_ref.shape[0]
    t = lax.broadcasted_iota(jnp.int32, (rows, HEAD_DIM), 0) + pl.program_id(0) * rows
    j = lax.broadcasted_iota(jnp.int32, (rows, HEAD_DIM), 1)
    half = HEAD_DIM // 2
    jj = jnp.where(j < half, j, j - half).astype(F32)
    inv_freq = jnp.exp(jj * (-2.0 / HEAD_DIM * math.log(ROPE_THETA)))
    ang = t.astype(F32) * inv_freq
    s = jnp.sin(ang)
    cos_ref[...] = jnp.cos(ang)
    sin_ref[...] = jnp.where(j < half, -s, s)


def _rope_tables(seq):
    rows = min(seq, 512)
    return pl.pallas_call(
        _rope_kernel,
        grid=(seq // rows,),
        out_specs=[pl.BlockSpec((rows, HEAD_DIM), lambda i: (i, 0))] * 2,
        out_shape=[jax.ShapeDtypeStruct((seq, HEAD_DIM), F32)] * 2,
        compiler_params=_params(1),
        name="rope_tables",
    )()


NORM_ROWS = 32


def _proj_kernel(x_ref, mod_ref, ng_ref, win_ref, wm_ref, bm_ref, cos_ref, sin_ref,
                 q_ref, k_ref, v_ref, ga_ref, rx_ref, gr_ref, mg_ref, h_scr, *, c_len):
    ts = x_ref.shape[0]
    gain = ng_ref[...]
    shift = mod_ref[:, 0:D_MODEL]
    scale1 = 1.0 + mod_ref[:, D_MODEL:2 * D_MODEL]

    def norm_rows(r, carry):
        rows = pl.ds(pl.multiple_of(r * NORM_ROWS, NORM_ROWS), NORM_ROWS)
        x = x_ref[rows, :]
        ms = jnp.mean(x * x, axis=-1, keepdims=True)
        y = x * lax.rsqrt(ms + EPS) * gain
        h_scr[rows, :] = (y * scale1 + shift).astype(BF16)
        return carry

    lax.fori_loop(0, ts // NORM_ROWS, norm_rows, 0)

    cos = cos_ref[...]
    sin = sin_ref[...]

    def rotary(p):
        return p * cos + pltpu.roll(p, HEAD_DIM // 2, axis=1) * sin

    def chunk(w_ref, c0):
        return jnp.dot(h_scr[...], w_ref[:, c0:c0 + MXU_COLS], preferred_element_type=F32)

    qk_scale = HEAD_DIM ** -0.5
    col = 0
    for c in range(ATTN_DIM // MXU_COLS):
        p = chunk(win_ref, col + c * MXU_COLS)
        for hh in range(MXU_COLS // HEAD_DIM):
            dst = slice(c * MXU_COLS + hh * HEAD_DIM, c * MXU_COLS + (hh + 1) * HEAD_DIM)
            q_ref[:, dst] = (rotary(p[:, hh * HEAD_DIM:(hh + 1) * HEAD_DIM]) * qk_scale).astype(BF16)
    col += ATTN_DIM
    p = chunk(win_ref, col)
    for hh in range(N_KV_HEADS):
        dst = slice(hh * HEAD_DIM, (hh + 1) * HEAD_DIM)
        k_ref[:, dst] = rotary(p[:, dst]).astype(BF16)
    col += KV_DIM
    v_ref[...] = chunk(win_ref, col).astype(BF16)
    col += KV_DIM
    for c in range(ATTN_DIM // MXU_COLS):
        dst = slice(c * MXU_COLS, (c + 1) * MXU_COLS)
        ga_ref[:, dst] = _silu(chunk(win_ref, col + c * MXU_COLS)).astype(BF16)
    col += ATTN_DIM
    pitch = rx_ref.shape[0] * c_len // ts
    for c in range(D_RNN // MXU_COLS):
        dst = slice(c * MXU_COLS, (c + 1) * MXU_COLS)
        p = chunk(win_ref, col + c * MXU_COLS)
        for kk in range(ts // c_len):
            rx_ref[kk * pitch:kk * pitch + c_len, dst] = p[kk * c_len:(kk + 1) * c_len]
            rx_ref[kk * pitch + c_len:(kk + 1) * pitch, dst] = jnp.zeros(
                (pitch - c_len, MXU_COLS), F32)
    col += D_RNN
    for c in range(D_RNN // MXU_COLS):
        dst = slice(c * MXU_COLS, (c + 1) * MXU_COLS)
        gr_ref[:, dst] = _silu(chunk(win_ref, col + c * MXU_COLS)).astype(BF16)
    for c in range(2 * D_MODEL // MXU_COLS):
        dst = slice(c * MXU_COLS, (c + 1) * MXU_COLS)
        mg_ref[:, dst] = _sigmoid(chunk(wm_ref, c * MXU_COLS) + bm_ref[:, dst]).astype(BF16)


def _seq_tile(seq, want):
    return want if seq % want == 0 else seq


def _project(x, mod, norm_gain, w_in, w_merge, b_merge, cos, sin, batch_off):
    bsz, seq, _ = x.shape
    ts = _seq_tile(seq, 512)
    d_in = w_in.shape[1]
    c_len, pitch = _chunking(seq)
    assert ts % c_len == 0
    rx_rows = ts // c_len * pitch

    def tile(width, rows=ts):
        return pl.BlockSpec((None, rows, width), lambda b, t: (b, t, 0))

    outs = [(ATTN_DIM, BF16, seq, ts), (KV_DIM, BF16, seq, ts), (KV_DIM, BF16, seq, ts),
            (ATTN_DIM, BF16, seq, ts), (D_RNN, F32, N_CHUNKS * pitch, rx_rows),
            (D_RNN, BF16, seq, ts), (2 * D_MODEL, BF16, seq, ts)]
    return pl.pallas_call(
        functools.partial(_proj_kernel, c_len=c_len),
        grid=(bsz, seq // ts),
        in_specs=[
            tile(D_MODEL),
            pl.BlockSpec((None, 1, 3 * D_MODEL), lambda b, t: (b + batch_off, 0, 0)),
            _resident((1, D_MODEL)),
            _resident((D_MODEL, d_in)),
            _resident((D_MODEL, 2 * D_MODEL)),
            _resident((1, 2 * D_MODEL)),
            pl.BlockSpec((ts, HEAD_DIM), lambda b, t: (t, 0)),
            pl.BlockSpec((ts, HEAD_DIM), lambda b, t: (t, 0)),
        ],
        out_specs=[tile(w, rows) for w, _, _, rows in outs],
        out_shape=[jax.ShapeDtypeStruct((bsz, total, w), dt) for w, dt, total, _ in outs],
        scratch_shapes=[pltpu.VMEM((ts, D_MODEL), BF16)],
        compiler_params=_params(2),
        name="in_proj",
    )(x, mod, norm_gain, w_in, w_merge, b_merge, cos, sin)


N_SETS = 4
N_CHUNKS = N_SETS * SUBLANES
STEP_ROWS = N_SETS * SUBLANES
PITCH_PAD = SUBLANES
EXT_SLOTS = CONV_WIDTH - 1
PHASE_ROWS = 256
SCAN_STEPS = 4
GATHER_STEPS = 8
TINY = 1e-30


def _chunking(seq):
    c_len = seq // N_CHUNKS
    assert seq % N_CHUNKS == 0 and (c_len * STEP_ROWS) % PHASE_ROWS == 0
    assert c_len % SCAN_STEPS == 0 and c_len % GATHER_STEPS == 0
    assert c_len % (2 * SUBLANES) == 0
    return c_len, c_len + PITCH_PAD


def _sublane_scan(a, b, reverse):
    row = lax.broadcasted_iota(jnp.int32, a.shape, 0)
    for d in (1, 2, 4):
        if reverse:
            keep, shift = row < SUBLANES - d, SUBLANES - d
        else:
            keep, shift = row >= d, d
        a_s = pltpu.roll(a, shift, axis=0)
        b_s = pltpu.roll(b, shift, axis=0)
        b = jnp.where(keep, a * b_s + b, b)
        a = jnp.where(keep, a * a_s, a)
    return a, b


def _rglru_kernel(rx_ref, gr_ref, cw_ref, cb_ref, wg_ref, lam_ref, out_ref,
                  xs, ext, a_scr, b_scr, hs):
    seq = out_ref.shape[0]
    c_len, pitch = _chunking(seq)
    row8 = lax.broadcasted_iota(jnp.int32, (SUBLANES, LANES), 0)
    zero8 = jnp.zeros((SUBLANES, LANES), F32)

    def sets(v):
        return [v[s * SUBLANES:(s + 1) * SUBLANES] for s in range(N_SETS)]

    def slot(i):
        return slice((i + CONV_LEFT) * STEP_ROWS, (i + CONV_LEFT + 1) * STEP_ROWS)

    def gather(n, carry):
        i0 = n * GATHER_STEPS
        vals = []
        for kk in range(GATHER_STEPS):
            for s in range(N_SETS):
                vals.append(rx_ref[pl.ds(s * SUBLANES * pitch + i0 + kk, SUBLANES, stride=pitch), :])
        dst = pl.multiple_of((i0 + CONV_LEFT) * STEP_ROWS, STEP_ROWS)
        ext[pl.ds(dst, GATHER_STEPS * STEP_ROWS), :] = jnp.concatenate(vals, axis=0)
        return carry

    lax.fori_loop(0, c_len // GATHER_STEPS, gather, 0)

    for back in (1, 2):
        tail = [pltpu.roll(v, 1, axis=0) for v in sets(ext[slot(c_len - back), :])]
        ext[slot(-back), :] = jnp.concatenate(
            [jnp.where(row8 == 0, tail[s - 1] if s > 0 else zero8, tail[s])
             for s in range(N_SETS)], axis=0)
    head = [pltpu.roll(v, SUBLANES - 1, axis=0) for v in sets(ext[slot(0), :])]
    ext[slot(c_len), :] = jnp.concatenate(
        [jnp.where(row8 == SUBLANES - 1, head[s + 1] if s < N_SETS - 1 else zero8, head[s])
         for s in range(N_SETS)], axis=0)

    neg = -lam_ref[...]
    softplus = jnp.maximum(neg, 0.0) + jnp.log(1.0 + jnp.exp(-jnp.abs(neg)))
    kexp = softplus * (-RGLRU_C * 0.5 * math.log2(math.e))
    cw = cw_ref[...]
    cb = cb_ref[...]
    ones16 = jnp.ones((PHASE_ROWS, RNN_BLOCK), BF16)

    def phase(n, carry):
        src = n * PHASE_ROWS
        dst = pl.ds(pl.multiple_of(src, PHASE_ROWS), PHASE_ROWS)
        y = cb
        for tap in range(CONV_WIDTH):
            rows = pl.ds(pl.multiple_of(src + tap * STEP_ROWS, STEP_ROWS), PHASE_ROWS)
            y = y + ext[rows, :] * cw[tap:tap + 1, :]
        lhs = jnp.concatenate([y.astype(BF16), ones16], axis=1)
        for d in range(2):
            g = jnp.dot(lhs, wg_ref[d], preferred_element_type=F32)
            ta = jnp.tanh(g[:, :RNN_BLOCK])
            ti = jnp.tanh(g[:, RNN_BLOCK:])
            k = kexp[d:d + 1, :]
            a = jnp.exp2(k * ta + k)
            z = 1.0 - a * a
            m = z * lax.rsqrt(jnp.maximum(z, TINY))
            a_scr[d, dst, :] = a
            b_scr[d, dst, :] = m * ((ti + 1.0) * y)
        return carry

    lax.fori_loop(0, seq // PHASE_ROWS, phase, 0, unroll=4)

    n_blocks = c_len // SCAN_STEPS
    block_rows = SCAN_STEPS * STEP_ROWS

    for d in range(2):
        reverse = d == 1
        order = range(SCAN_STEPS - 1, -1, -1) if reverse else range(SCAN_STEPS)

        def block(n):
            blk = (n_blocks - 1 - n) if reverse else n
            return blk, pl.ds(pl.multiple_of(blk * block_rows, block_rows), block_rows)

        def piece(v, kk, s):
            r0 = (kk * N_SETS + s) * SUBLANES
            return v[r0:r0 + SUBLANES]

        def summarise(n, carry):
            _, rows = block(n)
            av = a_scr[d, rows, :]
            bv = b_scr[d, rows, :]
            carry = list(carry)
            for kk in order:
                for s in range(N_SETS):
                    a = piece(av, kk, s)
                    h, p = carry[s]
                    carry[s] = (a * h + piece(bv, kk, s), a * p)
            return tuple(carry)

        ones8 = jnp.ones((SUBLANES, LANES), F32)
        ends = lax.fori_loop(0, n_blocks, summarise,
                             tuple((zero8, ones8) for _ in range(N_SETS)))

        edge = zero8
        start = [None] * N_SETS
        for s in (range(N_SETS - 1, -1, -1) if reverse else range(N_SETS)):
            p_cum, h_cum = _sublane_scan(ends[s][1], ends[s][0], reverse)
            after = p_cum * edge + h_cum
            if reverse:
                start[s] = jnp.where(row8 == SUBLANES - 1, edge,
                                     pltpu.roll(after, SUBLANES - 1, axis=0))
                edge = jnp.broadcast_to(after[0:1, :], (SUBLANES, LANES))
            else:
                start[s] = jnp.where(row8 == 0, edge, pltpu.roll(after, 1, axis=0))
                edge = jnp.broadcast_to(after[SUBLANES - 1:SUBLANES, :], (SUBLANES, LANES))

        def rescan(n, carry):
            blk, rows = block(n)
            av = a_scr[d, rows, :]
            bv = b_scr[d, rows, :]
            hv = hs[rows, :] if reverse else None
            carry = list(carry)
            out = [None] * (SCAN_STEPS * N_SETS)
            for kk in order:
                for s in range(N_SETS):
                    h = piece(av, kk, s) * carry[s] + piece(bv, kk, s)
                    carry[s] = h
                    if reverse:
                        i = blk * SCAN_STEPS + kk
                        xs[pl.ds(s * SUBLANES * pitch + i, SUBLANES, stride=pitch), :] = (
                            h + piece(hv, kk, s))
                    else:
                        out[kk * N_SETS + s] = h
            if not reverse:
                hs[rows, :] = jnp.concatenate(out, axis=0)
            return tuple(carry)

        lax.fori_loop(0, n_blocks, rescan, tuple(start))

    for ch in range(N_CHUNKS):
        rows = slice(ch * c_len, (ch + 1) * c_len)
        h = xs[ch * pitch:ch * pitch + c_len, :]
        out_ref[rows, :] = (h * gr_ref[rows, :].astype(F32)).astype(BF16)


def _rglru(rnn_x, gate, conv_w, conv_b, w_gate, lam):
    bsz, seq, _ = gate.shape
    c_len, pitch = _chunking(seq)
    assert rnn_x.shape[1] == N_CHUNKS * pitch

    def col_block(rows):
        return pl.BlockSpec((None, rows, RNN_BLOCK), lambda b, c: (b, 0, c))

    return pl.pallas_call(
        _rglru_kernel,
        grid=(bsz, N_RNN_BLOCKS),
        in_specs=[
            col_block(N_CHUNKS * pitch),
            col_block(seq),
            pl.BlockSpec((CONV_WIDTH, RNN_BLOCK), lambda b, c: (0, c)),
            pl.BlockSpec((1, RNN_BLOCK), lambda b, c: (0, c)),
            pl.BlockSpec((2, None, 2 * RNN_BLOCK, 2 * RNN_BLOCK), lambda b, c: (0, c, 0, 0)),
            pl.BlockSpec((2, RNN_BLOCK), lambda b, c: (0, c)),
        ],
        out_specs=col_block(seq),
        out_shape=jax.ShapeDtypeStruct((bsz, seq, D_RNN), BF16),
        scratch_shapes=[
            pltpu.VMEM((N_CHUNKS * pitch, RNN_BLOCK), F32),
            pltpu.VMEM(((c_len + EXT_SLOTS) * STEP_ROWS, RNN_BLOCK), F32),
            pltpu.VMEM((2, seq, RNN_BLOCK), F32),
            pltpu.VMEM((2, seq, RNN_BLOCK), F32),
            pltpu.VMEM((seq, RNN_BLOCK), F32),
        ],
        compiler_params=_params(2),
        name="rglru",
    )(rnn_x, gate, conv_w, conv_b, w_gate, lam)


def _attn_out_kernel(sink_ref, q_ref, k_ref, kp_ref, kn_ref, v_ref, vp_ref, vn_ref, ga_ref,
                     rnn_ref, mg_ref, x_ref, mod_ref, wa_ref, wr_ref, wo_ref, fg_ref,
                     o_ref, kext, vext, attn_scr, *, final):
    tq = q_ref.shape[0]
    t = pl.program_id(1)
    nt = pl.num_programs(1)
    n_blk = tq // BLOCK

    kext[0:BLOCK, :] = kp_ref[...]
    kext[BLOCK:BLOCK + tq, :] = k_ref[...]
    kext[BLOCK + tq:, :] = kn_ref[...]
    vext[0:BLOCK, :] = vp_ref[...]
    vext[BLOCK:BLOCK + tq, :] = v_ref[...]
    vext[BLOCK + tq:, :] = vn_ref[...]

    band = 3 * BLOCK
    qi = lax.broadcasted_iota(jnp.int32, (BLOCK, band), 0)
    kj = lax.broadcasted_iota(jnp.int32, (BLOCK, band), 1)
    in_window = (kj >= qi) & (kj <= qi + 2 * BLOCK)
    head = lax.broadcasted_iota(jnp.int32, (GROUP, BLOCK, 1), 0)

    for blk in range(n_blk):
        rows = slice(blk * BLOCK, (blk + 1) * BLOCK)
        band_rows = slice(blk * BLOCK, blk * BLOCK + band)
        first = (t == 0) if blk == 0 else False
        last = (t == nt - 1) if blk == n_blk - 1 else False
        lo = jnp.where(first, BLOCK, 0)
        hi = jnp.where(last, 2 * BLOCK, band)
        valid = in_window & (kj >= lo) & (kj < hi)
        for g in range(N_KV_HEADS):
            kv_cols = slice(g * HEAD_DIM, (g + 1) * HEAD_DIM)
            qg = jnp.concatenate(
                [q_ref[rows, (g * GROUP + h) * HEAD_DIM:(g * GROUP + h + 1) * HEAD_DIM]
                 for h in range(GROUP)], axis=0)
            s = lax.dot_general(qg, kext[band_rows, kv_cols], (((1,), (1,)), ((), ())),
                                preferred_element_type=F32)
            s = s.reshape(GROUP, BLOCK, band)
            s = jnp.where(valid[None], s, MASK_VALUE)
            sink = jnp.full((GROUP, BLOCK, 1), sink_ref[g * GROUP], F32)
            for h in range(1, GROUP):
                sink = jnp.where(head == h, sink_ref[g * GROUP + h], sink)
            m = jnp.maximum(jnp.max(s, axis=-1, keepdims=True), sink)
            p = jnp.exp(s - m)
            denom = jnp.sum(p, axis=-1, keepdims=True) + jnp.exp(sink - m)
            o = jnp.dot(p.reshape(GROUP * BLOCK, band).astype(BF16), vext[band_rows, kv_cols],
                        preferred_element_type=F32)
            o = o.reshape(GROUP, BLOCK, HEAD_DIM) / denom
            for h in range(GROUP):
                cols = slice((g * GROUP + h) * HEAD_DIM, (g * GROUP + h + 1) * HEAD_DIM)
                attn_scr[rows, cols] = (o[h] * ga_ref[rows, cols].astype(F32)).astype(BF16)

    a = jnp.dot(attn_scr[...], wa_ref[...], preferred_element_type=F32)
    r = jnp.dot(rnn_ref[...], wr_ref[...], preferred_element_type=F32)
    mixed = (mg_ref[:, 0:D_MODEL].astype(F32) * a + mg_ref[:, D_MODEL:].astype(F32) * r)
    y = jnp.dot(mixed.astype(BF16), wo_ref[...], preferred_element_type=F32)
    out = x_ref[...] + mod_ref[:, 2 * D_MODEL:] * y
    if final:
        ms = jnp.mean(out * out, axis=-1, keepdims=True)
        out = out * lax.rsqrt(ms + EPS) * fg_ref[...]
    o_ref[...] = out


def _attend_and_mix(sink, q, k, v, ga, rnn, mg, x, mod, w_attn, w_rnn, w_out, final_gain,
                    batch_off, final):
    bsz, seq, _ = x.shape
    tq = _seq_tile(seq, 512)
    per = tq // BLOCK
    n_blocks = seq // BLOCK

    def tile(width):
        return pl.BlockSpec((None, tq, width), lambda b, t: (b, t, 0))

    prev_kv = pl.BlockSpec((None, BLOCK, KV_DIM), lambda b, t: (b, jnp.maximum(t * per - 1, 0), 0))
    next_kv = pl.BlockSpec((None, BLOCK, KV_DIM),
                           lambda b, t: (b, jnp.minimum((t + 1) * per, n_blocks - 1), 0))
    return pl.pallas_call(
        functools.partial(_attn_out_kernel, final=final),
        grid=(bsz, seq // tq),
        in_specs=[
            pl.BlockSpec(memory_space=pltpu.SMEM),
            tile(ATTN_DIM),
            tile(KV_DIM), prev_kv, next_kv,
            tile(KV_DIM), prev_kv, next_kv,
            tile(ATTN_DIM),
            tile(D_RNN),
            tile(2 * D_MODEL),
            tile(D_MODEL),
            pl.BlockSpec((None, 1, 3 * D_MODEL), lambda b, t: (b + batch_off, 0, 0)),
            _resident((ATTN_DIM, D_MODEL)),
            _resident((D_RNN, D_MODEL)),
            _resident((D_MODEL, D_MODEL)),
            _resident((1, D_MODEL)),
        ],
        out_specs=tile(D_MODEL),
        out_shape=jax.ShapeDtypeStruct((bsz, seq, D_MODEL), F32),
        scratch_shapes=[
            pltpu.VMEM((tq + 2 * BLOCK, KV_DIM), BF16),
            pltpu.VMEM((tq + 2 * BLOCK, KV_DIM), BF16),
            pltpu.VMEM((tq, ATTN_DIM), BF16),
        ],
        compiler_params=_params(2),
        name="attn_out",
    )(sink, q, k, k, k, v, v, v, ga, rnn, mg, x, mod, w_attn, w_rnn, w_out, final_gain)


def _trunk(x, batch_off, mods, tables, layers, final_gain):
    cos, sin = tables
    seq = x.shape[1]
    cos, sin = cos[:seq], sin[:seq]
    depth = len(layers)
    for l, p in enumerate(layers):
        mod = mods[l]
        q, k, v, ga, rx, gr, mg = _project(x, mod, p["norm_gain"], p["w_in"], p["w_merge"],
                                           p["b_merge"], cos, sin, batch_off)
        rnn = _rglru(rx, gr, p["conv_w"], p["conv_b"], p["w_gate"], p["lam"])
        x = _attend_and_mix(p["sink"], q, k, v, ga, rnn, mg, x, mod, p["w_attn"], p["w_rnn"],
                            p["w_out"], final_gain, batch_off, final=(l == depth - 1))
    return x


def kernel(x_prompt, x_sample, c_prompt, c_sample, norm_gain, w_ada, b_ada, w_in, attn_sink, conv_w, conv_b, rg_w_a, rg_b_a, rg_w_x, rg_b_x, rg_lambda, w_attn_proj, w_rnn_proj, w_merge, b_merge, w_out, final_gain):
    depth = w_in.shape[0]
    n_prompt = x_prompt.shape[0]
    cond = jnp.concatenate([c_prompt, c_sample], axis=0)
    mods = _modulation(cond, w_ada, b_ada)
    mods = mods.reshape(depth, cond.shape[0], 1, 3 * D_MODEL)
    tables = _rope_tables(max(x_prompt.shape[1], x_sample.shape[1]))

    w_gate = jnp.concatenate([rg_w_a, rg_w_x], axis=-1).astype(BF16)
    b_half = 0.5 * jnp.concatenate(
        [rg_b_a.reshape(depth, 2, N_RNN_BLOCKS, 1, RNN_BLOCK),
         rg_b_x.reshape(depth, 2, N_RNN_BLOCKS, 1, RNN_BLOCK)], axis=-1)
    b_hi = b_half.astype(BF16)
    b_lo = (b_half - b_hi.astype(F32)).astype(BF16)
    w_gate = jnp.concatenate(
        [w_gate, b_hi, b_lo,
         jnp.zeros((depth, 2, N_RNN_BLOCKS, RNN_BLOCK - 2, 2 * RNN_BLOCK), BF16)], axis=-2)
    layers = []
    for l in range(depth):
        layers.append(dict(
            norm_gain=norm_gain[l].reshape(1, D_MODEL),
            w_in=w_in[l].astype(BF16),
            w_merge=w_merge[l].astype(BF16),
            b_merge=b_merge[l].reshape(1, 2 * D_MODEL),
            sink=attn_sink[l],
            conv_w=0.5 * conv_w[l],
            conv_b=0.5 * conv_b[l].reshape(1, D_RNN),
            w_gate=w_gate[l], lam=rg_lambda[l],
            w_attn=w_attn_proj[l].astype(BF16),
            w_rnn=w_rnn_proj[l].astype(BF16),
            w_out=w_out[l].astype(BF16),
        ))
    fg = final_gain.reshape(1, D_MODEL)
    y_prompt = _trunk(x_prompt, 0, mods, tables, layers, fg)
    y_sample = _trunk(x_sample, n_prompt, mods, tables, layers, fg)
    return (y_prompt, y_sample)
```

```python
import functools
import math

import jax
import jax.numpy as jnp
from jax import lax
from jax.experimental import pallas as pl
from jax.experimental.pallas import tpu as pltpu

D_MODEL = 1024
HEAD_DIM = 128
N_HEADS = 8
N_KV_HEADS = 2
GROUP = N_HEADS // N_KV_HEADS
ATTN_DIM = N_HEADS * HEAD_DIM
KV_DIM = N_KV_HEADS * HEAD_DIM
BLOCK = 128
ROPE_THETA = 10000.0
D_RNN = 3 * D_MODEL // 2
N_RNN_BLOCKS = 12
RNN_BLOCK = D_RNN // N_RNN_BLOCKS
CONV_WIDTH = 4
CONV_LEFT = 2
RGLRU_C = 8.0
EPS = 1e-6
MASK_VALUE = -1e30

SUBLANES = 8
LANES = 128
MXU_COLS = 256
VMEM_LIMIT_BYTES = 56 * 1024 * 1024

F32 = jnp.float32
BF16 = jnp.bfloat16


def _sigmoid(x):
    return 0.5 * jnp.tanh(0.5 * x) + 0.5


def _silu(x):
    return x * _sigmoid(x)


def _resident(shape):
    zeros = (0,) * len(shape)
    return pl.BlockSpec(shape, lambda *_: zeros, pipeline_mode=pl.Buffered(1))


def _params(n_axes):
    return pltpu.CompilerParams(dimension_semantics=("arbitrary",) * n_axes,
                                vmem_limit_bytes=VMEM_LIMIT_BYTES)


def _mod_kernel(c_ref, w_ref, b_ref, o_ref):
    c = c_ref[...]
    s = c * jax.nn.sigmoid(c)
    o_ref[...] = jnp.dot(s, w_ref[...], precision=lax.Precision.HIGHEST,
                         preferred_element_type=F32) + b_ref[...]


def _modulation(cond, w_ada, b_ada):
    depth = w_ada.shape[0]
    nb = cond.shape[0]
    n_col = 3 * D_MODEL // D_MODEL
    return pl.pallas_call(
        _mod_kernel,
        grid=(depth, n_col),
        in_specs=[
            pl.BlockSpec((nb, D_MODEL), lambda l, j: (0, 0)),
            pl.BlockSpec((None, D_MODEL, D_MODEL), lambda l, j: (l, 0, j)),
            pl.BlockSpec((None, 1, D_MODEL), lambda l, j: (l, 0, j)),
        ],
        out_specs=pl.BlockSpec((None, nb, D_MODEL), lambda l, j: (l, 0, j)),
        out_shape=jax.ShapeDtypeStruct((depth, nb, 3 * D_MODEL), F32),
        compiler_params=_params(2),
        name="adaln_mod",
    )(cond, w_ada, b_ada.reshape(depth, 1, 3 * D_MODEL))


def _rope_kernel(cos_ref, sin_ref):
    rows = cos_ref.shape[0]
    t = lax.broadcasted_iota(jnp.int32, (rows, HEAD_DIM), 0) + pl.program_id(0) * rows
    j = lax.broadcasted_iota(jnp.int32, (rows, HEAD_DIM), 1)
    half = HEAD_DIM // 2
    jj = jnp.where(j < half, j, j - half).astype(F32)
    inv_freq = jnp.exp(jj * (-2.0 / HEAD_DIM * math.log(ROPE_THETA)))
    ang = t.astype(F32) * inv_freq
    s = jnp.sin(ang)
    cos_ref[...] = jnp.cos(ang)
    sin_ref[...] = jnp.where(j < half, -s, s)


def _rope_tables(seq):
    rows = min(seq, 512)
    return pl.pallas_call(
        _rope_kernel,
        grid=(seq // rows,),
        out_specs=[pl.BlockSpec((rows, HEAD_DIM), lambda i: (i, 0))] * 2,
        out_shape=[jax.ShapeDtypeStruct((seq, HEAD_DIM), F32)] * 2,
        compiler_params=_params(1),
        name="rope_tables",
    )()


NORM_ROWS = 32


def _adaln(x, gain, mod_ref):
    ms = jnp.mean(x * x, axis=-1, keepdims=True)
    y = x * lax.rsqrt(ms + EPS) * gain
    return y * (1.0 + mod_ref[:, D_MODEL:2 * D_MODEL]) + mod_ref[:, 0:D_MODEL]


def _prenorm_kernel(x_ref, mod_ref, ng_ref, h_ref):
    gain = ng_ref[...]

    def norm_rows(r, carry):
        rows = pl.ds(pl.multiple_of(r * NORM_ROWS, NORM_ROWS), NORM_ROWS)
        h_ref[rows, :] = _adaln(x_ref[rows, :], gain, mod_ref).astype(BF16)
        return carry

    lax.fori_loop(0, x_ref.shape[0] // NORM_ROWS, norm_rows, 0)


def _prenorm(x, mod, norm_gain, batch_off):
    bsz, seq, _ = x.shape
    ts = _seq_tile(seq, 512)
    tile = pl.BlockSpec((None, ts, D_MODEL), lambda b, t: (b, t, 0))
    return pl.pallas_call(
        _prenorm_kernel,
        grid=(bsz, seq // ts),
        in_specs=[tile,
                  pl.BlockSpec((None, 1, 3 * D_MODEL), lambda b, t: (b + batch_off, 0, 0)),
                  _resident((1, D_MODEL))],
        out_specs=tile,
        out_shape=jax.ShapeDtypeStruct((bsz, seq, D_MODEL), BF16),
        compiler_params=_params(2),
        name="prenorm",
    )(x, mod, norm_gain)


def _proj_kernel(h_ref, win_ref, wm_ref, bm_ref, cos_ref, sin_ref,
                 q_ref, k_ref, v_ref, ga_ref, rx_ref, gr_ref, mg_ref, *, c_len):
    ts = h_ref.shape[0]
    cos = cos_ref[...]
    sin = sin_ref[...]

    def rotary(p):
        return p * cos + pltpu.roll(p, HEAD_DIM // 2, axis=1) * sin

    def chunk(w_ref, c0):
        return jnp.dot(h_ref[...], w_ref[:, c0:c0 + MXU_COLS], preferred_element_type=F32)

    qk_scale = HEAD_DIM ** -0.5
    col = 0
    for c in range(ATTN_DIM // MXU_COLS):
        p = chunk(win_ref, col + c * MXU_COLS)
        for hh in range(MXU_COLS // HEAD_DIM):
            dst = slice(c * MXU_COLS + hh * HEAD_DIM, c * MXU_COLS + (hh + 1) * HEAD_DIM)
            q_ref[:, dst] = (rotary(p[:, hh * HEAD_DIM:(hh + 1) * HEAD_DIM]) * qk_scale).astype(BF16)
    col += ATTN_DIM
    p = chunk(win_ref, col)
    for hh in range(N_KV_HEADS):
        dst = slice(hh * HEAD_DIM, (hh + 1) * HEAD_DIM)
        k_ref[:, dst] = rotary(p[:, dst]).astype(BF16)
    col += KV_DIM
    v_ref[...] = chunk(win_ref, col).astype(BF16)
    col += KV_DIM
    for c in range(ATTN_DIM // MXU_COLS):
        dst = slice(c * MXU_COLS, (c + 1) * MXU_COLS)
        ga_ref[:, dst] = _silu(chunk(win_ref, col + c * MXU_COLS)).astype(BF16)
    col += ATTN_DIM
    pitch = rx_ref.shape[0] * c_len // ts
    for c in range(D_RNN // MXU_COLS):
        dst = slice(c * MXU_COLS, (c + 1) * MXU_COLS)
        p = chunk(win_ref, col + c * MXU_COLS)
        for kk in range(ts // c_len):
            rx_ref[kk * pitch:kk * pitch + c_len, dst] = p[kk * c_len:(kk + 1) * c_len]
            rx_ref[kk * pitch + c_len:(kk + 1) * pitch, dst] = jnp.zeros(
                (pitch - c_len, MXU_COLS), F32)
    col += D_RNN
    for c in range(D_RNN // MXU_COLS):
        dst = slice(c * MXU_COLS, (c + 1) * MXU_COLS)
        gr_ref[:, dst] = _silu(chunk(win_ref, col + c * MXU_COLS)).astype(BF16)
    for c in range(2 * D_MODEL // MXU_COLS):
        dst = slice(c * MXU_COLS, (c + 1) * MXU_COLS)
        mg_ref[:, dst] = _sigmoid(chunk(wm_ref, c * MXU_COLS) + bm_ref[:, dst]).astype(BF16)


def _seq_tile(seq, want):
    return want if seq % want == 0 else seq


def _project(h, w_in, w_merge, b_merge, cos, sin):
    bsz, seq, _ = h.shape
    ts = _seq_tile(seq, 512)
    d_in = w_in.shape[1]
    c_len, pitch = _chunking(seq)
    assert ts % c_len == 0
    rx_rows = ts // c_len * pitch

    def tile(width, rows=ts):
        return pl.BlockSpec((None, rows, width), lambda b, t: (b, t, 0))

    outs = [(ATTN_DIM, BF16, seq, ts), (KV_DIM, BF16, seq, ts), (KV_DIM, BF16, seq, ts),
            (ATTN_DIM, BF16, seq, ts), (D_RNN, F32, N_CHUNKS * pitch, rx_rows),
            (D_RNN, BF16, seq, ts), (2 * D_MODEL, BF16, seq, ts)]
    return pl.pallas_call(
        functools.partial(_proj_kernel, c_len=c_len),
        grid=(bsz, seq // ts),
        in_specs=[
            tile(D_MODEL),
            _resident((D_MODEL, d_in)),
            _resident((D_MODEL, 2 * D_MODEL)),
            _resident((1, 2 * D_MODEL)),
            pl.BlockSpec((ts, HEAD_DIM), lambda b, t: (t, 0)),
            pl.BlockSpec((ts, HEAD_DIM), lambda b, t: (t, 0)),
        ],
        out_specs=[tile(w, rows) for w, _, _, rows in outs],
        out_shape=[jax.ShapeDtypeStruct((bsz, total, w), dt) for w, dt, total, _ in outs],
        compiler_params=_params(2),
        name="in_proj",
    )(h, w_in, w_merge, b_merge, cos, sin)


N_SETS = 8
N_CHUNKS = N_SETS * SUBLANES
STEP_ROWS = N_SETS * SUBLANES
PITCH_PAD = SUBLANES
EXT_SLOTS = CONV_WIDTH - 1
PHASE_ROWS = 256
SCAN_STEPS = 4
GATHER_STEPS = 8
TINY = 1e-30


def _chunking(seq):
    c_len = seq // N_CHUNKS
    assert seq % N_CHUNKS == 0 and (c_len * STEP_ROWS) % PHASE_ROWS == 0
    assert c_len % SCAN_STEPS == 0 and c_len % GATHER_STEPS == 0
    assert c_len % (2 * SUBLANES) == 0
    return c_len, c_len + PITCH_PAD


def _sublane_scan(a, b, reverse):
    row = lax.broadcasted_iota(jnp.int32, a.shape, 0)
    for d in (1, 2, 4):
        if reverse:
            keep, shift = row < SUBLANES - d, SUBLANES - d
        else:
            keep, shift = row >= d, d
        a_s = pltpu.roll(a, shift, axis=0)
        b_s = pltpu.roll(b, shift, axis=0)
        b = jnp.where(keep, a * b_s + b, b)
        a = jnp.where(keep, a * a_s, a)
    return a, b


def _rglru_kernel(rx_ref, gr_ref, cw_ref, cb_ref, wg_ref, lam_ref, out_ref,
                  xs, ext, a_scr, b_scr, hs):
    seq = out_ref.shape[0]
    c_len, pitch = _chunking(seq)
    row8 = lax.broadcasted_iota(jnp.int32, (SUBLANES, LANES), 0)
    zero8 = jnp.zeros((SUBLANES, LANES), F32)

    def sets(v):
        return [v[s * SUBLANES:(s + 1) * SUBLANES] for s in range(N_SETS)]

    def slot(i):
        return slice((i + CONV_LEFT) * STEP_ROWS, (i + CONV_LEFT + 1) * STEP_ROWS)

    def gather(n, carry):
        i0 = n * GATHER_STEPS
        vals = []
        for kk in range(GATHER_STEPS):
            for s in range(N_SETS):
                vals.append(rx_ref[pl.ds(s * SUBLANES * pitch + i0 + kk, SUBLANES, stride=pitch), :])
        dst = pl.multiple_of((i0 + CONV_LEFT) * STEP_ROWS, STEP_ROWS)
        ext[pl.ds(dst, GATHER_STEPS * STEP_ROWS), :] = jnp.concatenate(vals, axis=0)
        return carry

    lax.fori_loop(0, c_len // GATHER_STEPS, gather, 0)

    for back in (1, 2):
        tail = [pltpu.roll(v, 1, axis=0) for v in sets(ext[slot(c_len - back), :])]
        ext[slot(-back), :] = jnp.concatenate(
            [jnp.where(row8 == 0, tail[s - 1] if s > 0 else zero8, tail[s])
             for s in range(N_SETS)], axis=0)
    head = [pltpu.roll(v, SUBLANES - 1, axis=0) for v in sets(ext[slot(0), :])]
    ext[slot(c_len), :] = jnp.concatenate(
        [jnp.where(row8 == SUBLANES - 1, head[s + 1] if s < N_SETS - 1 else zero8, head[s])
         for s in range(N_SETS)], axis=0)

    neg = -lam_ref[...]
    softplus = jnp.maximum(neg, 0.0) + jnp.log(1.0 + jnp.exp(-jnp.abs(neg)))
    kexp = softplus * (-RGLRU_C * 0.5 * math.log2(math.e))
    cw = cw_ref[...]
    cb = cb_ref[...]
    ones16 = jnp.ones((PHASE_ROWS, RNN_BLOCK), BF16)

    def phase(n, carry):
        src = n * PHASE_ROWS
        dst = pl.ds(pl.multiple_of(src, PHASE_ROWS), PHASE_ROWS)
        y = cb
        for tap in range(CONV_WIDTH):
            rows = pl.ds(pl.multiple_of(src + tap * STEP_ROWS, STEP_ROWS), PHASE_ROWS)
            y = y + ext[rows, :] * cw[tap:tap + 1, :]
        lhs = jnp.concatenate([y.astype(BF16), ones16], axis=1)
        for d in range(2):
            g = jnp.dot(lhs, wg_ref[d], preferred_element_type=F32)
            ta = jnp.tanh(g[:, :RNN_BLOCK])
            ti = jnp.tanh(g[:, RNN_BLOCK:])
            k = kexp[d:d + 1, :]
            a = jnp.exp2(k * ta + k)
            z = 1.0 - a * a
            m = z * lax.rsqrt(jnp.maximum(z, TINY))
            a_scr[d, dst, :] = a
            b_scr[d, dst, :] = m * ((ti + 1.0) * y)
        return carry

    lax.fori_loop(0, seq // PHASE_ROWS, phase, 0, unroll=4)

    n_blocks = c_len // SCAN_STEPS
    block_rows = SCAN_STEPS * STEP_ROWS

    for d in range(2):
        reverse = d == 1
        order = range(SCAN_STEPS - 1, -1, -1) if reverse else range(SCAN_STEPS)

        def block(n):
            blk = (n_blocks - 1 - n) if reverse else n
            return blk, pl.ds(pl.multiple_of(blk * block_rows, block_rows), block_rows)

        def piece(v, kk, s):
            r0 = (kk * N_SETS + s) * SUBLANES
            return v[r0:r0 + SUBLANES]

        def summarise(n, carry):
            _, rows = block(n)
            av = a_scr[d, rows, :]
            bv = b_scr[d, rows, :]
            carry = list(carry)
            for kk in order:
                for s in range(N_SETS):
                    a = piece(av, kk, s)
                    h, p = carry[s]
                    carry[s] = (a * h + piece(bv, kk, s), a * p)
            return tuple(carry)

        ones8 = jnp.ones((SUBLANES, LANES), F32)
        ends = lax.fori_loop(0, n_blocks, summarise,
                             tuple((zero8, ones8) for _ in range(N_SETS)))

        edge = zero8
        start = [None] * N_SETS
        for s in (range(N_SETS - 1, -1, -1) if reverse else range(N_SETS)):
            p_cum, h_cum = _sublane_scan(ends[s][1], ends[s][0], reverse)
            after = p_cum * edge + h_cum
            if reverse:
                start[s] = jnp.where(row8 == SUBLANES - 1, edge,
                                     pltpu.roll(after, SUBLANES - 1, axis=0))
                edge = jnp.broadcast_to(after[0:1, :], (SUBLANES, LANES))
            else:
                start[s] = jnp.where(row8 == 0, edge, pltpu.roll(after, 1, axis=0))
                edge = jnp.broadcast_to(after[SUBLANES - 1:SUBLANES, :], (SUBLANES, LANES))

        def rescan(n, carry):
            blk, rows = block(n)
            av = a_scr[d, rows, :]
            bv = b_scr[d, rows, :]
            hv = hs[rows, :] if reverse else None
            carry = list(carry)
            out = [None] * (SCAN_STEPS * N_SETS)
            for kk in order:
                for s in range(N_SETS):
                    h = piece(av, kk, s) * carry[s] + piece(bv, kk, s)
                    carry[s] = h
                    if reverse:
                        i = blk * SCAN_STEPS + kk
                        xs[pl.ds(s * SUBLANES * pitch + i, SUBLANES, stride=pitch), :] = (
                            h + piece(hv, kk, s))
                    else:
                        out[kk * N_SETS + s] = h
            if not reverse:
                hs[rows, :] = jnp.concatenate(out, axis=0)
            return tuple(carry)

        lax.fori_loop(0, n_blocks, rescan, tuple(start))

    for ch in range(N_CHUNKS):
        rows = slice(ch * c_len, (ch + 1) * c_len)
        h = xs[ch * pitch:ch * pitch + c_len, :]
        out_ref[rows, :] = (h * gr_ref[rows, :].astype(F32)).astype(BF16)


def _rglru(rnn_x, gate, conv_w, conv_b, w_gate, lam):
    bsz, seq, _ = gate.shape
    c_len, pitch = _chunking(seq)
    assert rnn_x.shape[1] == N_CHUNKS * pitch

    def col_block(rows):
        return pl.BlockSpec((None, rows, RNN_BLOCK), lambda b, c: (b, 0, c))

    return pl.pallas_call(
        _rglru_kernel,
        grid=(bsz, N_RNN_BLOCKS),
        in_specs=[
            col_block(N_CHUNKS * pitch),
            col_block(seq),
            pl.BlockSpec((CONV_WIDTH, RNN_BLOCK), lambda b, c: (0, c)),
            pl.BlockSpec((1, RNN_BLOCK), lambda b, c: (0, c)),
            pl.BlockSpec((2, None, 2 * RNN_BLOCK, 2 * RNN_BLOCK), lambda b, c: (0, c, 0, 0)),
            pl.BlockSpec((2, RNN_BLOCK), lambda b, c: (0, c)),
        ],
        out_specs=col_block(seq),
        out_shape=jax.ShapeDtypeStruct((bsz, seq, D_RNN), BF16),
        scratch_shapes=[
            pltpu.VMEM((N_CHUNKS * pitch, RNN_BLOCK), F32),
            pltpu.VMEM(((c_len + EXT_SLOTS) * STEP_ROWS, RNN_BLOCK), F32),
            pltpu.VMEM((2, seq, RNN_BLOCK), F32),
            pltpu.VMEM((2, seq, RNN_BLOCK), F32),
            pltpu.VMEM((seq, RNN_BLOCK), F32),
        ],
        compiler_params=_params(2),
        name="rglru",
    )(rnn_x, gate, conv_w, conv_b, w_gate, lam)


def _attn_out_kernel(sink_ref, q_ref, k_ref, kp_ref, kn_ref, v_ref, vp_ref, vn_ref, ga_ref,
                     rnn_ref, mg_ref, x_ref, mod_ref, wa_ref, wr_ref, wo_ref, ng_ref, nmod_ref,
                     *rest, final):
    if final:
        o_ref, kext, vext, attn_scr = rest
    else:
        o_ref, h_ref, kext, vext, attn_scr = rest
    tq = q_ref.shape[0]
    t = pl.program_id(1)
    nt = pl.num_programs(1)
    n_blk = tq // BLOCK

    kext[0:BLOCK, :] = kp_ref[...]
    kext[BLOCK:BLOCK + tq, :] = k_ref[...]
    kext[BLOCK + tq:, :] = kn_ref[...]
    vext[0:BLOCK, :] = vp_ref[...]
    vext[BLOCK:BLOCK + tq, :] = v_ref[...]
    vext[BLOCK + tq:, :] = vn_ref[...]

    band = 3 * BLOCK
    qi = lax.broadcasted_iota(jnp.int32, (BLOCK, band), 0)
    kj = lax.broadcasted_iota(jnp.int32, (BLOCK, band), 1)
    in_window = (kj >= qi) & (kj <= qi + 2 * BLOCK)
    head = lax.broadcasted_iota(jnp.int32, (GROUP, BLOCK, 1), 0)

    for blk in range(n_blk):
        rows = slice(blk * BLOCK, (blk + 1) * BLOCK)
        band_rows = slice(blk * BLOCK, blk * BLOCK + band)
        first = (t == 0) if blk == 0 else False
        last = (t == nt - 1) if blk == n_blk - 1 else False
        lo = jnp.where(first, BLOCK, 0)
        hi = jnp.where(last, 2 * BLOCK, band)
        valid = in_window & (kj >= lo) & (kj < hi)
        for g in range(N_KV_HEADS):
            kv_cols = slice(g * HEAD_DIM, (g + 1) * HEAD_DIM)
            qg = jnp.concatenate(
                [q_ref[rows, (g * GROUP + h) * HEAD_DIM:(g * GROUP + h + 1) * HEAD_DIM]
                 for h in range(GROUP)], axis=0)
            s = lax.dot_general(qg, kext[band_rows, kv_cols], (((1,), (1,)), ((), ())),
                                preferred_element_type=F32)
            s = s.reshape(GROUP, BLOCK, band)
            s = jnp.where(valid[None], s, MASK_VALUE)
            sink = jnp.full((GROUP, BLOCK, 1), sink_ref[g * GROUP], F32)
            for h in range(1, GROUP):
                sink = jnp.where(head == h, sink_ref[g * GROUP + h], sink)
            m = jnp.maximum(jnp.max(s, axis=-1, keepdims=True), sink)
            p = jnp.exp(s - m)
            denom = jnp.sum(p, axis=-1, keepdims=True) + jnp.exp(sink - m)
            o = jnp.dot(p.reshape(GROUP * BLOCK, band).astype(BF16), vext[band_rows, kv_cols],
                        preferred_element_type=F32)
            o = o.reshape(GROUP, BLOCK, HEAD_DIM) / denom
            for h in range(GROUP):
                cols = slice((g * GROUP + h) * HEAD_DIM, (g * GROUP + h + 1) * HEAD_DIM)
                attn_scr[rows, cols] = (o[h] * ga_ref[rows, cols].astype(F32)).astype(BF16)

    a = jnp.dot(attn_scr[...], wa_ref[...], preferred_element_type=F32)
    r = jnp.dot(rnn_ref[...], wr_ref[...], preferred_element_type=F32)
    mixed = (mg_ref[:, 0:D_MODEL].astype(F32) * a + mg_ref[:, D_MODEL:].astype(F32) * r)
    y = jnp.dot(mixed.astype(BF16), wo_ref[...], preferred_element_type=F32)
    out = x_ref[...] + mod_ref[:, 2 * D_MODEL:] * y
    if final:
        ms = jnp.mean(out * out, axis=-1, keepdims=True)
        o_ref[...] = out * lax.rsqrt(ms + EPS) * ng_ref[...]
    else:
        o_ref[...] = out
        h_ref[...] = _adaln(out, ng_ref[...], nmod_ref).astype(BF16)


def _attend_and_mix(sink, q, k, v, ga, rnn, mg, x, mod, w_attn, w_rnn, w_out, next_gain,
                    next_mod, batch_off, final):
    bsz, seq, _ = x.shape
    tq = _seq_tile(seq, 512)
    per = tq // BLOCK
    n_blocks = seq // BLOCK

    def tile(width):
        return pl.BlockSpec((None, tq, width), lambda b, t: (b, t, 0))

    prev_kv = pl.BlockSpec((None, BLOCK, KV_DIM), lambda b, t: (b, jnp.maximum(t * per - 1, 0), 0))
    next_kv = pl.BlockSpec((None, BLOCK, KV_DIM),
                           lambda b, t: (b, jnp.minimum((t + 1) * per, n_blocks - 1), 0))
    mod_spec = pl.BlockSpec((None, 1, 3 * D_MODEL), lambda b, t: (b + batch_off, 0, 0))
    out_specs = [tile(D_MODEL)]
    out_shape = [jax.ShapeDtypeStruct((bsz, seq, D_MODEL), F32)]
    if not final:
        out_specs.append(tile(D_MODEL))
        out_shape.append(jax.ShapeDtypeStruct((bsz, seq, D_MODEL), BF16))
    return pl.pallas_call(
        functools.partial(_attn_out_kernel, final=final),
        grid=(bsz, seq // tq),
        in_specs=[
            pl.BlockSpec(memory_space=pltpu.SMEM),
            tile(ATTN_DIM),
            tile(KV_DIM), prev_kv, next_kv,
            tile(KV_DIM), prev_kv, next_kv,
            tile(ATTN_DIM),
            tile(D_RNN),
            tile(2 * D_MODEL),
            tile(D_MODEL),
            mod_spec,
            _resident((ATTN_DIM, D_MODEL)),
            _resident((D_RNN, D_MODEL)),
            _resident((D_MODEL, D_MODEL)),
            _resident((1, D_MODEL)),
            mod_spec,
        ],
        out_specs=out_specs,
        out_shape=out_shape,
        scratch_shapes=[
            pltpu.VMEM((tq + 2 * BLOCK, KV_DIM), BF16),
            pltpu.VMEM((tq + 2 * BLOCK, KV_DIM), BF16),
            pltpu.VMEM((tq, ATTN_DIM), BF16),
        ],
        compiler_params=_params(2),
        name="attn_out",
    )(sink, q, k, k, k, v, v, v, ga, rnn, mg, x, mod, w_attn, w_rnn, w_out, next_gain, next_mod)


def _trunk(x, batch_off, mods, tables, layers, final_gain):
    cos, sin = tables
    seq = x.shape[1]
    cos, sin = cos[:seq], sin[:seq]
    depth = len(layers)
    h = _prenorm(x, mods[0], layers[0]["norm_gain"], batch_off)
    for l, p in enumerate(layers):
        final = l == depth - 1
        q, k, v, ga, rx, gr, mg = _project(h, p["w_in"], p["w_merge"], p["b_merge"], cos, sin)
        rnn = _rglru(rx, gr, p["conv_w"], p["conv_b"], p["w_gate"], p["lam"])
        next_gain = final_gain if final else layers[l + 1]["norm_gain"]
        next_mod = mods[l] if final else mods[l + 1]
        res = _attend_and_mix(p["sink"], q, k, v, ga, rnn, mg, x, mods[l], p["w_attn"],
                              p["w_rnn"], p["w_out"], next_gain, next_mod, batch_off, final)
        x = res[0]
        h = None if final else res[1]
    return x


def kernel(x_prompt, x_sample, c_prompt, c_sample, norm_gain, w_ada, b_ada, w_in, attn_sink, conv_w, conv_b, rg_w_a, rg_b_a, rg_w_x, rg_b_x, rg_lambda, w_attn_proj, w_rnn_proj, w_merge, b_merge, w_out, final_gain):
    depth = w_in.shape[0]
    n_prompt = x_prompt.shape[0]
    cond = jnp.concatenate([c_prompt, c_sample], axis=0)
    mods = _modulation(cond, w_ada, b_ada)
    mods = mods.reshape(depth, cond.shape[0], 1, 3 * D_MODEL)
    tables = _rope_tables(max(x_prompt.shape[1], x_sample.shape[1]))

    w_gate = jnp.concatenate([rg_w_a, rg_w_x], axis=-1).astype(BF16)
    b_half = 0.5 * jnp.concatenate(
        [rg_b_a.reshape(depth, 2, N_RNN_BLOCKS, 1, RNN_BLOCK),
         rg_b_x.reshape(depth, 2, N_RNN_BLOCKS, 1, RNN_BLOCK)], axis=-1)
    b_hi = b_half.astype(BF16)
    b_lo = (b_half - b_hi.astype(F32)).astype(BF16)
    w_gate = jnp.concatenate(
        [w_gate, b_hi, b_lo,
         jnp.zeros((depth, 2, N_RNN_BLOCKS, RNN_BLOCK - 2, 2 * RNN_BLOCK), BF16)], axis=-2)
    layers = []
    for l in range(depth):
        layers.append(dict(
            norm_gain=norm_gain[l].reshape(1, D_MODEL),
            w_in=w_in[l].astype(BF16),
            w_merge=w_merge[l].astype(BF16),
            b_merge=b_merge[l].reshape(1, 2 * D_MODEL),
            sink=attn_sink[l],
            conv_w=0.5 * conv_w[l],
            conv_b=0.5 * conv_b[l].reshape(1, D_RNN),
            w_gate=w_gate[l], lam=rg_lambda[l],
            w_attn=w_attn_proj[l].astype(BF16),
            w_rnn=w_rnn_proj[l].astype(BF16),
            w_out=w_out[l].astype(BF16),
        ))
    fg = final_gain.reshape(1, D_MODEL)
    y_prompt = _trunk(x_prompt, 0, mods, tables, layers, fg)
    y_sample = _trunk(x_sample, n_prompt, mods, tables, layers, fg)
    return (y_prompt, y_sample)
```

```python
import functools
import math

import jax
import jax.numpy as jnp
from jax import lax
from jax.experimental import pallas as pl
from jax.experimental.pallas import tpu as pltpu

D_MODEL = 1024
HEAD_DIM = 128
N_HEADS = 8
N_KV_HEADS = 2
GROUP = N_HEADS // N_KV_HEADS
ATTN_DIM = N_HEADS * HEAD_DIM
KV_DIM = N_KV_HEADS * HEAD_DIM
BLOCK = 128
ROPE_THETA = 10000.0
D_RNN = 3 * D_MODEL // 2
N_RNN_BLOCKS = 12
RNN_BLOCK = D_RNN // N_RNN_BLOCKS
CONV_WIDTH = 4
CONV_LEFT = 2
RGLRU_C = 8.0
EPS = 1e-6
MASK_VALUE = -1e30

SUBLANES = 8
LANES = 128
MXU_COLS = 256
VMEM_LIMIT_BYTES = 56 * 1024 * 1024

F32 = jnp.float32
BF16 = jnp.bfloat16


def _sigmoid(x):
    return 0.5 * jnp.tanh(0.5 * x) + 0.5


def _silu(x):
    return x * _sigmoid(x)


def _resident(shape):
    zeros = (0,) * len(shape)
    return pl.BlockSpec(shape, lambda *_: zeros, pipeline_mode=pl.Buffered(1))


def _params(n_axes):
    return pltpu.CompilerParams(dimension_semantics=("arbitrary",) * n_axes,
                                vmem_limit_bytes=VMEM_LIMIT_BYTES)


def _mod_kernel(c_ref, w_ref, b_ref, o_ref):
    c = c_ref[...]
    s = c * jax.nn.sigmoid(c)
    o_ref[...] = jnp.dot(s, w_ref[...], precision=lax.Precision.HIGHEST,
                         preferred_element_type=F32) + b_ref[...]


def _modulation(cond, w_ada, b_ada):
    depth = w_ada.shape[0]
    nb = cond.shape[0]
    n_col = 3 * D_MODEL // D_MODEL
    return pl.pallas_call(
        _mod_kernel,
        grid=(depth, n_col),
        in_specs=[
            pl.BlockSpec((nb, D_MODEL), lambda l, j: (0, 0)),
            pl.BlockSpec((None, D_MODEL, D_MODEL), lambda l, j: (l, 0, j)),
            pl.BlockSpec((None, 1, D_MODEL), lambda l, j: (l, 0, j)),
        ],
        out_specs=pl.BlockSpec((None, nb, D_MODEL), lambda l, j: (l, 0, j)),
        out_shape=jax.ShapeDtypeStruct((depth, nb, 3 * D_MODEL), F32),
        compiler_params=_params(2),
        name="adaln_mod",
    )(cond, w_ada, b_ada.reshape(depth, 1, 3 * D_MODEL))


def _rope_kernel(cos_ref, sin_ref):
    rows = cos_ref.shape[0]
    t = lax.broadcasted_iota(jnp.int32, (rows, HEAD_DIM), 0) + pl.program_id(0) * rows
    j = lax.broadcasted_iota(jnp.int32, (rows, HEAD_DIM), 1)
    half = HEAD_DIM // 2
    jj = jnp.where(j < half, j, j - half).astype(F32)
    inv_freq = jnp.exp(jj * (-2.0 / HEAD_DIM * math.log(ROPE_THETA)))
    ang = t.astype(F32) * inv_freq
    s = jnp.sin(ang)
    cos_ref[...] = jnp.cos(ang)
    sin_ref[...] = jnp.where(j < half, -s, s)


def _rope_tables(seq):
    rows = min(seq, 512)
    return pl.pallas_call(
        _rope_kernel,
        grid=(seq // rows,),
        out_specs=[pl.BlockSpec((rows, HEAD_DIM), lambda i: (i, 0))] * 2,
        out_shape=[jax.ShapeDtypeStruct((seq, HEAD_DIM), F32)] * 2,
        compiler_params=_params(1),
        name="rope_tables",
    )()


NORM_ROWS = 32


def _adaln(x, gain, mod_ref):
    ms = jnp.mean(x * x, axis=-1, keepdims=True)
    y = x * lax.rsqrt(ms + EPS) * gain
    return y * (1.0 + mod_ref[:, D_MODEL:2 * D_MODEL]) + mod_ref[:, 0:D_MODEL]


def _prenorm_kernel(x_ref, mod_ref, ng_ref, h_ref):
    gain = ng_ref[...]

    def norm_rows(r, carry):
        rows = pl.ds(pl.multiple_of(r * NORM_ROWS, NORM_ROWS), NORM_ROWS)
        h_ref[rows, :] = _adaln(x_ref[rows, :], gain, mod_ref).astype(BF16)
        return carry

    lax.fori_loop(0, x_ref.shape[0] // NORM_ROWS, norm_rows, 0)


def _prenorm(x, mod, norm_gain, batch_off):
    bsz, seq, _ = x.shape
    ts = _seq_tile(seq, 512)
    tile = pl.BlockSpec((None, ts, D_MODEL), lambda b, t: (b, t, 0))
    return pl.pallas_call(
        _prenorm_kernel,
        grid=(bsz, seq // ts),
        in_specs=[tile,
                  pl.BlockSpec((None, 1, 3 * D_MODEL), lambda b, t: (b + batch_off, 0, 0)),
                  _resident((1, D_MODEL))],
        out_specs=tile,
        out_shape=jax.ShapeDtypeStruct((bsz, seq, D_MODEL), BF16),
        compiler_params=_params(2),
        name="prenorm",
    )(x, mod, norm_gain)


def _proj_kernel(h_ref, win_ref, wm_ref, bm_ref, cos_ref, sin_ref,
                 q_ref, k_ref, v_ref, ga_ref, rx_ref, gr_ref, mg_ref, *, c_len):
    ts = h_ref.shape[0]
    cos = cos_ref[...]
    sin = sin_ref[...]

    def rotary(p):
        return p * cos + pltpu.roll(p, HEAD_DIM // 2, axis=1) * sin

    def chunk(w_ref, c0):
        return jnp.dot(h_ref[...], w_ref[:, c0:c0 + MXU_COLS], preferred_element_type=F32)

    qk_scale = HEAD_DIM ** -0.5
    col = 0
    for c in range(ATTN_DIM // MXU_COLS):
        p = chunk(win_ref, col + c * MXU_COLS)
        for hh in range(MXU_COLS // HEAD_DIM):
            dst = slice(c * MXU_COLS + hh * HEAD_DIM, c * MXU_COLS + (hh + 1) * HEAD_DIM)
            q_ref[:, dst] = (rotary(p[:, hh * HEAD_DIM:(hh + 1) * HEAD_DIM]) * qk_scale).astype(BF16)
    col += ATTN_DIM
    p = chunk(win_ref, col)
    for hh in range(N_KV_HEADS):
        dst = slice(hh * HEAD_DIM, (hh + 1) * HEAD_DIM)
        k_ref[:, dst] = rotary(p[:, dst]).astype(BF16)
    col += KV_DIM
    v_ref[...] = chunk(win_ref, col).astype(BF16)
    col += KV_DIM
    for c in range(ATTN_DIM // MXU_COLS):
        dst = slice(c * MXU_COLS, (c + 1) * MXU_COLS)
        ga_ref[:, dst] = _silu(chunk(win_ref, col + c * MXU_COLS)).astype(BF16)
    col += ATTN_DIM
    pitch = rx_ref.shape[0] * c_len // ts
    for c in range(D_RNN // MXU_COLS):
        dst = slice(c * MXU_COLS, (c + 1) * MXU_COLS)
        p = chunk(win_ref, col + c * MXU_COLS)
        for kk in range(ts // c_len):
            rx_ref[kk * pitch:kk * pitch + c_len, dst] = p[kk * c_len:(kk + 1) * c_len]
            rx_ref[kk * pitch + c_len:(kk + 1) * pitch, dst] = jnp.zeros(
                (pitch - c_len, MXU_COLS), F32)
    col += D_RNN
    for c in range(D_RNN // MXU_COLS):
        dst = slice(c * MXU_COLS, (c + 1) * MXU_COLS)
        gr_ref[:, dst] = _silu(chunk(win_ref, col + c * MXU_COLS)).astype(BF16)
    for c in range(2 * D_MODEL // MXU_COLS):
        dst = slice(c * MXU_COLS, (c + 1) * MXU_COLS)
        mg_ref[:, dst] = _sigmoid(chunk(wm_ref, c * MXU_COLS) + bm_ref[:, dst]).astype(BF16)


def _seq_tile(seq, want):
    return want if seq % want == 0 else seq


def _project(h, w_in, w_merge, b_merge, cos, sin):
    bsz, seq, _ = h.shape
    ts = _seq_tile(seq, 512)
    d_in = w_in.shape[1]
    c_len, pitch = _chunking(seq)
    assert ts % c_len == 0
    rx_rows = ts // c_len * pitch

    def tile(width, rows=ts):
        return pl.BlockSpec((None, rows, width), lambda b, t: (b, t, 0))

    outs = [(ATTN_DIM, BF16, seq, ts), (KV_DIM, BF16, seq, ts), (KV_DIM, BF16, seq, ts),
            (ATTN_DIM, BF16, seq, ts), (D_RNN, F32, N_CHUNKS * pitch, rx_rows),
            (D_RNN, BF16, seq, ts), (2 * D_MODEL, BF16, seq, ts)]
    return pl.pallas_call(
        functools.partial(_proj_kernel, c_len=c_len),
        grid=(bsz, seq // ts),
        in_specs=[
            tile(D_MODEL),
            _resident((D_MODEL, d_in)),
            _resident((D_MODEL, 2 * D_MODEL)),
            _resident((1, 2 * D_MODEL)),
            pl.BlockSpec((ts, HEAD_DIM), lambda b, t: (t, 0)),
            pl.BlockSpec((ts, HEAD_DIM), lambda b, t: (t, 0)),
        ],
        out_specs=[tile(w, rows) for w, _, _, rows in outs],
        out_shape=[jax.ShapeDtypeStruct((bsz, total, w), dt) for w, dt, total, _ in outs],
        compiler_params=_params(2),
        name="in_proj",
    )(h, w_in, w_merge, b_merge, cos, sin)


N_SETS = 8
N_CHUNKS = N_SETS * SUBLANES
STEP_ROWS = N_SETS * SUBLANES
PITCH_PAD = SUBLANES
EXT_SLOTS = CONV_WIDTH - 1
PHASE_ROWS = 256
SCAN_STEPS = 4
GATHER_STEPS = PHASE_ROWS // STEP_ROWS
TINY = 1e-30


def _chunking(seq):
    c_len = seq // N_CHUNKS
    assert seq % N_CHUNKS == 0 and (c_len * STEP_ROWS) % PHASE_ROWS == 0
    assert c_len % SCAN_STEPS == 0 and c_len % GATHER_STEPS == 0
    assert c_len % (2 * SUBLANES) == 0
    return c_len, c_len + PITCH_PAD


def _sublane_scan(a, b, reverse):
    row = lax.broadcasted_iota(jnp.int32, a.shape, 0)
    for d in (1, 2, 4):
        if reverse:
            keep, shift = row < SUBLANES - d, SUBLANES - d
        else:
            keep, shift = row >= d, d
        a_s = pltpu.roll(a, shift, axis=0)
        b_s = pltpu.roll(b, shift, axis=0)
        b = jnp.where(keep, a * b_s + b, b)
        a = jnp.where(keep, a * a_s, a)
    return a, b


def _rglru_kernel(rx_ref, gr_ref, cw_ref, cb_ref, wg_ref, lam_ref, out_ref,
                  xs, ext, a_scr, b_scr, hs):
    seq = out_ref.shape[0]
    c_len, pitch = _chunking(seq)
    row8 = lax.broadcasted_iota(jnp.int32, (SUBLANES, LANES), 0)
    zero8 = jnp.zeros((SUBLANES, LANES), F32)

    def sets(v):
        return [v[s * SUBLANES:(s + 1) * SUBLANES] for s in range(N_SETS)]

    def slot(i):
        return slice((i + CONV_LEFT) * STEP_ROWS, (i + CONV_LEFT + 1) * STEP_ROWS)

    n_gather = c_len // GATHER_STEPS

    def gather(g):
        i0 = g * GATHER_STEPS
        vals = []
        for kk in range(GATHER_STEPS):
            for s in range(N_SETS):
                vals.append(rx_ref[pl.ds(s * SUBLANES * pitch + i0 + kk, SUBLANES, stride=pitch), :])
        dst = pl.multiple_of((i0 + CONV_LEFT) * STEP_ROWS, STEP_ROWS)
        ext[pl.ds(dst, GATHER_STEPS * STEP_ROWS), :] = jnp.concatenate(vals, axis=0)

    for g in sorted({0, min(1, n_gather - 1), n_gather - 1}):
        gather(g)

    for back in (1, 2):
        tail = [pltpu.roll(v, 1, axis=0) for v in sets(ext[slot(c_len - back), :])]
        ext[slot(-back), :] = jnp.concatenate(
            [jnp.where(row8 == 0, tail[s - 1] if s > 0 else zero8, tail[s])
             for s in range(N_SETS)], axis=0)
    head = [pltpu.roll(v, SUBLANES - 1, axis=0) for v in sets(ext[slot(0), :])]
    ext[slot(c_len), :] = jnp.concatenate(
        [jnp.where(row8 == SUBLANES - 1, head[s + 1] if s < N_SETS - 1 else zero8, head[s])
         for s in range(N_SETS)], axis=0)

    neg = -lam_ref[...]
    softplus = jnp.maximum(neg, 0.0) + jnp.log(1.0 + jnp.exp(-jnp.abs(neg)))
    kexp = softplus * (-RGLRU_C * 0.5 * math.log2(math.e))
    cw = cw_ref[...]
    cb = cb_ref[...]
    ones16 = jnp.ones((PHASE_ROWS, RNN_BLOCK), BF16)

    def phase(n, carry):
        gather(jnp.minimum(n + 2, n_gather - 1))
        src = n * PHASE_ROWS
        dst = pl.ds(pl.multiple_of(src, PHASE_ROWS), PHASE_ROWS)
        y = cb
        for tap in range(CONV_WIDTH):
            rows = pl.ds(pl.multiple_of(src + tap * STEP_ROWS, STEP_ROWS), PHASE_ROWS)
            y = y + ext[rows, :] * cw[tap:tap + 1, :]
        lhs = jnp.concatenate([y.astype(BF16), ones16], axis=1)
        for d in range(2):
            g = jnp.dot(lhs, wg_ref[d], preferred_element_type=F32)
            ta = jnp.tanh(g[:, :RNN_BLOCK])
            ti = jnp.tanh(g[:, RNN_BLOCK:])
            k = kexp[d:d + 1, :]
            a = jnp.exp2(k * ta + k)
            z = 1.0 - a * a
            m = z * lax.rsqrt(jnp.maximum(z, TINY))
            a_scr[d, dst, :] = a
            b_scr[d, dst, :] = m * ((ti + 1.0) * y)
        return carry

    lax.fori_loop(0, seq // PHASE_ROWS, phase, 0, unroll=8)

    n_blocks = c_len // SCAN_STEPS
    block_rows = SCAN_STEPS * STEP_ROWS

    def order(d):
        return range(SCAN_STEPS - 1, -1, -1) if d else range(SCAN_STEPS)

    def block(d, n):
        blk = (n_blocks - 1 - n) if d else n
        return blk, pl.ds(pl.multiple_of(blk * block_rows, block_rows), block_rows)

    def piece(v, kk, s):
        r0 = (kk * N_SETS + s) * SUBLANES
        return v[r0:r0 + SUBLANES]

    def summarise(d, n, carry):
        _, rows = block(d, n)
        av = a_scr[d, rows, :]
        bv = b_scr[d, rows, :]
        carry = list(carry)
        for kk in order(d):
            for s in range(N_SETS):
                a = piece(av, kk, s)
                h, p = carry[s]
                carry[s] = (a * h + piece(bv, kk, s), a * p)
        return tuple(carry)

    def resolve(d, ends):
        edge = zero8
        start = [None] * N_SETS
        for s in (range(N_SETS - 1, -1, -1) if d else range(N_SETS)):
            p_cum, h_cum = _sublane_scan(ends[s][1], ends[s][0], bool(d))
            after = p_cum * edge + h_cum
            if d:
                start[s] = jnp.where(row8 == SUBLANES - 1, edge,
                                     pltpu.roll(after, SUBLANES - 1, axis=0))
                edge = jnp.broadcast_to(after[0:1, :], (SUBLANES, LANES))
            else:
                start[s] = jnp.where(row8 == 0, edge, pltpu.roll(after, 1, axis=0))
                edge = jnp.broadcast_to(after[SUBLANES - 1:SUBLANES, :], (SUBLANES, LANES))
        return tuple(start)

    def rescan(d, n, carry):
        blk, rows = block(d, n)
        av = a_scr[d, rows, :]
        bv = b_scr[d, rows, :]
        hv = hs[rows, :] if d else None
        carry = list(carry)
        out = [None] * (SCAN_STEPS * N_SETS)
        for kk in order(d):
            for s in range(N_SETS):
                h = piece(av, kk, s) * carry[s] + piece(bv, kk, s)
                carry[s] = h
                if d:
                    i = blk * SCAN_STEPS + kk
                    xs[pl.ds(s * SUBLANES * pitch + i, SUBLANES, stride=pitch), :] = (
                        h + piece(hv, kk, s))
                else:
                    out[kk * N_SETS + s] = h
        if not d:
            hs[rows, :] = jnp.concatenate(out, axis=0)
        return tuple(carry)

    ones8 = jnp.ones((SUBLANES, LANES), F32)
    fresh = tuple((zero8, ones8) for _ in range(N_SETS))
    ends_f = lax.fori_loop(0, n_blocks, functools.partial(summarise, 0), fresh)
    _, ends_b = lax.fori_loop(
        0, n_blocks, lambda n, c: (rescan(0, n, c[0]), summarise(1, n, c[1])),
        (resolve(0, ends_f), fresh))
    lax.fori_loop(0, n_blocks, functools.partial(rescan, 1), resolve(1, ends_b))

    for ch in range(N_CHUNKS):
        rows = slice(ch * c_len, (ch + 1) * c_len)
        h = xs[ch * pitch:ch * pitch + c_len, :]
        out_ref[rows, :] = (h * gr_ref[rows, :].astype(F32)).astype(BF16)


def _rglru(rnn_x, gate, conv_w, conv_b, w_gate, lam):
    bsz, seq, _ = gate.shape
    c_len, pitch = _chunking(seq)
    assert rnn_x.shape[1] == N_CHUNKS * pitch

    def col_block(rows):
        return pl.BlockSpec((None, rows, RNN_BLOCK), lambda b, c: (b, 0, c))

    return pl.pallas_call(
        _rglru_kernel,
        grid=(bsz, N_RNN_BLOCKS),
        in_specs=[
            col_block(N_CHUNKS * pitch),
            col_block(seq),
            pl.BlockSpec((CONV_WIDTH, RNN_BLOCK), lambda b, c: (0, c)),
            pl.BlockSpec((1, RNN_BLOCK), lambda b, c: (0, c)),
            pl.BlockSpec((2, None, 2 * RNN_BLOCK, 2 * RNN_BLOCK), lambda b, c: (0, c, 0, 0)),
            pl.BlockSpec((2, RNN_BLOCK), lambda b, c: (0, c)),
        ],
        out_specs=col_block(seq),
        out_shape=jax.ShapeDtypeStruct((bsz, seq, D_RNN), BF16),
        scratch_shapes=[
            pltpu.VMEM((N_CHUNKS * pitch, RNN_BLOCK), F32),
            pltpu.VMEM(((c_len + EXT_SLOTS) * STEP_ROWS, RNN_BLOCK), F32),
            pltpu.VMEM((2, seq, RNN_BLOCK), F32),
            pltpu.VMEM((2, seq, RNN_BLOCK), F32),
            pltpu.VMEM((seq, RNN_BLOCK), F32),
        ],
        compiler_params=_params(2),
        name="rglru",
    )(rnn_x, gate, conv_w, conv_b, w_gate, lam)


MIX_BLOCKS = 4

def _attn_out_kernel(sink_ref, q_ref, k_ref, kp_ref, kn_ref, v_ref, vp_ref, vn_ref, ga_ref,
                     rnn_ref, mg_ref, x_ref, mod_ref, wa_ref, wr_ref, wo_ref, ng_ref, nmod_ref,
                     *rest, final):
    if final:
        o_ref, kext, vext, kbd, vbd, attn_scr = rest
    else:
        o_ref, h_ref, kext, vext, kbd, vbd, attn_scr = rest
    tq = q_ref.shape[0]
    t = pl.program_id(1)
    nt = pl.num_programs(1)
    n_blk = tq // BLOCK

    kext[0:BLOCK, :] = kp_ref[...]
    kext[BLOCK:BLOCK + tq, :] = k_ref[...]
    kext[BLOCK + tq:, :] = kn_ref[...]
    vext[0:BLOCK, :] = vp_ref[...]
    vext[BLOCK:BLOCK + tq, :] = v_ref[...]
    vext[BLOCK + tq:, :] = vn_ref[...]

    band = 3 * BLOCK
    qi = lax.broadcasted_iota(jnp.int32, (BLOCK, band), 0)
    kj = lax.broadcasted_iota(jnp.int32, (BLOCK, band), 1)
    in_window = (kj >= qi) & (kj <= qi + 2 * BLOCK)
    head = lax.broadcasted_iota(jnp.int32, (GROUP, BLOCK, 1), 0)

    for blk in range(n_blk):
        rows = slice(blk * BLOCK, (blk + 1) * BLOCK)
        band_rows = slice(blk * BLOCK, blk * BLOCK + band)
        first = (t == 0) if blk == 0 else False
        last = (t == nt - 1) if blk == n_blk - 1 else False
        lo = jnp.where(first, BLOCK, 0)
        hi = jnp.where(last, 2 * BLOCK, band)
        valid = in_window & (kj >= lo) & (kj < hi)
        zeros = jnp.zeros((band, HEAD_DIM), BF16)
        for g in range(N_KV_HEADS):
            for src, dst in ((kext, kbd), (vext, vbd)):
                for gg in range(N_KV_HEADS):
                    cols = slice(gg * HEAD_DIM, (gg + 1) * HEAD_DIM)
                    dst[blk, g * band:(g + 1) * band, cols] = (
                        src[band_rows, cols] if gg == g else zeros)
        qcat = jnp.concatenate(
            [jnp.concatenate(
                [q_ref[rows, (g * GROUP + h) * HEAD_DIM:(g * GROUP + h + 1) * HEAD_DIM]
                 for g in range(N_KV_HEADS)], axis=1) for h in range(GROUP)], axis=0)
        s_all = lax.dot_general(qcat, kbd[blk], (((1,), (1,)), ((), ())),
                                preferred_element_type=F32)
        probs, denoms = [], []
        for g in range(N_KV_HEADS):
            s = s_all[:, g * band:(g + 1) * band].reshape(GROUP, BLOCK, band)
            s = jnp.where(valid[None], s, MASK_VALUE)
            sink = jnp.full((GROUP, BLOCK, 1), sink_ref[g * GROUP], F32)
            for h in range(1, GROUP):
                sink = jnp.where(head == h, sink_ref[g * GROUP + h], sink)
            m = jnp.maximum(jnp.max(s, axis=-1, keepdims=True), sink)
            p = jnp.exp(s - m)
            denoms.append(jnp.sum(p, axis=-1, keepdims=True) + jnp.exp(sink - m))
            probs.append(p.reshape(GROUP * BLOCK, band).astype(BF16))
        o_all = jnp.dot(jnp.concatenate(probs, axis=1), vbd[blk],
                        preferred_element_type=F32)
        for g in range(N_KV_HEADS):
            o = o_all[:, g * HEAD_DIM:(g + 1) * HEAD_DIM].reshape(GROUP, BLOCK, HEAD_DIM)
            o = o / denoms[g]
            for h in range(GROUP):
                cols = slice((g * GROUP + h) * HEAD_DIM, (g * GROUP + h + 1) * HEAD_DIM)
                attn_scr[rows, cols] = (o[h] * ga_ref[rows, cols].astype(F32)).astype(BF16)

        if (blk + 1) % MIX_BLOCKS == 0 or blk == n_blk - 1:
            lo_row = (blk // MIX_BLOCKS) * MIX_BLOCKS * BLOCK
            rows = slice(lo_row, (blk + 1) * BLOCK)
            a = jnp.dot(attn_scr[rows, :], wa_ref[...], preferred_element_type=F32)
            r = jnp.dot(rnn_ref[rows, :], wr_ref[...], preferred_element_type=F32)
            mixed = (mg_ref[rows, 0:D_MODEL].astype(F32) * a
                     + mg_ref[rows, D_MODEL:].astype(F32) * r)
            y = jnp.dot(mixed.astype(BF16), wo_ref[...], preferred_element_type=F32)
            out = x_ref[rows, :] + mod_ref[:, 2 * D_MODEL:] * y
            if final:
                ms = jnp.mean(out * out, axis=-1, keepdims=True)
                o_ref[rows, :] = out * lax.rsqrt(ms + EPS) * ng_ref[...]
            else:
                o_ref[rows, :] = out
                h_ref[rows, :] = _adaln(out, ng_ref[...], nmod_ref).astype(BF16)


def _attend_and_mix(sink, q, k, v, ga, rnn, mg, x, mod, w_attn, w_rnn, w_out, next_gain,
                    next_mod, batch_off, final):
    bsz, seq, _ = x.shape
    tq = _seq_tile(seq, 512)
    per = tq // BLOCK
    n_blocks = seq // BLOCK

    def tile(width):
        return pl.BlockSpec((None, tq, width), lambda b, t: (b, t, 0))

    prev_kv = pl.BlockSpec((None, BLOCK, KV_DIM), lambda b, t: (b, jnp.maximum(t * per - 1, 0), 0))
    next_kv = pl.BlockSpec((None, BLOCK, KV_DIM),
                           lambda b, t: (b, jnp.minimum((t + 1) * per, n_blocks - 1), 0))
    mod_spec = pl.BlockSpec((None, 1, 3 * D_MODEL), lambda b, t: (b + batch_off, 0, 0))
    out_specs = [tile(D_MODEL)]
    out_shape = [jax.ShapeDtypeStruct((bsz, seq, D_MODEL), F32)]
    if not final:
        out_specs.append(tile(D_MODEL))
        out_shape.append(jax.ShapeDtypeStruct((bsz, seq, D_MODEL), BF16))
    return pl.pallas_call(
        functools.partial(_attn_out_kernel, final=final),
        grid=(bsz, seq // tq),
        in_specs=[
            pl.BlockSpec(memory_space=pltpu.SMEM),
            tile(ATTN_DIM),
            tile(KV_DIM), prev_kv, next_kv,
            tile(KV_DIM), prev_kv, next_kv,
            tile(ATTN_DIM),
            tile(D_RNN),
            tile(2 * D_MODEL),
            tile(D_MODEL),
            mod_spec,
            _resident((ATTN_DIM, D_MODEL)),
            _resident((D_RNN, D_MODEL)),
            _resident((D_MODEL, D_MODEL)),
            _resident((1, D_MODEL)),
            mod_spec,
        ],
        out_specs=out_specs,
        out_shape=out_shape,
        scratch_shapes=[
            pltpu.VMEM((tq + 2 * BLOCK, KV_DIM), BF16),
            pltpu.VMEM((tq + 2 * BLOCK, KV_DIM), BF16),
            pltpu.VMEM((per, N_KV_HEADS * 3 * BLOCK, KV_DIM), BF16),
            pltpu.VMEM((per, N_KV_HEADS * 3 * BLOCK, KV_DIM), BF16),
            pltpu.VMEM((tq, ATTN_DIM), BF16),
        ],
        compiler_params=_params(2),
        name="attn_out",
    )(sink, q, k, k, k, v, v, v, ga, rnn, mg, x, mod, w_attn, w_rnn, w_out, next_gain, next_mod)


def _trunk(x, batch_off, mods, tables, layers, final_gain):
    cos, sin = tables
    seq = x.shape[1]
    cos, sin = cos[:seq], sin[:seq]
    depth = len(layers)
    h = _prenorm(x, mods[0], layers[0]["norm_gain"], batch_off)
    for l, p in enumerate(layers):
        final = l == depth - 1
        q, k, v, ga, rx, gr, mg = _project(h, p["w_in"], p["w_merge"], p["b_merge"], cos, sin)
        rnn = _rglru(rx, gr, p["conv_w"], p["conv_b"], p["w_gate"], p["lam"])
        next_gain = final_gain if final else layers[l + 1]["norm_gain"]
        next_mod = mods[l] if final else mods[l + 1]
        res = _attend_and_mix(p["sink"], q, k, v, ga, rnn, mg, x, mods[l], p["w_attn"],
                              p["w_rnn"], p["w_out"], next_gain, next_mod, batch_off, final)
        x = res[0]
        h = None if final else res[1]
    return x


def kernel(x_prompt, x_sample, c_prompt, c_sample, norm_gain, w_ada, b_ada, w_in, attn_sink, conv_w, conv_b, rg_w_a, rg_b_a, rg_w_x, rg_b_x, rg_lambda, w_attn_proj, w_rnn_proj, w_merge, b_merge, w_out, final_gain):
    depth = w_in.shape[0]
    n_prompt = x_prompt.shape[0]
    cond = jnp.concatenate([c_prompt, c_sample], axis=0)
    mods = _modulation(cond, w_ada, b_ada)
    mods = mods.reshape(depth, cond.shape[0], 1, 3 * D_MODEL)
    tables = _rope_tables(max(x_prompt.shape[1], x_sample.shape[1]))

    w_gate = jnp.concatenate([rg_w_a, rg_w_x], axis=-1).astype(BF16)
    b_half = 0.5 * jnp.concatenate(
        [rg_b_a.reshape(depth, 2, N_RNN_BLOCKS, 1, RNN_BLOCK),
         rg_b_x.reshape(depth, 2, N_RNN_BLOCKS, 1, RNN_BLOCK)], axis=-1)
    b_hi = b_half.astype(BF16)
    b_lo = (b_half - b_hi.astype(F32)).astype(BF16)
    w_gate = jnp.concatenate(
        [w_gate, b_hi, b_lo,
         jnp.zeros((depth, 2, N_RNN_BLOCKS, RNN_BLOCK - 2, 2 * RNN_BLOCK), BF16)], axis=-2)
    layers = []
    for l in range(depth):
        layers.append(dict(
            norm_gain=norm_gain[l].reshape(1, D_MODEL),
            w_in=w_in[l].astype(BF16),
            w_merge=w_merge[l].astype(BF16),
            b_merge=b_merge[l].reshape(1, 2 * D_MODEL),
            sink=attn_sink[l],
            conv_w=0.5 * conv_w[l],
            conv_b=0.5 * conv_b[l].reshape(1, D_RNN),
            w_gate=w_gate[l], lam=rg_lambda[l],
            w_attn=w_attn_proj[l].astype(BF16),
            w_rnn=w_rnn_proj[l].astype(BF16),
            w_out=w_out[l].astype(BF16),
        ))
    fg = final_gain.reshape(1, D_MODEL)
    y_prompt = _trunk(x_prompt, 0, mods, tables, layers, fg)
    y_sample = _trunk(x_sample, n_prompt, mods, tables, layers, fg)
    return (y_prompt, y_sample)
```

```python
import functools
import math

import jax
import jax.numpy as jnp
from jax import lax
from jax.experimental import pallas as pl
from jax.experimental.pallas import tpu as pltpu

D_MODEL = 1024
HEAD_DIM = 128
N_HEADS = 8
N_KV_HEADS = 2
GROUP = N_HEADS // N_KV_HEADS
ATTN_DIM = N_HEADS * HEAD_DIM
KV_DIM = N_KV_HEADS * HEAD_DIM
BLOCK = 128
ROPE_THETA = 10000.0
D_RNN = 3 * D_MODEL // 2
N_RNN_BLOCKS = 12
RNN_BLOCK = D_RNN // N_RNN_BLOCKS
CONV_WIDTH = 4
CONV_LEFT = 2
RGLRU_C = 8.0
EPS = 1e-6
MASK_VALUE = -1e30
LOG2E = math.log2(math.e)

SUBLANES = 8
LANES = 128
MXU_COLS = 256
VMEM_LIMIT_BYTES = 56 * 1024 * 1024

F32 = jnp.float32
BF16 = jnp.bfloat16


def _sigmoid(x):
    return 0.5 * jnp.tanh(0.5 * x) + 0.5


def _silu(x):
    return x * _sigmoid(x)


def _resident(shape):
    zeros = (0,) * len(shape)
    return pl.BlockSpec(shape, lambda *_: zeros, pipeline_mode=pl.Buffered(1))


def _params(n_axes):
    return pltpu.CompilerParams(dimension_semantics=("arbitrary",) * n_axes,
                                vmem_limit_bytes=VMEM_LIMIT_BYTES)


def _mod_kernel(c_ref, w_ref, b_ref, o_ref):
    c = c_ref[...]
    s = c * jax.nn.sigmoid(c)
    o_ref[...] = jnp.dot(s, w_ref[...], precision=lax.Precision.HIGHEST,
                         preferred_element_type=F32) + b_ref[...]


def _modulation(cond, w_ada, b_ada):
    depth = w_ada.shape[0]
    nb = cond.shape[0]
    n_col = 3 * D_MODEL // D_MODEL
    return pl.pallas_call(
        _mod_kernel,
        grid=(depth, n_col),
        in_specs=[
            pl.BlockSpec((nb, D_MODEL), lambda l, j: (0, 0)),
            pl.BlockSpec((None, D_MODEL, D_MODEL), lambda l, j: (l, 0, j)),
            pl.BlockSpec((None, 1, D_MODEL), lambda l, j: (l, 0, j)),
        ],
        out_specs=pl.BlockSpec((None, nb, D_MODEL), lambda l, j: (l, 0, j)),
        out_shape=jax.ShapeDtypeStruct((depth, nb, 3 * D_MODEL), F32),
        compiler_params=_params(2),
        name="adaln_mod",
    )(cond, w_ada, b_ada.reshape(depth, 1, 3 * D_MODEL))


def _rope_kernel(cos_ref, sin_ref):
    rows = cos_ref.shape[0]
    t = lax.broadcasted_iota(jnp.int32, (rows, HEAD_DIM), 0) + pl.program_id(0) * rows
    j = lax.broadcasted_iota(jnp.int32, (rows, HEAD_DIM), 1)
    half = HEAD_DIM // 2
    jj = jnp.where(j < half, j, j - half).astype(F32)
    inv_freq = jnp.exp(jj * (-2.0 / HEAD_DIM * math.log(ROPE_THETA)))
    ang = t.astype(F32) * inv_freq
    s = jnp.sin(ang)
    cos_ref[...] = jnp.cos(ang)
    sin_ref[...] = jnp.where(j < half, -s, s)


def _rope_tables(seq):
    rows = min(seq, 512)
    return pl.pallas_call(
        _rope_kernel,
        grid=(seq // rows,),
        out_specs=[pl.BlockSpec((rows, HEAD_DIM), lambda i: (i, 0))] * 2,
        out_shape=[jax.ShapeDtypeStruct((seq, HEAD_DIM), F32)] * 2,
        compiler_params=_params(1),
        name="rope_tables",
    )()


NORM_ROWS = 32


def _adaln(x, gain, mod_ref):
    ms = jnp.mean(x * x, axis=-1, keepdims=True)
    y = x * lax.rsqrt(ms + EPS) * gain
    return y * (1.0 + mod_ref[:, D_MODEL:2 * D_MODEL]) + mod_ref[:, 0:D_MODEL]


def _prenorm_kernel(x_ref, mod_ref, ng_ref, h_ref):
    gain = ng_ref[...]

    def norm_rows(r, carry):
        rows = pl.ds(pl.multiple_of(r * NORM_ROWS, NORM_ROWS), NORM_ROWS)
        h_ref[rows, :] = _adaln(x_ref[rows, :], gain, mod_ref).astype(BF16)
        return carry

    lax.fori_loop(0, x_ref.shape[0] // NORM_ROWS, norm_rows, 0, unroll=4)


def _prenorm(x, mod, norm_gain, batch_off):
    bsz, seq, _ = x.shape
    ts = _seq_tile(seq, 512)
    tile = pl.BlockSpec((None, ts, D_MODEL), lambda b, t: (b, t, 0))
    return pl.pallas_call(
        _prenorm_kernel,
        grid=(bsz, seq // ts),
        in_specs=[tile,
                  pl.BlockSpec((None, 1, 3 * D_MODEL), lambda b, t: (b + batch_off, 0, 0)),
                  _resident((1, D_MODEL))],
        out_specs=tile,
        out_shape=jax.ShapeDtypeStruct((bsz, seq, D_MODEL), BF16),
        compiler_params=_params(2),
        name="prenorm",
    )(x, mod, norm_gain)


def _proj_kernel(h_ref, win_ref, wm_ref, bm_ref, cos_ref, sin_ref,
                 q_ref, k_ref, v_ref, ga_ref, rx_ref, gr_ref, mg_ref, *, c_len):
    ts = h_ref.shape[0]
    cos = cos_ref[...]
    sin = sin_ref[...]

    def rotary(p):
        return p * cos + pltpu.roll(p, HEAD_DIM // 2, axis=1) * sin

    def chunk(w_ref, c0):
        return jnp.dot(h_ref[...], w_ref[:, c0:c0 + MXU_COLS], preferred_element_type=F32)

    qk_scale = HEAD_DIM ** -0.5 * LOG2E
    col = 0
    for c in range(ATTN_DIM // MXU_COLS):
        p = chunk(win_ref, col + c * MXU_COLS)
        for hh in range(MXU_COLS // HEAD_DIM):
            dst = slice(c * MXU_COLS + hh * HEAD_DIM, c * MXU_COLS + (hh + 1) * HEAD_DIM)
            q_ref[:, dst] = (rotary(p[:, hh * HEAD_DIM:(hh + 1) * HEAD_DIM]) * qk_scale).astype(BF16)
    col += ATTN_DIM
    p = chunk(win_ref, col)
    for hh in range(N_KV_HEADS):
        dst = slice(hh * HEAD_DIM, (hh + 1) * HEAD_DIM)
        k_ref[:, dst] = rotary(p[:, dst]).astype(BF16)
    col += KV_DIM
    v_ref[...] = chunk(win_ref, col).astype(BF16)
    col += KV_DIM
    for c in range(ATTN_DIM // MXU_COLS):
        dst = slice(c * MXU_COLS, (c + 1) * MXU_COLS)
        ga_ref[:, dst] = _silu(chunk(win_ref, col + c * MXU_COLS)).astype(BF16)
    col += ATTN_DIM
    pitch = rx_ref.shape[0] * c_len // ts
    for c in range(D_RNN // MXU_COLS):
        dst = slice(c * MXU_COLS, (c + 1) * MXU_COLS)
        p = chunk(win_ref, col + c * MXU_COLS)
        for kk in range(ts // c_len):
            rx_ref[kk * pitch:kk * pitch + c_len, dst] = p[kk * c_len:(kk + 1) * c_len]
            rx_ref[kk * pitch + c_len:(kk + 1) * pitch, dst] = jnp.zeros(
                (pitch - c_len, MXU_COLS), F32)
    col += D_RNN
    for c in range(D_RNN // MXU_COLS):
        dst = slice(c * MXU_COLS, (c + 1) * MXU_COLS)
        gr_ref[:, dst] = _silu(chunk(win_ref, col + c * MXU_COLS)).astype(BF16)
    for c in range(2 * D_MODEL // MXU_COLS):
        dst = slice(c * MXU_COLS, (c + 1) * MXU_COLS)
        mg_ref[:, dst] = _sigmoid(chunk(wm_ref, c * MXU_COLS) + bm_ref[:, dst]).astype(BF16)


def _seq_tile(seq, want):
    return want if seq % want == 0 else seq


def _project(h, w_in, w_merge, b_merge, cos, sin):
    bsz, seq, _ = h.shape
    ts = _seq_tile(seq, 512)
    d_in = w_in.shape[1]
    c_len, pitch = _chunking(seq)
    assert ts % c_len == 0
    rx_rows = ts // c_len * pitch

    def tile(width, rows=ts):
        return pl.BlockSpec((None, rows, width), lambda b, t: (b, t, 0))

    outs = [(ATTN_DIM, BF16, seq, ts), (KV_DIM, BF16, seq, ts), (KV_DIM, BF16, seq, ts),
            (ATTN_DIM, BF16, seq, ts), (D_RNN, F32, N_CHUNKS * pitch, rx_rows),
            (D_RNN, BF16, seq, ts), (2 * D_MODEL, BF16, seq, ts)]
    return pl.pallas_call(
        functools.partial(_proj_kernel, c_len=c_len),
        grid=(bsz, seq // ts),
        in_specs=[
            tile(D_MODEL),
            _resident((D_MODEL, d_in)),
            _resident((D_MODEL, 2 * D_MODEL)),
            _resident((1, 2 * D_MODEL)),
            pl.BlockSpec((ts, HEAD_DIM), lambda b, t: (t, 0)),
            pl.BlockSpec((ts, HEAD_DIM), lambda b, t: (t, 0)),
        ],
        out_specs=[tile(w, rows) for w, _, _, rows in outs],
        out_shape=[jax.ShapeDtypeStruct((bsz, total, w), dt) for w, dt, total, _ in outs],
        compiler_params=_params(2),
        name="in_proj",
    )(h, w_in, w_merge, b_merge, cos, sin)


N_SETS = 8
N_CHUNKS = N_SETS * SUBLANES
STEP_ROWS = N_SETS * SUBLANES
PITCH_PAD = SUBLANES
EXT_SLOTS = CONV_WIDTH - 1
PHASE_ROWS = 256
SCAN_STEPS = 4
GATHER_STEPS = PHASE_ROWS // STEP_ROWS
TINY = 1e-30


def _chunking(seq):
    c_len = seq // N_CHUNKS
    assert seq % N_CHUNKS == 0 and (c_len * STEP_ROWS) % PHASE_ROWS == 0
    assert c_len % SCAN_STEPS == 0 and c_len % GATHER_STEPS == 0
    assert c_len % (2 * SUBLANES) == 0
    return c_len, c_len + PITCH_PAD


def _sublane_scan(a, b, reverse):
    row = lax.broadcasted_iota(jnp.int32, a.shape, 0)
    for d in (1, 2, 4):
        if reverse:
            keep, shift = row < SUBLANES - d, SUBLANES - d
        else:
            keep, shift = row >= d, d
        a_s = pltpu.roll(a, shift, axis=0)
        b_s = pltpu.roll(b, shift, axis=0)
        b = jnp.where(keep, a * b_s + b, b)
        a = jnp.where(keep, a * a_s, a)
    return a, b


def _rglru_kernel(rx_ref, gr_ref, cw_ref, cb_ref, wg_ref, lam_ref, out_ref,
                  xs, ext, a_scr, b_scr, hs):
    seq = out_ref.shape[0]
    c_len, pitch = _chunking(seq)
    row8 = lax.broadcasted_iota(jnp.int32, (SUBLANES, LANES), 0)
    zero8 = jnp.zeros((SUBLANES, LANES), F32)

    def sets(v):
        return [v[s * SUBLANES:(s + 1) * SUBLANES] for s in range(N_SETS)]

    def slot(i):
        return slice((i + CONV_LEFT) * STEP_ROWS, (i + CONV_LEFT + 1) * STEP_ROWS)

    n_gather = c_len // GATHER_STEPS

    def gather(g):
        i0 = g * GATHER_STEPS
        vals = []
        for kk in range(GATHER_STEPS):
            for s in range(N_SETS):
                vals.append(rx_ref[pl.ds(s * SUBLANES * pitch + i0 + kk, SUBLANES, stride=pitch), :])
        dst = pl.multiple_of((i0 + CONV_LEFT) * STEP_ROWS, STEP_ROWS)
        ext[pl.ds(dst, GATHER_STEPS * STEP_ROWS), :] = jnp.concatenate(vals, axis=0)

    for g in sorted({0, min(1, n_gather - 1), n_gather - 1}):
        gather(g)

    for back in (1, 2):
        tail = [pltpu.roll(v, 1, axis=0) for v in sets(ext[slot(c_len - back), :])]
        ext[slot(-back), :] = jnp.concatenate(
            [jnp.where(row8 == 0, tail[s - 1] if s > 0 else zero8, tail[s])
             for s in range(N_SETS)], axis=0)
    head = [pltpu.roll(v, SUBLANES - 1, axis=0) for v in sets(ext[slot(0), :])]
    ext[slot(c_len), :] = jnp.concatenate(
        [jnp.where(row8 == SUBLANES - 1, head[s + 1] if s < N_SETS - 1 else zero8, head[s])
         for s in range(N_SETS)], axis=0)

    neg = -lam_ref[...]
    softplus = jnp.maximum(neg, 0.0) + jnp.log(1.0 + jnp.exp(-jnp.abs(neg)))
    kexp = softplus * (-RGLRU_C * 0.5 * math.log2(math.e))
    cw = cw_ref[...]
    cb = cb_ref[...]
    ones16 = jnp.ones((PHASE_ROWS, RNN_BLOCK), BF16)

    def phase(n, carry):
        gather(jnp.minimum(n + 2, n_gather - 1))
        src = n * PHASE_ROWS
        dst = pl.ds(pl.multiple_of(src, PHASE_ROWS), PHASE_ROWS)
        y = cb
        for tap in range(CONV_WIDTH):
            rows = pl.ds(pl.multiple_of(src + tap * STEP_ROWS, STEP_ROWS), PHASE_ROWS)
            y = y + ext[rows, :] * cw[tap:tap + 1, :]
        lhs = jnp.concatenate([y.astype(BF16), ones16], axis=1)
        for d in range(2):
            g = jnp.dot(lhs, wg_ref[d], preferred_element_type=F32)
            ta = jnp.tanh(g[:, :RNN_BLOCK])
            ti = jnp.tanh(g[:, RNN_BLOCK:])
            k = kexp[d:d + 1, :]
            a = jnp.exp2(k * ta + k)
            z = 1.0 - a * a
            m = z * lax.rsqrt(jnp.maximum(z, TINY))
            a_scr[d, dst, :] = a
            b_scr[d, dst, :] = m * ((ti + 1.0) * y)
        return carry

    lax.fori_loop(0, seq // PHASE_ROWS, phase, 0, unroll=8)

    n_blocks = c_len // SCAN_STEPS
    block_rows = SCAN_STEPS * STEP_ROWS

    def order(d):
        return range(SCAN_STEPS - 1, -1, -1) if d else range(SCAN_STEPS)

    def block(d, n):
        blk = (n_blocks - 1 - n) if d else n
        return blk, pl.ds(pl.multiple_of(blk * block_rows, block_rows), block_rows)

    def piece(v, kk, s):
        r0 = (kk * N_SETS + s) * SUBLANES
        return v[r0:r0 + SUBLANES]

    def summarise(d, n, carry):
        _, rows = block(d, n)
        av = a_scr[d, rows, :]
        bv = b_scr[d, rows, :]
        carry = list(carry)
        for kk in order(d):
            for s in range(N_SETS):
                a = piece(av, kk, s)
                h, p = carry[s]
                carry[s] = (a * h + piece(bv, kk, s), a * p)
        return tuple(carry)

    def resolve(d, ends):
        edge = zero8
        start = [None] * N_SETS
        for s in (range(N_SETS - 1, -1, -1) if d else range(N_SETS)):
            p_cum, h_cum = _sublane_scan(ends[s][1], ends[s][0], bool(d))
            after = p_cum * edge + h_cum
            if d:
                start[s] = jnp.where(row8 == SUBLANES - 1, edge,
                                     pltpu.roll(after, SUBLANES - 1, axis=0))
                edge = jnp.broadcast_to(after[0:1, :], (SUBLANES, LANES))
            else:
                start[s] = jnp.where(row8 == 0, edge, pltpu.roll(after, 1, axis=0))
                edge = jnp.broadcast_to(after[SUBLANES - 1:SUBLANES, :], (SUBLANES, LANES))
        return tuple(start)

    def rescan(d, n, carry):
        blk, rows = block(d, n)
        av = a_scr[d, rows, :]
        bv = b_scr[d, rows, :]
        hv = hs[rows, :] if d else None
        carry = list(carry)
        out = [None] * (SCAN_STEPS * N_SETS)
        for kk in order(d):
            for s in range(N_SETS):
                h = piece(av, kk, s) * carry[s] + piece(bv, kk, s)
                carry[s] = h
                if d:
                    i = blk * SCAN_STEPS + kk
                    xs[pl.ds(s * SUBLANES * pitch + i, SUBLANES, stride=pitch), :] = (
                        h + piece(hv, kk, s))
                else:
                    out[kk * N_SETS + s] = h
        if not d:
            hs[rows, :] = jnp.concatenate(out, axis=0)
        return tuple(carry)

    ones8 = jnp.ones((SUBLANES, LANES), F32)
    fresh = tuple((zero8, ones8) for _ in range(N_SETS))
    ends_f = lax.fori_loop(0, n_blocks, functools.partial(summarise, 0), fresh)
    _, ends_b = lax.fori_loop(
        0, n_blocks, lambda n, c: (rescan(0, n, c[0]), summarise(1, n, c[1])),
        (resolve(0, ends_f), fresh))
    lax.fori_loop(0, n_blocks, functools.partial(rescan, 1), resolve(1, ends_b))

    for ch in range(N_CHUNKS):
        rows = slice(ch * c_len, (ch + 1) * c_len)
        h = xs[ch * pitch:ch * pitch + c_len, :]
        out_ref[rows, :] = h.astype(BF16) * gr_ref[rows, :]


def _rglru(rnn_x, gate, conv_w, conv_b, w_gate, lam):
    bsz, seq, _ = gate.shape
    c_len, pitch = _chunking(seq)
    assert rnn_x.shape[1] == N_CHUNKS * pitch

    def col_block(rows):
        return pl.BlockSpec((None, rows, RNN_BLOCK), lambda b, c: (b, 0, c))

    return pl.pallas_call(
        _rglru_kernel,
        grid=(bsz, N_RNN_BLOCKS),
        in_specs=[
            col_block(N_CHUNKS * pitch),
            col_block(seq),
            pl.BlockSpec((CONV_WIDTH, RNN_BLOCK), lambda b, c: (0, c)),
            pl.BlockSpec((1, RNN_BLOCK), lambda b, c: (0, c)),
            pl.BlockSpec((2, None, 2 * RNN_BLOCK, 2 * RNN_BLOCK), lambda b, c: (0, c, 0, 0)),
            pl.BlockSpec((2, RNN_BLOCK), lambda b, c: (0, c)),
        ],
        out_specs=col_block(seq),
        out_shape=jax.ShapeDtypeStruct((bsz, seq, D_RNN), BF16),
        scratch_shapes=[
            pltpu.VMEM((N_CHUNKS * pitch, RNN_BLOCK), F32),
            pltpu.VMEM(((c_len + EXT_SLOTS) * STEP_ROWS, RNN_BLOCK), F32),
            pltpu.VMEM((2, seq, RNN_BLOCK), F32),
            pltpu.VMEM((2, seq, RNN_BLOCK), F32),
            pltpu.VMEM((seq, RNN_BLOCK), F32),
        ],
        compiler_params=_params(2),
        name="rglru",
    )(rnn_x, gate, conv_w, conv_b, w_gate, lam)


MIX_BLOCKS = 4

def _attn_out_kernel(sink_ref, q_ref, k_ref, kp_ref, kn_ref, v_ref, vp_ref, vn_ref, ga_ref,
                     rnn_ref, mg_ref, x_ref, mod_ref, wa_ref, wr_ref, wo_ref, ng_ref, nmod_ref,
                     *rest, final):
    if final:
        o_ref, kext, vext, kbd, vbd, attn_scr = rest
    else:
        o_ref, h_ref, kext, vext, kbd, vbd, attn_scr = rest
    tq = q_ref.shape[0]
    t = pl.program_id(1)
    nt = pl.num_programs(1)
    n_blk = tq // BLOCK

    kext[0:BLOCK, :] = kp_ref[...]
    kext[BLOCK:BLOCK + tq, :] = k_ref[...]
    kext[BLOCK + tq:, :] = kn_ref[...]
    vext[0:BLOCK, :] = vp_ref[...]
    vext[BLOCK:BLOCK + tq, :] = v_ref[...]
    vext[BLOCK + tq:, :] = vn_ref[...]

    band = 3 * BLOCK
    qi = lax.broadcasted_iota(jnp.int32, (BLOCK, band), 0)
    kj = lax.broadcasted_iota(jnp.int32, (BLOCK, band), 1)
    in_window = (kj >= qi) & (kj <= qi + 2 * BLOCK)
    head = lax.broadcasted_iota(jnp.int32, (GROUP, BLOCK, 1), 0)

    for blk in range(n_blk):
        rows = slice(blk * BLOCK, (blk + 1) * BLOCK)
        band_rows = slice(blk * BLOCK, blk * BLOCK + band)
        first = (t == 0) if blk == 0 else False
        last = (t == nt - 1) if blk == n_blk - 1 else False
        lo = jnp.where(first, BLOCK, 0)
        hi = jnp.where(last, 2 * BLOCK, band)
        valid = in_window & (kj >= lo) & (kj < hi)
        zeros = jnp.zeros((band, HEAD_DIM), BF16)
        for g in range(N_KV_HEADS):
            for src, dst in ((kext, kbd), (vext, vbd)):
                for gg in range(N_KV_HEADS):
                    cols = slice(gg * HEAD_DIM, (gg + 1) * HEAD_DIM)
                    dst[blk, g * band:(g + 1) * band, cols] = (
                        src[band_rows, cols] if gg == g else zeros)
        qcat = jnp.concatenate(
            [jnp.concatenate(
                [q_ref[rows, (g * GROUP + h) * HEAD_DIM:(g * GROUP + h + 1) * HEAD_DIM]
                 for g in range(N_KV_HEADS)], axis=1) for h in range(GROUP)], axis=0)
        s_all = lax.dot_general(qcat, kbd[blk], (((1,), (1,)), ((), ())),
                                preferred_element_type=F32)
        probs, denoms = [], []
        for g in range(N_KV_HEADS):
            s = s_all[:, g * band:(g + 1) * band].reshape(GROUP, BLOCK, band)
            s = jnp.where(valid[None], s, MASK_VALUE)
            sink = jnp.full((GROUP, BLOCK, 1), sink_ref[g * GROUP] * LOG2E, F32)
            for h in range(1, GROUP):
                sink = jnp.where(head == h, sink_ref[g * GROUP + h] * LOG2E, sink)
            m = jnp.maximum(jnp.max(s, axis=-1, keepdims=True), sink)
            p = jnp.exp2(s - m)
            denoms.append(jnp.sum(p, axis=-1, keepdims=True) + jnp.exp2(sink - m))
            probs.append(p.reshape(GROUP * BLOCK, band).astype(BF16))
        o_all = jnp.dot(jnp.concatenate(probs, axis=1), vbd[blk],
                        preferred_element_type=F32)
        for g in range(N_KV_HEADS):
            o = o_all[:, g * HEAD_DIM:(g + 1) * HEAD_DIM].reshape(GROUP, BLOCK, HEAD_DIM)
            o = o / denoms[g]
            for h in range(GROUP):
                cols = slice((g * GROUP + h) * HEAD_DIM, (g * GROUP + h + 1) * HEAD_DIM)
                attn_scr[rows, cols] = (o[h] * ga_ref[rows, cols].astype(F32)).astype(BF16)

        if (blk + 1) % MIX_BLOCKS == 0 or blk == n_blk - 1:
            lo_row = (blk // MIX_BLOCKS) * MIX_BLOCKS * BLOCK
            rows = slice(lo_row, (blk + 1) * BLOCK)
            a = jnp.dot(attn_scr[rows, :], wa_ref[...], preferred_element_type=F32)
            r = jnp.dot(rnn_ref[rows, :], wr_ref[...], preferred_element_type=F32)
            mixed = (mg_ref[rows, 0:D_MODEL].astype(F32) * a
                     + mg_ref[rows, D_MODEL:].astype(F32) * r)
            y = jnp.dot(mixed.astype(BF16), wo_ref[...], preferred_element_type=F32)
            out = x_ref[rows, :] + mod_ref[:, 2 * D_MODEL:] * y
            if final:
                ms = jnp.mean(out * out, axis=-1, keepdims=True)
                o_ref[rows, :] = out * lax.rsqrt(ms + EPS) * ng_ref[...]
            else:
                o_ref[rows, :] = out
                h_ref[rows, :] = _adaln(out, ng_ref[...], nmod_ref).astype(BF16)


def _attend_and_mix(sink, q, k, v, ga, rnn, mg, x, mod, w_attn, w_rnn, w_out, next_gain,
                    next_mod, batch_off, final):
    bsz, seq, _ = x.shape
    tq = _seq_tile(seq, 512)
    per = tq // BLOCK
    n_blocks = seq // BLOCK

    def tile(width):
        return pl.BlockSpec((None, tq, width), lambda b, t: (b, t, 0))

    prev_kv = pl.BlockSpec((None, BLOCK, KV_DIM), lambda b, t: (b, jnp.maximum(t * per - 1, 0), 0))
    next_kv = pl.BlockSpec((None, BLOCK, KV_DIM),
                           lambda b, t: (b, jnp.minimum((t + 1) * per, n_blocks - 1), 0))
    mod_spec = pl.BlockSpec((None, 1, 3 * D_MODEL), lambda b, t: (b + batch_off, 0, 0))
    out_specs = [tile(D_MODEL)]
    out_shape = [jax.ShapeDtypeStruct((bsz, seq, D_MODEL), F32)]
    if not final:
        out_specs.append(tile(D_MODEL))
        out_shape.append(jax.ShapeDtypeStruct((bsz, seq, D_MODEL), BF16))
    return pl.pallas_call(
        functools.partial(_attn_out_kernel, final=final),
        grid=(bsz, seq // tq),
        in_specs=[
            pl.BlockSpec(memory_space=pltpu.SMEM),
            tile(ATTN_DIM),
            tile(KV_DIM), prev_kv, next_kv,
            tile(KV_DIM), prev_kv, next_kv,
            tile(ATTN_DIM),
            tile(D_RNN),
            tile(2 * D_MODEL),
            tile(D_MODEL),
            mod_spec,
            _resident((ATTN_DIM, D_MODEL)),
            _resident((D_RNN, D_MODEL)),
            _resident((D_MODEL, D_MODEL)),
            _resident((1, D_MODEL)),
            mod_spec,
        ],
        out_specs=out_specs,
        out_shape=out_shape,
        scratch_shapes=[
            pltpu.VMEM((tq + 2 * BLOCK, KV_DIM), BF16),
            pltpu.VMEM((tq + 2 * BLOCK, KV_DIM), BF16),
            pltpu.VMEM((per, N_KV_HEADS * 3 * BLOCK, KV_DIM), BF16),
            pltpu.VMEM((per, N_KV_HEADS * 3 * BLOCK, KV_DIM), BF16),
            pltpu.VMEM((tq, ATTN_DIM), BF16),
        ],
        compiler_params=_params(2),
        name="attn_out",
    )(sink, q, k, k, k, v, v, v, ga, rnn, mg, x, mod, w_attn, w_rnn, w_out, next_gain, next_mod)


def _trunk(x, batch_off, mods, tables, layers, final_gain):
    cos, sin = tables
    seq = x.shape[1]
    cos, sin = cos[:seq], sin[:seq]
    depth = len(layers)
    h = _prenorm(x, mods[0], layers[0]["norm_gain"], batch_off)
    for l, p in enumerate(layers):
        final = l == depth - 1
        q, k, v, ga, rx, gr, mg = _project(h, p["w_in"], p["w_merge"], p["b_merge"], cos, sin)
        rnn = _rglru(rx, gr, p["conv_w"], p["conv_b"], p["w_gate"], p["lam"])
        next_gain = final_gain if final else layers[l + 1]["norm_gain"]
        next_mod = mods[l] if final else mods[l + 1]
        res = _attend_and_mix(p["sink"], q, k, v, ga, rnn, mg, x, mods[l], p["w_attn"],
                              p["w_rnn"], p["w_out"], next_gain, next_mod, batch_off, final)
        x = res[0]
        h = None if final else res[1]
    return x


def kernel(x_prompt, x_sample, c_prompt, c_sample, norm_gain, w_ada, b_ada, w_in, attn_sink, conv_w, conv_b, rg_w_a, rg_b_a, rg_w_x, rg_b_x, rg_lambda, w_attn_proj, w_rnn_proj, w_merge, b_merge, w_out, final_gain):
    depth = w_in.shape[0]
    n_prompt = x_prompt.shape[0]
    cond = jnp.concatenate([c_prompt, c_sample], axis=0)
    mods = _modulation(cond, w_ada, b_ada)
    mods = mods.reshape(depth, cond.shape[0], 1, 3 * D_MODEL)
    tables = _rope_tables(max(x_prompt.shape[1], x_sample.shape[1]))

    w_gate = jnp.concatenate([rg_w_a, rg_w_x], axis=-1).astype(BF16)
    b_half = 0.5 * jnp.concatenate(
        [rg_b_a.reshape(depth, 2, N_RNN_BLOCKS, 1, RNN_BLOCK),
         rg_b_x.reshape(depth, 2, N_RNN_BLOCKS, 1, RNN_BLOCK)], axis=-1)
    b_hi = b_half.astype(BF16)
    b_lo = (b_half - b_hi.astype(F32)).astype(BF16)
    w_gate = jnp.concatenate(
        [w_gate, b_hi, b_lo,
         jnp.zeros((depth, 2, N_RNN_BLOCKS, RNN_BLOCK - 2, 2 * RNN_BLOCK), BF16)], axis=-2)
    layers = []
    for l in range(depth):
        layers.append(dict(
            norm_gain=norm_gain[l].reshape(1, D_MODEL),
            w_in=w_in[l].astype(BF16),
            w_merge=w_merge[l].astype(BF16),
            b_merge=b_merge[l].reshape(1, 2 * D_MODEL),
            sink=attn_sink[l],
            conv_w=0.5 * conv_w[l],
            conv_b=0.5 * conv_b[l].reshape(1, D_RNN),
            w_gate=w_gate[l], lam=rg_lambda[l],
            w_attn=w_attn_proj[l].astype(BF16),
            w_rnn=w_rnn_proj[l].astype(BF16),
            w_out=w_out[l].astype(BF16),
        ))
    fg = final_gain.reshape(1, D_MODEL)
    y_prompt = _trunk(x_prompt, 0, mods, tables, layers, fg)
    y_sample = _trunk(x_sample, n_prompt, mods, tables, layers, fg)
    return (y_prompt, y_sample)
```

```python
import functools
import math

import jax
import jax.numpy as jnp
from jax import lax
from jax.experimental import pallas as pl
from jax.experimental.pallas import tpu as pltpu

D_MODEL = 1024
HEAD_DIM = 128
N_HEADS = 8
N_KV_HEADS = 2
GROUP = N_HEADS // N_KV_HEADS
ATTN_DIM = N_HEADS * HEAD_DIM
KV_DIM = N_KV_HEADS * HEAD_DIM
BLOCK = 128
ROPE_THETA = 10000.0
D_RNN = 3 * D_MODEL // 2
N_RNN_BLOCKS = 12
RNN_BLOCK = D_RNN // N_RNN_BLOCKS
CONV_WIDTH = 4
CONV_LEFT = 2
RGLRU_C = 8.0
EPS = 1e-6
MASK_VALUE = -1e30
LOG2E = math.log2(math.e)

SUBLANES = 8
LANES = 128
MXU_COLS = 256
VMEM_LIMIT_BYTES = 56 * 1024 * 1024

F32 = jnp.float32
BF16 = jnp.bfloat16


def _sigmoid(x):
    return 0.5 * jnp.tanh(0.5 * x) + 0.5


def _silu(x):
    return x * _sigmoid(x)


def _resident(shape):
    zeros = (0,) * len(shape)
    return pl.BlockSpec(shape, lambda *_: zeros, pipeline_mode=pl.Buffered(1))


def _params(n_axes):
    return pltpu.CompilerParams(dimension_semantics=("arbitrary",) * n_axes,
                                vmem_limit_bytes=VMEM_LIMIT_BYTES)


def _mod_kernel(c_ref, w_ref, b_ref, o_ref):
    c = c_ref[...]
    s = c * jax.nn.sigmoid(c)
    o_ref[...] = jnp.dot(s, w_ref[...], precision=lax.Precision.HIGHEST,
                         preferred_element_type=F32) + b_ref[...]


def _modulation(cond, w_ada, b_ada):
    depth = w_ada.shape[0]
    nb = cond.shape[0]
    n_col = 3 * D_MODEL // D_MODEL
    return pl.pallas_call(
        _mod_kernel,
        grid=(depth, n_col),
        in_specs=[
            pl.BlockSpec((nb, D_MODEL), lambda l, j: (0, 0)),
            pl.BlockSpec((None, D_MODEL, D_MODEL), lambda l, j: (l, 0, j)),
            pl.BlockSpec((None, 1, D_MODEL), lambda l, j: (l, 0, j)),
        ],
        out_specs=pl.BlockSpec((None, nb, D_MODEL), lambda l, j: (l, 0, j)),
        out_shape=jax.ShapeDtypeStruct((depth, nb, 3 * D_MODEL), F32),
        compiler_params=_params(2),
        name="adaln_mod",
    )(cond, w_ada, b_ada.reshape(depth, 1, 3 * D_MODEL))


def _rope_kernel(cos_ref, sin_ref):
    rows = cos_ref.shape[0]
    t = lax.broadcasted_iota(jnp.int32, (rows, HEAD_DIM), 0) + pl.program_id(0) * rows
    j = lax.broadcasted_iota(jnp.int32, (rows, HEAD_DIM), 1)
    half = HEAD_DIM // 2
    jj = jnp.where(j < half, j, j - half).astype(F32)
    inv_freq = jnp.exp(jj * (-2.0 / HEAD_DIM * math.log(ROPE_THETA)))
    ang = t.astype(F32) * inv_freq
    s = jnp.sin(ang)
    cos_ref[...] = jnp.cos(ang)
    sin_ref[...] = jnp.where(j < half, -s, s)


def _rope_tables(seq):
    rows = min(seq, 512)
    return pl.pallas_call(
        _rope_kernel,
        grid=(seq // rows,),
        out_specs=[pl.BlockSpec((rows, HEAD_DIM), lambda i: (i, 0))] * 2,
        out_shape=[jax.ShapeDtypeStruct((seq, HEAD_DIM), F32)] * 2,
        compiler_params=_params(1),
        name="rope_tables",
    )()


NORM_ROWS = 32


def _adaln(x, gain, mod_ref):
    ms = jnp.mean(x * x, axis=-1, keepdims=True)
    y = x * lax.rsqrt(ms + EPS) * gain
    return y * (1.0 + mod_ref[:, D_MODEL:2 * D_MODEL]) + mod_ref[:, 0:D_MODEL]


def _prenorm_kernel(x_ref, mod_ref, ng_ref, h_ref):
    gain = ng_ref[...]

    def norm_rows(r, carry):
        rows = pl.ds(pl.multiple_of(r * NORM_ROWS, NORM_ROWS), NORM_ROWS)
        h_ref[rows, :] = _adaln(x_ref[rows, :], gain, mod_ref).astype(BF16)
        return carry

    lax.fori_loop(0, x_ref.shape[0] // NORM_ROWS, norm_rows, 0, unroll=4)


def _prenorm(x, mod, norm_gain, batch_off):
    bsz, seq, _ = x.shape
    ts = _seq_tile(seq, 512)
    tile = pl.BlockSpec((None, ts, D_MODEL), lambda b, t: (b, t, 0))
    return pl.pallas_call(
        _prenorm_kernel,
        grid=(bsz, seq // ts),
        in_specs=[tile,
                  pl.BlockSpec((None, 1, 3 * D_MODEL), lambda b, t: (b + batch_off, 0, 0)),
                  _resident((1, D_MODEL))],
        out_specs=tile,
        out_shape=jax.ShapeDtypeStruct((bsz, seq, D_MODEL), BF16),
        compiler_params=_params(2),
        name="prenorm",
    )(x, mod, norm_gain)


def _proj_kernel(h_ref, win_ref, wm_ref, bm_ref, cos_ref, sin_ref,
                 q_ref, k_ref, v_ref, ga_ref, rx_ref, gr_ref, mg_ref, *, c_len):
    ts = h_ref.shape[0]
    cos = cos_ref[...]
    sin = sin_ref[...]

    def rotary(p):
        return p * cos + pltpu.roll(p, HEAD_DIM // 2, axis=1) * sin

    def chunk(w_ref, c0):
        return jnp.dot(h_ref[...], w_ref[:, c0:c0 + MXU_COLS], preferred_element_type=F32)

    qk_scale = HEAD_DIM ** -0.5 * LOG2E
    col = 0
    for c in range(ATTN_DIM // MXU_COLS):
        p = chunk(win_ref, col + c * MXU_COLS)
        for hh in range(MXU_COLS // HEAD_DIM):
            dst = slice(c * MXU_COLS + hh * HEAD_DIM, c * MXU_COLS + (hh + 1) * HEAD_DIM)
            q_ref[:, dst] = (rotary(p[:, hh * HEAD_DIM:(hh + 1) * HEAD_DIM]) * qk_scale).astype(BF16)
    col += ATTN_DIM
    p = chunk(win_ref, col)
    for hh in range(N_KV_HEADS):
        dst = slice(hh * HEAD_DIM, (hh + 1) * HEAD_DIM)
        k_ref[:, dst] = rotary(p[:, dst]).astype(BF16)
    col += KV_DIM
    v_ref[...] = chunk(win_ref, col).astype(BF16)
    col += KV_DIM
    for c in range(ATTN_DIM // MXU_COLS):
        dst = slice(c * MXU_COLS, (c + 1) * MXU_COLS)
        ga_ref[:, dst] = _silu(chunk(win_ref, col + c * MXU_COLS)).astype(BF16)
    col += ATTN_DIM
    pitch = rx_ref.shape[0] * c_len // ts
    for c in range(D_RNN // MXU_COLS):
        dst = slice(c * MXU_COLS, (c + 1) * MXU_COLS)
        p = chunk(win_ref, col + c * MXU_COLS)
        for kk in range(ts // c_len):
            rx_ref[kk * pitch:kk * pitch + c_len, dst] = p[kk * c_len:(kk + 1) * c_len]
            rx_ref[kk * pitch + c_len:(kk + 1) * pitch, dst] = jnp.zeros(
                (pitch - c_len, MXU_COLS), F32)
    col += D_RNN
    for c in range(D_RNN // MXU_COLS):
        dst = slice(c * MXU_COLS, (c + 1) * MXU_COLS)
        gr_ref[:, dst] = _silu(chunk(win_ref, col + c * MXU_COLS)).astype(BF16)
    for c in range(2 * D_MODEL // MXU_COLS):
        dst = slice(c * MXU_COLS, (c + 1) * MXU_COLS)
        mg_ref[:, dst] = _sigmoid(chunk(wm_ref, c * MXU_COLS) + bm_ref[:, dst]).astype(BF16)


def _seq_tile(seq, want):
    return want if seq % want == 0 else seq


def _project(h, w_in, w_merge, b_merge, cos, sin):
    bsz, seq, _ = h.shape
    ts = _seq_tile(seq, 512)
    d_in = w_in.shape[1]
    c_len, pitch = _chunking(seq)
    assert ts % c_len == 0
    rx_rows = ts // c_len * pitch

    def tile(width, rows=ts):
        return pl.BlockSpec((None, rows, width), lambda b, t: (b, t, 0))

    outs = [(ATTN_DIM, BF16, seq, ts), (KV_DIM, BF16, seq, ts), (KV_DIM, BF16, seq, ts),
            (ATTN_DIM, BF16, seq, ts), (D_RNN, F32, N_CHUNKS * pitch, rx_rows),
            (D_RNN, BF16, seq, ts), (2 * D_MODEL, BF16, seq, ts)]
    return pl.pallas_call(
        functools.partial(_proj_kernel, c_len=c_len),
        grid=(bsz, seq // ts),
        in_specs=[
            tile(D_MODEL),
            _resident((D_MODEL, d_in)),
            _resident((D_MODEL, 2 * D_MODEL)),
            _resident((1, 2 * D_MODEL)),
            pl.BlockSpec((ts, HEAD_DIM), lambda b, t: (t, 0)),
            pl.BlockSpec((ts, HEAD_DIM), lambda b, t: (t, 0)),
        ],
        out_specs=[tile(w, rows) for w, _, _, rows in outs],
        out_shape=[jax.ShapeDtypeStruct((bsz, total, w), dt) for w, dt, total, _ in outs],
        compiler_params=_params(2),
        name="in_proj",
    )(h, w_in, w_merge, b_merge, cos, sin)


N_SETS = 8
N_CHUNKS = N_SETS * SUBLANES
STEP_ROWS = N_SETS * SUBLANES
PITCH_PAD = SUBLANES
EXT_SLOTS = CONV_WIDTH - 1
PHASE_ROWS = 256
SCAN_STEPS = 4
GATHER_STEPS = PHASE_ROWS // STEP_ROWS
TINY = 1e-30


def _chunking(seq):
    c_len = seq // N_CHUNKS
    assert seq % N_CHUNKS == 0 and (c_len * STEP_ROWS) % PHASE_ROWS == 0
    assert c_len % SCAN_STEPS == 0 and c_len % GATHER_STEPS == 0
    assert c_len % (2 * SUBLANES) == 0
    return c_len, c_len + PITCH_PAD


def _sublane_scan(a, b, reverse):
    row = lax.broadcasted_iota(jnp.int32, a.shape, 0)
    for d in (1, 2, 4):
        if reverse:
            keep, shift = row < SUBLANES - d, SUBLANES - d
        else:
            keep, shift = row >= d, d
        a_s = pltpu.roll(a, shift, axis=0)
        b_s = pltpu.roll(b, shift, axis=0)
        b = jnp.where(keep, a * b_s + b, b)
        a = jnp.where(keep, a * a_s, a)
    return a, b


def _rglru_kernel(rx_ref, gr_ref, cw_ref, cb_ref, wg_ref, lam_ref, out_ref,
                  xs, ext, a_scr, b_scr, hs):
    seq = out_ref.shape[0]
    c_len, pitch = _chunking(seq)
    row8 = lax.broadcasted_iota(jnp.int32, (SUBLANES, LANES), 0)
    zero8 = jnp.zeros((SUBLANES, LANES), F32)

    def sets(v):
        return [v[s * SUBLANES:(s + 1) * SUBLANES] for s in range(N_SETS)]

    def slot(i):
        return slice((i + CONV_LEFT) * STEP_ROWS, (i + CONV_LEFT + 1) * STEP_ROWS)

    n_gather = c_len // GATHER_STEPS

    def gather(g):
        i0 = g * GATHER_STEPS
        vals = []
        for kk in range(GATHER_STEPS):
            for s in range(N_SETS):
                vals.append(rx_ref[pl.ds(s * SUBLANES * pitch + i0 + kk, SUBLANES, stride=pitch), :])
        dst = pl.multiple_of((i0 + CONV_LEFT) * STEP_ROWS, STEP_ROWS)
        ext[pl.ds(dst, GATHER_STEPS * STEP_ROWS), :] = jnp.concatenate(vals, axis=0)

    for g in sorted({0, min(1, n_gather - 1), n_gather - 1}):
        gather(g)

    for back in (1, 2):
        tail = [pltpu.roll(v, 1, axis=0) for v in sets(ext[slot(c_len - back), :])]
        ext[slot(-back), :] = jnp.concatenate(
            [jnp.where(row8 == 0, tail[s - 1] if s > 0 else zero8, tail[s])
             for s in range(N_SETS)], axis=0)
    head = [pltpu.roll(v, SUBLANES - 1, axis=0) for v in sets(ext[slot(0), :])]
    ext[slot(c_len), :] = jnp.concatenate(
        [jnp.where(row8 == SUBLANES - 1, head[s + 1] if s < N_SETS - 1 else zero8, head[s])
         for s in range(N_SETS)], axis=0)

    neg = -lam_ref[...]
    softplus = jnp.maximum(neg, 0.0) + jnp.log(1.0 + jnp.exp(-jnp.abs(neg)))
    kexp = softplus * (-RGLRU_C * 0.5 * math.log2(math.e))
    cw = cw_ref[...]
    cb = cb_ref[...]
    ones16 = jnp.ones((PHASE_ROWS, RNN_BLOCK), BF16)

    def order(d):
        return range(SCAN_STEPS - 1, -1, -1) if d else range(SCAN_STEPS)

    def piece(v, kk, s):
        r0 = (kk * N_SETS + s) * SUBLANES
        return v[r0:r0 + SUBLANES]

    def advance(d, av, bv, carry):
        carry = list(carry)
        for kk in order(d):
            for s in range(N_SETS):
                a = piece(av, kk, s)
                h, p = carry[s]
                carry[s] = (a * h + piece(bv, kk, s), a * p)
        return tuple(carry)

    def phase(n, carry):
        gather(jnp.minimum(n + 2, n_gather - 1))
        src = n * PHASE_ROWS
        dst = pl.ds(pl.multiple_of(src, PHASE_ROWS), PHASE_ROWS)
        y = cb
        for tap in range(CONV_WIDTH):
            rows = pl.ds(pl.multiple_of(src + tap * STEP_ROWS, STEP_ROWS), PHASE_ROWS)
            y = y + ext[rows, :] * cw[tap:tap + 1, :]
        lhs = jnp.concatenate([y.astype(BF16), ones16], axis=1)
        for d in range(2):
            g = jnp.dot(lhs, wg_ref[d], preferred_element_type=F32)
            ta = jnp.tanh(g[:, :RNN_BLOCK])
            ti = jnp.tanh(g[:, RNN_BLOCK:])
            k = kexp[d:d + 1, :]
            a = jnp.exp2(k * ta + k)
            z = 1.0 - a * a
            m = z * lax.rsqrt(jnp.maximum(z, TINY))
            b = m * ((ti + 1.0) * y)
            a_scr[d, dst, :] = a
            b_scr[d, dst, :] = b
            if d == 0:
                carry = advance(0, a, b, carry)
        return carry

    ones8 = jnp.ones((SUBLANES, LANES), F32)
    fresh = tuple((zero8, ones8) for _ in range(N_SETS))
    assert PHASE_ROWS == SCAN_STEPS * STEP_ROWS
    ends_f = lax.fori_loop(0, seq // PHASE_ROWS, phase, fresh, unroll=8)

    n_blocks = c_len // SCAN_STEPS
    block_rows = SCAN_STEPS * STEP_ROWS

    def block(d, n):
        blk = (n_blocks - 1 - n) if d else n
        return blk, pl.ds(pl.multiple_of(blk * block_rows, block_rows), block_rows)

    def summarise(d, n, carry):
        _, rows = block(d, n)
        return advance(d, a_scr[d, rows, :], b_scr[d, rows, :], carry)

    def resolve(d, ends):
        edge = zero8
        start = [None] * N_SETS
        for s in (range(N_SETS - 1, -1, -1) if d else range(N_SETS)):
            p_cum, h_cum = _sublane_scan(ends[s][1], ends[s][0], bool(d))
            after = p_cum * edge + h_cum
            if d:
                start[s] = jnp.where(row8 == SUBLANES - 1, edge,
                                     pltpu.roll(after, SUBLANES - 1, axis=0))
                edge = jnp.broadcast_to(after[0:1, :], (SUBLANES, LANES))
            else:
                start[s] = jnp.where(row8 == 0, edge, pltpu.roll(after, 1, axis=0))
                edge = jnp.broadcast_to(after[SUBLANES - 1:SUBLANES, :], (SUBLANES, LANES))
        return tuple(start)

    def rescan(d, n, carry):
        blk, rows = block(d, n)
        av = a_scr[d, rows, :]
        bv = b_scr[d, rows, :]
        hv = hs[rows, :] if d else None
        carry = list(carry)
        out = [None] * (SCAN_STEPS * N_SETS)
        for kk in order(d):
            for s in range(N_SETS):
                h = piece(av, kk, s) * carry[s] + piece(bv, kk, s)
                carry[s] = h
                if d:
                    i = blk * SCAN_STEPS + kk
                    xs[pl.ds(s * SUBLANES * pitch + i, SUBLANES, stride=pitch), :] = (
                        h + piece(hv, kk, s))
                else:
                    out[kk * N_SETS + s] = h
        if not d:
            hs[rows, :] = jnp.concatenate(out, axis=0)
        return tuple(carry)

    _, ends_b = lax.fori_loop(
        0, n_blocks, lambda n, c: (rescan(0, n, c[0]), summarise(1, n, c[1])),
        (resolve(0, ends_f), fresh))
    lax.fori_loop(0, n_blocks, functools.partial(rescan, 1), resolve(1, ends_b))

    for ch in range(N_CHUNKS):
        rows = slice(ch * c_len, (ch + 1) * c_len)
        h = xs[ch * pitch:ch * pitch + c_len, :]
        out_ref[rows, :] = (h * gr_ref[rows, :].astype(F32)).astype(BF16)


def _rglru(rnn_x, gate, conv_w, conv_b, w_gate, lam):
    bsz, seq, _ = gate.shape
    c_len, pitch = _chunking(seq)
    assert rnn_x.shape[1] == N_CHUNKS * pitch

    def col_block(rows):
        return pl.BlockSpec((None, rows, RNN_BLOCK), lambda b, c: (b, 0, c))

    return pl.pallas_call(
        _rglru_kernel,
        grid=(bsz, N_RNN_BLOCKS),
        in_specs=[
            col_block(N_CHUNKS * pitch),
            col_block(seq),
            pl.BlockSpec((CONV_WIDTH, RNN_BLOCK), lambda b, c: (0, c)),
            pl.BlockSpec((1, RNN_BLOCK), lambda b, c: (0, c)),
            pl.BlockSpec((2, None, 2 * RNN_BLOCK, 2 * RNN_BLOCK), lambda b, c: (0, c, 0, 0)),
            pl.BlockSpec((2, RNN_BLOCK), lambda b, c: (0, c)),
        ],
        out_specs=col_block(seq),
        out_shape=jax.ShapeDtypeStruct((bsz, seq, D_RNN), BF16),
        scratch_shapes=[
            pltpu.VMEM((N_CHUNKS * pitch, RNN_BLOCK), F32),
            pltpu.VMEM(((c_len + EXT_SLOTS) * STEP_ROWS, RNN_BLOCK), F32),
            pltpu.VMEM((2, seq, RNN_BLOCK), F32),
            pltpu.VMEM((2, seq, RNN_BLOCK), F32),
            pltpu.VMEM((seq, RNN_BLOCK), F32),
        ],
        compiler_params=_params(2),
        name="rglru",
    )(rnn_x, gate, conv_w, conv_b, w_gate, lam)


MIX_BLOCKS = 4

def _attn_out_kernel(sink_ref, q_ref, k_ref, kp_ref, kn_ref, v_ref, vp_ref, vn_ref, ga_ref,
                     rnn_ref, mg_ref, x_ref, mod_ref, wa_ref, wr_ref, wo_ref, ng_ref, nmod_ref,
                     *rest, final):
    if final:
        o_ref, kext, vext, kbd, vbd, attn_scr = rest
    else:
        o_ref, h_ref, kext, vext, kbd, vbd, attn_scr = rest
    tq = q_ref.shape[0]
    t = pl.program_id(1)
    nt = pl.num_programs(1)
    n_blk = tq // BLOCK

    kext[0:BLOCK, :] = kp_ref[...]
    kext[BLOCK:BLOCK + tq, :] = k_ref[...]
    kext[BLOCK + tq:, :] = kn_ref[...]
    vext[0:BLOCK, :] = vp_ref[...]
    vext[BLOCK:BLOCK + tq, :] = v_ref[...]
    vext[BLOCK + tq:, :] = vn_ref[...]

    band = 3 * BLOCK
    qi = lax.broadcasted_iota(jnp.int32, (BLOCK, band), 0)
    kj = lax.broadcasted_iota(jnp.int32, (BLOCK, band), 1)
    in_window = (kj >= qi) & (kj <= qi + 2 * BLOCK)
    head = lax.broadcasted_iota(jnp.int32, (GROUP, BLOCK, 1), 0)

    for blk in range(n_blk):
        rows = slice(blk * BLOCK, (blk + 1) * BLOCK)
        band_rows = slice(blk * BLOCK, blk * BLOCK + band)
        valid = in_window
        if blk == 0:
            valid = valid & (kj >= jnp.where(t == 0, BLOCK, 0))
        if blk == n_blk - 1:
            valid = valid & (kj < jnp.where(t == nt - 1, 2 * BLOCK, band))
        zeros = jnp.zeros((band, HEAD_DIM), BF16)
        for g in range(N_KV_HEADS):
            for src, dst in ((kext, kbd), (vext, vbd)):
                for gg in range(N_KV_HEADS):
                    cols = slice(gg * HEAD_DIM, (gg + 1) * HEAD_DIM)
                    dst[blk, g * band:(g + 1) * band, cols] = (
                        src[band_rows, cols] if gg == g else zeros)
        qcat = jnp.concatenate(
            [jnp.concatenate(
                [q_ref[rows, (g * GROUP + h) * HEAD_DIM:(g * GROUP + h + 1) * HEAD_DIM]
                 for g in range(N_KV_HEADS)], axis=1) for h in range(GROUP)], axis=0)
        s_all = lax.dot_general(qcat, kbd[blk], (((1,), (1,)), ((), ())),
                                preferred_element_type=F32)
        probs, denoms = [], []
        for g in range(N_KV_HEADS):
            s = s_all[:, g * band:(g + 1) * band].reshape(GROUP, BLOCK, band)
            s = jnp.where(valid[None], s, MASK_VALUE)
            sink = jnp.full((GROUP, BLOCK, 1), sink_ref[g * GROUP] * LOG2E, F32)
            for h in range(1, GROUP):
                sink = jnp.where(head == h, sink_ref[g * GROUP + h] * LOG2E, sink)
            m = jnp.maximum(jnp.max(s, axis=-1, keepdims=True), sink)
            p = jnp.exp2(s - m)
            denoms.append(jnp.sum(p, axis=-1, keepdims=True) + jnp.exp2(sink - m))
            probs.append(p.reshape(GROUP * BLOCK, band).astype(BF16))
        o_all = jnp.dot(jnp.concatenate(probs, axis=1), vbd[blk],
                        preferred_element_type=F32)
        for g in range(N_KV_HEADS):
            o = o_all[:, g * HEAD_DIM:(g + 1) * HEAD_DIM].reshape(GROUP, BLOCK, HEAD_DIM)
            o = o / denoms[g]
            for h in range(GROUP):
                cols = slice((g * GROUP + h) * HEAD_DIM, (g * GROUP + h + 1) * HEAD_DIM)
                attn_scr[rows, cols] = (o[h] * ga_ref[rows, cols].astype(F32)).astype(BF16)

        if (blk + 1) % MIX_BLOCKS == 0 or blk == n_blk - 1:
            lo_row = (blk // MIX_BLOCKS) * MIX_BLOCKS * BLOCK
            rows = slice(lo_row, (blk + 1) * BLOCK)
            a = jnp.dot(attn_scr[rows, :], wa_ref[...], preferred_element_type=F32)
            r = jnp.dot(rnn_ref[rows, :], wr_ref[...], preferred_element_type=F32)
            mixed = (mg_ref[rows, 0:D_MODEL].astype(F32) * a
                     + mg_ref[rows, D_MODEL:].astype(F32) * r)
            y = jnp.dot(mixed.astype(BF16), wo_ref[...], preferred_element_type=F32)
            out = x_ref[rows, :] + mod_ref[:, 2 * D_MODEL:] * y
            if final:
                ms = jnp.mean(out * out, axis=-1, keepdims=True)
                o_ref[rows, :] = out * lax.rsqrt(ms + EPS) * ng_ref[...]
            else:
                o_ref[rows, :] = out
                h_ref[rows, :] = _adaln(out, ng_ref[...], nmod_ref).astype(BF16)


def _attend_and_mix(sink, q, k, v, ga, rnn, mg, x, mod, w_attn, w_rnn, w_out, next_gain,
                    next_mod, batch_off, final):
    bsz, seq, _ = x.shape
    tq = _seq_tile(seq, 512)
    per = tq // BLOCK
    n_blocks = seq // BLOCK

    def tile(width):
        return pl.BlockSpec((None, tq, width), lambda b, t: (b, t, 0))

    prev_kv = pl.BlockSpec((None, BLOCK, KV_DIM), lambda b, t: (b, jnp.maximum(t * per - 1, 0), 0))
    next_kv = pl.BlockSpec((None, BLOCK, KV_DIM),
                           lambda b, t: (b, jnp.minimum((t + 1) * per, n_blocks - 1), 0))
    mod_spec = pl.BlockSpec((None, 1, 3 * D_MODEL), lambda b, t: (b + batch_off, 0, 0))
    out_specs = [tile(D_MODEL)]
    out_shape = [jax.ShapeDtypeStruct((bsz, seq, D_MODEL), F32)]
    if not final:
        out_specs.append(tile(D_MODEL))
        out_shape.append(jax.ShapeDtypeStruct((bsz, seq, D_MODEL), BF16))
    return pl.pallas_call(
        functools.partial(_attn_out_kernel, final=final),
        grid=(bsz, seq // tq),
        in_specs=[
            pl.BlockSpec(memory_space=pltpu.SMEM),
            tile(ATTN_DIM),
            tile(KV_DIM), prev_kv, next_kv,
            tile(KV_DIM), prev_kv, next_kv,
            tile(ATTN_DIM),
            tile(D_RNN),
            tile(2 * D_MODEL),
            tile(D_MODEL),
            mod_spec,
            _resident((ATTN_DIM, D_MODEL)),
            _resident((D_RNN, D_MODEL)),
            _resident((D_MODEL, D_MODEL)),
            _resident((1, D_MODEL)),
            mod_spec,
        ],
        out_specs=out_specs,
        out_shape=out_shape,
        scratch_shapes=[
            pltpu.VMEM((tq + 2 * BLOCK, KV_DIM), BF16),
            pltpu.VMEM((tq + 2 * BLOCK, KV_DIM), BF16),
            pltpu.VMEM((per, N_KV_HEADS * 3 * BLOCK, KV_DIM), BF16),
            pltpu.VMEM((per, N_KV_HEADS * 3 * BLOCK, KV_DIM), BF16),
            pltpu.VMEM((tq, ATTN_DIM), BF16),
        ],
        compiler_params=_params(2),
        name="attn_out",
    )(sink, q, k, k, k, v, v, v, ga, rnn, mg, x, mod, w_attn, w_rnn, w_out, next_gain, next_mod)


def _trunk(x, batch_off, mods, tables, layers, final_gain):
    cos, sin = tables
    seq = x.shape[1]
    cos, sin = cos[:seq], sin[:seq]
    depth = len(layers)
    h = _prenorm(x, mods[0], layers[0]["norm_gain"], batch_off)
    for l, p in enumerate(layers):
        final = l == depth - 1
        q, k, v, ga, rx, gr, mg = _project(h, p["w_in"], p["w_merge"], p["b_merge"], cos, sin)
        rnn = _rglru(rx, gr, p["conv_w"], p["conv_b"], p["w_gate"], p["lam"])
        next_gain = final_gain if final else layers[l + 1]["norm_gain"]
        next_mod = mods[l] if final else mods[l + 1]
        res = _attend_and_mix(p["sink"], q, k, v, ga, rnn, mg, x, mods[l], p["w_attn"],
                              p["w_rnn"], p["w_out"], next_gain, next_mod, batch_off, final)
        x = res[0]
        h = None if final else res[1]
    return x


def kernel(x_prompt, x_sample, c_prompt, c_sample, norm_gain, w_ada, b_ada, w_in, attn_sink, conv_w, conv_b, rg_w_a, rg_b_a, rg_w_x, rg_b_x, rg_lambda, w_attn_proj, w_rnn_proj, w_merge, b_merge, w_out, final_gain):
    depth = w_in.shape[0]
    n_prompt = x_prompt.shape[0]
    cond = jnp.concatenate([c_prompt, c_sample], axis=0)
    mods = _modulation(cond, w_ada, b_ada)
    mods = mods.reshape(depth, cond.shape[0], 1, 3 * D_MODEL)
    tables = _rope_tables(max(x_prompt.shape[1], x_sample.shape[1]))

    w_gate = jnp.concatenate([rg_w_a, rg_w_x], axis=-1).astype(BF16)
    b_half = 0.5 * jnp.concatenate(
        [rg_b_a.reshape(depth, 2, N_RNN_BLOCKS, 1, RNN_BLOCK),
         rg_b_x.reshape(depth, 2, N_RNN_BLOCKS, 1, RNN_BLOCK)], axis=-1)
    b_hi = b_half.astype(BF16)
    b_lo = (b_half - b_hi.astype(F32)).astype(BF16)
    w_gate = jnp.concatenate(
        [w_gate, b_hi, b_lo,
         jnp.zeros((depth, 2, N_RNN_BLOCKS, RNN_BLOCK - 2, 2 * RNN_BLOCK), BF16)], axis=-2)
    layers = []
    for l in range(depth):
        layers.append(dict(
            norm_gain=norm_gain[l].reshape(1, D_MODEL),
            w_in=w_in[l].astype(BF16),
            w_merge=w_merge[l].astype(BF16),
            b_merge=b_merge[l].reshape(1, 2 * D_MODEL),
            sink=attn_sink[l],
            conv_w=0.5 * conv_w[l],
            conv_b=0.5 * conv_b[l].reshape(1, D_RNN),
            w_gate=w_gate[l], lam=rg_lambda[l],
            w_attn=w_attn_proj[l].astype(BF16),
            w_rnn=w_rnn_proj[l].astype(BF16),
            w_out=w_out[l].astype(BF16),
        ))
    fg = final_gain.reshape(1, D_MODEL)
    y_prompt = _trunk(x_prompt, 0, mods, tables, layers, fg)
    y_sample = _trunk(x_sample, n_prompt, mods, tables, layers, fg)
    return (y_prompt, y_sample)
```

```python
import functools
import math

import jax
import jax.numpy as jnp
from jax import lax
from jax.experimental import pallas as pl
from jax.experimental.pallas import tpu as pltpu

D_MODEL = 1024
HEAD_DIM = 128
N_HEADS = 8
N_KV_HEADS = 2
GROUP = N_HEADS // N_KV_HEADS
ATTN_DIM = N_HEADS * HEAD_DIM
KV_DIM = N_KV_HEADS * HEAD_DIM
BLOCK = 128
ROPE_THETA = 10000.0
D_RNN = 3 * D_MODEL // 2
N_RNN_BLOCKS = 12
RNN_BLOCK = D_RNN // N_RNN_BLOCKS
CONV_WIDTH = 4
CONV_LEFT = 2
RGLRU_C = 8.0
EPS = 1e-6
MASK_VALUE = -1e30
LOG2E = math.log2(math.e)

SUBLANES = 8
LANES = 128
MXU_COLS = 256
VMEM_LIMIT_BYTES = 56 * 1024 * 1024

F32 = jnp.float32
BF16 = jnp.bfloat16


def _sigmoid(x):
    return 0.5 * jnp.tanh(0.5 * x) + 0.5


def _silu(x):
    return x * _sigmoid(x)


def _resident(shape):
    zeros = (0,) * len(shape)
    return pl.BlockSpec(shape, lambda *_: zeros, pipeline_mode=pl.Buffered(1))


def _params(n_axes):
    return pltpu.CompilerParams(dimension_semantics=("arbitrary",) * n_axes,
                                vmem_limit_bytes=VMEM_LIMIT_BYTES)


def _mod_kernel(c_ref, w_ref, b_ref, o_ref):
    c = c_ref[...]
    s = c * jax.nn.sigmoid(c)
    o_ref[...] = jnp.dot(s, w_ref[...], precision=lax.Precision.HIGHEST,
                         preferred_element_type=F32) + b_ref[...]


def _modulation(cond, w_ada, b_ada):
    depth = w_ada.shape[0]
    nb = cond.shape[0]
    n_col = 3 * D_MODEL // D_MODEL
    return pl.pallas_call(
        _mod_kernel,
        grid=(depth, n_col),
        in_specs=[
            pl.BlockSpec((nb, D_MODEL), lambda l, j: (0, 0)),
            pl.BlockSpec((None, D_MODEL, D_MODEL), lambda l, j: (l, 0, j)),
            pl.BlockSpec((None, 1, D_MODEL), lambda l, j: (l, 0, j)),
        ],
        out_specs=pl.BlockSpec((None, nb, D_MODEL), lambda l, j: (l, 0, j)),
        out_shape=jax.ShapeDtypeStruct((depth, nb, 3 * D_MODEL), F32),
        compiler_params=_params(2),
        name="adaln_mod",
    )(cond, w_ada, b_ada.reshape(depth, 1, 3 * D_MODEL))


def _rope_kernel(cos_ref, sin_ref):
    rows = cos_ref.shape[0]
    t = lax.broadcasted_iota(jnp.int32, (rows, HEAD_DIM), 0) + pl.program_id(0) * rows
    j = lax.broadcasted_iota(jnp.int32, (rows, HEAD_DIM), 1)
    half = HEAD_DIM // 2
    jj = jnp.where(j < half, j, j - half).astype(F32)
    inv_freq = jnp.exp(jj * (-2.0 / HEAD_DIM * math.log(ROPE_THETA)))
    ang = t.astype(F32) * inv_freq
    s = jnp.sin(ang)
    cos_ref[...] = jnp.cos(ang)
    sin_ref[...] = jnp.where(j < half, -s, s)


def _rope_tables(seq):
    rows = min(seq, 512)
    return pl.pallas_call(
        _rope_kernel,
        grid=(seq // rows,),
        out_specs=[pl.BlockSpec((rows, HEAD_DIM), lambda i: (i, 0))] * 2,
        out_shape=[jax.ShapeDtypeStruct((seq, HEAD_DIM), F32)] * 2,
        compiler_params=_params(1),
        name="rope_tables",
    )()


NORM_ROWS = 32


def _adaln(x, gain, mod_ref):
    ms = jnp.mean(x * x, axis=-1, keepdims=True)
    y = x * lax.rsqrt(ms + EPS) * gain
    return y * (1.0 + mod_ref[:, D_MODEL:2 * D_MODEL]) + mod_ref[:, 0:D_MODEL]


def _prenorm_kernel(x_ref, mod_ref, ng_ref, h_ref):
    gain = ng_ref[...]

    def norm_rows(r, carry):
        rows = pl.ds(pl.multiple_of(r * NORM_ROWS, NORM_ROWS), NORM_ROWS)
        h_ref[rows, :] = _adaln(x_ref[rows, :], gain, mod_ref).astype(BF16)
        return carry

    lax.fori_loop(0, x_ref.shape[0] // NORM_ROWS, norm_rows, 0, unroll=4)


def _prenorm(x, mod, norm_gain, batch_off):
    bsz, seq, _ = x.shape
    ts = _seq_tile(seq, 1024)
    tile = pl.BlockSpec((None, ts, D_MODEL), lambda b, t: (b, t, 0))
    return pl.pallas_call(
        _prenorm_kernel,
        grid=(bsz, seq // ts),
        in_specs=[tile,
                  pl.BlockSpec((None, 1, 3 * D_MODEL), lambda b, t: (b + batch_off, 0, 0)),
                  _resident((1, D_MODEL))],
        out_specs=tile,
        out_shape=jax.ShapeDtypeStruct((bsz, seq, D_MODEL), BF16),
        compiler_params=_params(2),
        name="prenorm",
    )(x, mod, norm_gain)


def _proj_kernel(h_ref, win_ref, wm_ref, bm_ref, cos_ref, sin_ref,
                 q_ref, k_ref, v_ref, ga_ref, rx_ref, gr_ref, mg_ref, *, c_len):
    ts = h_ref.shape[0]
    cos = cos_ref[...]
    sin = sin_ref[...]

    def rotary(p):
        return p * cos + pltpu.roll(p, HEAD_DIM // 2, axis=1) * sin

    def chunk(w_ref, c0):
        return jnp.dot(h_ref[...], w_ref[:, c0:c0 + MXU_COLS], preferred_element_type=F32)

    qk_scale = HEAD_DIM ** -0.5 * LOG2E
    col = 0
    for c in range(ATTN_DIM // MXU_COLS):
        p = chunk(win_ref, col + c * MXU_COLS)
        for hh in range(MXU_COLS // HEAD_DIM):
            dst = slice(c * MXU_COLS + hh * HEAD_DIM, c * MXU_COLS + (hh + 1) * HEAD_DIM)
            q_ref[:, dst] = (rotary(p[:, hh * HEAD_DIM:(hh + 1) * HEAD_DIM]) * qk_scale).astype(BF16)
    col += ATTN_DIM
    p = chunk(win_ref, col)
    for hh in range(N_KV_HEADS):
        dst = slice(hh * HEAD_DIM, (hh + 1) * HEAD_DIM)
        k_ref[:, dst] = rotary(p[:, dst]).astype(BF16)
    col += KV_DIM
    v_ref[...] = chunk(win_ref, col).astype(BF16)
    col += KV_DIM
    for c in range(ATTN_DIM // MXU_COLS):
        dst = slice(c * MXU_COLS, (c + 1) * MXU_COLS)
        ga_ref[:, dst] = _silu(chunk(win_ref, col + c * MXU_COLS)).astype(BF16)
    col += ATTN_DIM
    pitch = rx_ref.shape[0] * c_len // ts
    for c in range(D_RNN // MXU_COLS):
        dst = slice(c * MXU_COLS, (c + 1) * MXU_COLS)
        p = chunk(win_ref, col + c * MXU_COLS)
        for kk in range(ts // c_len):
            rx_ref[kk * pitch:kk * pitch + c_len, dst] = p[kk * c_len:(kk + 1) * c_len]
            rx_ref[kk * pitch + c_len:(kk + 1) * pitch, dst] = jnp.zeros(
                (pitch - c_len, MXU_COLS), F32)
    col += D_RNN
    for c in range(D_RNN // MXU_COLS):
        dst = slice(c * MXU_COLS, (c + 1) * MXU_COLS)
        gr_ref[:, dst] = _silu(chunk(win_ref, col + c * MXU_COLS)).astype(BF16)
    for c in range(2 * D_MODEL // MXU_COLS):
        dst = slice(c * MXU_COLS, (c + 1) * MXU_COLS)
        mg_ref[:, dst] = _sigmoid(chunk(wm_ref, c * MXU_COLS) + bm_ref[:, dst]).astype(BF16)


def _seq_tile(seq, want):
    return want if seq % want == 0 else seq


def _project(h, w_in, w_merge, b_merge, cos, sin):
    bsz, seq, _ = h.shape
    ts = _seq_tile(seq, 512)
    d_in = w_in.shape[1]
    c_len, pitch = _chunking(seq)
    assert ts % c_len == 0
    rx_rows = ts // c_len * pitch

    def tile(width, rows=ts):
        return pl.BlockSpec((None, rows, width), lambda b, t: (b, t, 0))

    outs = [(ATTN_DIM, BF16, seq, ts), (KV_DIM, BF16, seq, ts), (KV_DIM, BF16, seq, ts),
            (ATTN_DIM, BF16, seq, ts), (D_RNN, F32, N_CHUNKS * pitch, rx_rows),
            (D_RNN, BF16, seq, ts), (2 * D_MODEL, BF16, seq, ts)]
    return pl.pallas_call(
        functools.partial(_proj_kernel, c_len=c_len),
        grid=(bsz, seq // ts),
        in_specs=[
            tile(D_MODEL),
            _resident((D_MODEL, d_in)),
            _resident((D_MODEL, 2 * D_MODEL)),
            _resident((1, 2 * D_MODEL)),
            pl.BlockSpec((ts, HEAD_DIM), lambda b, t: (t, 0)),
            pl.BlockSpec((ts, HEAD_DIM), lambda b, t: (t, 0)),
        ],
        out_specs=[tile(w, rows) for w, _, _, rows in outs],
        out_shape=[jax.ShapeDtypeStruct((bsz, total, w), dt) for w, dt, total, _ in outs],
        compiler_params=_params(2),
        name="in_proj",
    )(h, w_in, w_merge, b_merge, cos, sin)


N_SETS = 8
N_CHUNKS = N_SETS * SUBLANES
STEP_ROWS = N_SETS * SUBLANES
PITCH_PAD = SUBLANES
EXT_SLOTS = CONV_WIDTH - 1
PHASE_ROWS = 256
SCAN_STEPS = 4
GATHER_STEPS = PHASE_ROWS // STEP_ROWS
TINY = 1e-30


def _chunking(seq):
    c_len = seq // N_CHUNKS
    assert seq % N_CHUNKS == 0 and (c_len * STEP_ROWS) % PHASE_ROWS == 0
    assert c_len % SCAN_STEPS == 0 and c_len % GATHER_STEPS == 0
    assert c_len % (2 * SUBLANES) == 0
    return c_len, c_len + PITCH_PAD


def _sublane_scan(a, b, reverse):
    row = lax.broadcasted_iota(jnp.int32, a.shape, 0)
    for d in (1, 2, 4):
        if reverse:
            keep, shift = row < SUBLANES - d, SUBLANES - d
        else:
            keep, shift = row >= d, d
        a_s = pltpu.roll(a, shift, axis=0)
        b_s = pltpu.roll(b, shift, axis=0)
        b = jnp.where(keep, a * b_s + b, b)
        a = jnp.where(keep, a * a_s, a)
    return a, b


def _rglru_kernel(rx_ref, gr_ref, cw_ref, cb_ref, wg_ref, lam_ref, out_ref,
                  xs, ext, a_scr, b_scr, hs):
    seq = out_ref.shape[0]
    c_len, pitch = _chunking(seq)
    row8 = lax.broadcasted_iota(jnp.int32, (SUBLANES, LANES), 0)
    zero8 = jnp.zeros((SUBLANES, LANES), F32)

    def sets(v):
        return [v[s * SUBLANES:(s + 1) * SUBLANES] for s in range(N_SETS)]

    def slot(i):
        return slice((i + CONV_LEFT) * STEP_ROWS, (i + CONV_LEFT + 1) * STEP_ROWS)

    n_gather = c_len // GATHER_STEPS

    def gather(g):
        i0 = g * GATHER_STEPS
        vals = []
        for kk in range(GATHER_STEPS):
            for s in range(N_SETS):
                vals.append(rx_ref[pl.ds(s * SUBLANES * pitch + i0 + kk, SUBLANES, stride=pitch), :])
        dst = pl.multiple_of((i0 + CONV_LEFT) * STEP_ROWS, STEP_ROWS)
        ext[pl.ds(dst, GATHER_STEPS * STEP_ROWS), :] = jnp.concatenate(vals, axis=0)

    for g in sorted({0, min(1, n_gather - 1), n_gather - 1}):
        gather(g)

    for back in (1, 2):
        tail = [pltpu.roll(v, 1, axis=0) for v in sets(ext[slot(c_len - back), :])]
        ext[slot(-back), :] = jnp.concatenate(
            [jnp.where(row8 == 0, tail[s - 1] if s > 0 else zero8, tail[s])
             for s in range(N_SETS)], axis=0)
    head = [pltpu.roll(v, SUBLANES - 1, axis=0) for v in sets(ext[slot(0), :])]
    ext[slot(c_len), :] = jnp.concatenate(
        [jnp.where(row8 == SUBLANES - 1, head[s + 1] if s < N_SETS - 1 else zero8, head[s])
         for s in range(N_SETS)], axis=0)

    neg = -lam_ref[...]
    softplus = jnp.maximum(neg, 0.0) + jnp.log(1.0 + jnp.exp(-jnp.abs(neg)))
    kexp = softplus * (-RGLRU_C * 0.5 * math.log2(math.e))
    cw = cw_ref[...]
    cb = cb_ref[...]
    ones16 = jnp.ones((PHASE_ROWS, RNN_BLOCK), BF16)

    def order(d):
        return range(SCAN_STEPS - 1, -1, -1) if d else range(SCAN_STEPS)

    def piece(v, kk, s):
        r0 = (kk * N_SETS + s) * SUBLANES
        return v[r0:r0 + SUBLANES]

    def advance(d, av, bv, carry):
        carry = list(carry)
        for kk in order(d):
            for s in range(N_SETS):
                a = piece(av, kk, s)
                h, p = carry[s]
                carry[s] = (a * h + piece(bv, kk, s), a * p)
        return tuple(carry)

    def phase(n, carry):
        gather(jnp.minimum(n + 2, n_gather - 1))
        src = n * PHASE_ROWS
        dst = pl.ds(pl.multiple_of(src, PHASE_ROWS), PHASE_ROWS)
        y = cb
        for tap in range(CONV_WIDTH):
            rows = pl.ds(pl.multiple_of(src + tap * STEP_ROWS, STEP_ROWS), PHASE_ROWS)
            y = y + ext[rows, :] * cw[tap:tap + 1, :]
        lhs = jnp.concatenate([y.astype(BF16), ones16], axis=1)
        for d in range(2):
            g = jnp.dot(lhs, wg_ref[d], preferred_element_type=F32)
            ta = jnp.tanh(g[:, :RNN_BLOCK])
            ti = jnp.tanh(g[:, RNN_BLOCK:])
            k = kexp[d:d + 1, :]
            a = jnp.exp2(k * ta + k)
            z = 1.0 - a * a
            m = z * lax.rsqrt(jnp.maximum(z, TINY))
            b = m * ((ti + 1.0) * y)
            a_scr[d, dst, :] = a
            b_scr[d, dst, :] = b
            if d == 0:
                carry = advance(0, a, b, carry)
        return carry

    ones8 = jnp.ones((SUBLANES, LANES), F32)
    fresh = tuple((zero8, ones8) for _ in range(N_SETS))
    assert PHASE_ROWS == SCAN_STEPS * STEP_ROWS
    ends_f = lax.fori_loop(0, seq // PHASE_ROWS, phase, fresh, unroll=16)

    n_blocks = c_len // SCAN_STEPS
    block_rows = SCAN_STEPS * STEP_ROWS

    def block(d, n):
        blk = (n_blocks - 1 - n) if d else n
        return blk, pl.ds(pl.multiple_of(blk * block_rows, block_rows), block_rows)

    def summarise(d, n, carry):
        _, rows = block(d, n)
        return advance(d, a_scr[d, rows, :], b_scr[d, rows, :], carry)

    def resolve(d, ends):
        edge = zero8
        start = [None] * N_SETS
        for s in (range(N_SETS - 1, -1, -1) if d else range(N_SETS)):
            p_cum, h_cum = _sublane_scan(ends[s][1], ends[s][0], bool(d))
            after = p_cum * edge + h_cum
            if d:
                start[s] = jnp.where(row8 == SUBLANES - 1, edge,
                                     pltpu.roll(after, SUBLANES - 1, axis=0))
                edge = jnp.broadcast_to(after[0:1, :], (SUBLANES, LANES))
            else:
                start[s] = jnp.where(row8 == 0, edge, pltpu.roll(after, 1, axis=0))
                edge = jnp.broadcast_to(after[SUBLANES - 1:SUBLANES, :], (SUBLANES, LANES))
        return tuple(start)

    def rescan(d, n, carry):
        blk, rows = block(d, n)
        av = a_scr[d, rows, :]
        bv = b_scr[d, rows, :]
        hv = hs[rows, :] if d else None
        carry = list(carry)
        out = [None] * (SCAN_STEPS * N_SETS)
        for kk in order(d):
            for s in range(N_SETS):
                h = piece(av, kk, s) * carry[s] + piece(bv, kk, s)
                carry[s] = h
                if d:
                    i = blk * SCAN_STEPS + kk
                    xs[pl.ds(s * SUBLANES * pitch + i, SUBLANES, stride=pitch), :] = (
                        h + piece(hv, kk, s))
                else:
                    out[kk * N_SETS + s] = h
        if not d:
            hs[rows, :] = jnp.concatenate(out, axis=0)
        return tuple(carry)

    _, ends_b = lax.fori_loop(
        0, n_blocks, lambda n, c: (rescan(0, n, c[0]), summarise(1, n, c[1])),
        (resolve(0, ends_f), fresh))
    lax.fori_loop(0, n_blocks, functools.partial(rescan, 1), resolve(1, ends_b))

    for ch in range(N_CHUNKS):
        rows = slice(ch * c_len, (ch + 1) * c_len)
        h = xs[ch * pitch:ch * pitch + c_len, :]
        out_ref[rows, :] = (h * gr_ref[rows, :].astype(F32)).astype(BF16)


def _rglru(rnn_x, gate, conv_w, conv_b, w_gate, lam):
    bsz, seq, _ = gate.shape
    c_len, pitch = _chunking(seq)
    assert rnn_x.shape[1] == N_CHUNKS * pitch

    def col_block(rows):
        return pl.BlockSpec((None, rows, RNN_BLOCK), lambda b, c: (b, 0, c))

    return pl.pallas_call(
        _rglru_kernel,
        grid=(bsz, N_RNN_BLOCKS),
        in_specs=[
            col_block(N_CHUNKS * pitch),
            col_block(seq),
            pl.BlockSpec((CONV_WIDTH, RNN_BLOCK), lambda b, c: (0, c)),
            pl.BlockSpec((1, RNN_BLOCK), lambda b, c: (0, c)),
            pl.BlockSpec((2, None, 2 * RNN_BLOCK, 2 * RNN_BLOCK), lambda b, c: (0, c, 0, 0)),
            pl.BlockSpec((2, RNN_BLOCK), lambda b, c: (0, c)),
        ],
        out_specs=col_block(seq),
        out_shape=jax.ShapeDtypeStruct((bsz, seq, D_RNN), BF16),
        scratch_shapes=[
            pltpu.VMEM((N_CHUNKS * pitch, RNN_BLOCK), F32),
            pltpu.VMEM(((c_len + EXT_SLOTS) * STEP_ROWS, RNN_BLOCK), F32),
            pltpu.VMEM((2, seq, RNN_BLOCK), F32),
            pltpu.VMEM((2, seq, RNN_BLOCK), F32),
            pltpu.VMEM((seq, RNN_BLOCK), F32),
        ],
        compiler_params=_params(2),
        name="rglru",
    )(rnn_x, gate, conv_w, conv_b, w_gate, lam)


MIX_BLOCKS = 4

def _attn_out_kernel(sink_ref, q_ref, k_ref, kp_ref, kn_ref, v_ref, vp_ref, vn_ref, ga_ref,
                     rnn_ref, mg_ref, x_ref, mod_ref, wa_ref, wr_ref, wo_ref, ng_ref, nmod_ref,
                     *rest, final):
    if final:
        o_ref, kext, vext, kbd, vbd, attn_scr = rest
    else:
        o_ref, h_ref, kext, vext, kbd, vbd, attn_scr = rest
    tq = q_ref.shape[0]
    t = pl.program_id(1)
    nt = pl.num_programs(1)
    n_blk = tq // BLOCK

    kext[0:BLOCK, :] = kp_ref[...]
    kext[BLOCK:BLOCK + tq, :] = k_ref[...]
    kext[BLOCK + tq:, :] = kn_ref[...]
    vext[0:BLOCK, :] = vp_ref[...]
    vext[BLOCK:BLOCK + tq, :] = v_ref[...]
    vext[BLOCK + tq:, :] = vn_ref[...]

    band = 3 * BLOCK
    qi = lax.broadcasted_iota(jnp.int32, (BLOCK, band), 0)
    kj = lax.broadcasted_iota(jnp.int32, (BLOCK, band), 1)
    in_window = (kj >= qi) & (kj <= qi + 2 * BLOCK)
    head = lax.broadcasted_iota(jnp.int32, (GROUP, BLOCK, 1), 0)

    for blk in range(n_blk):
        rows = slice(blk * BLOCK, (blk + 1) * BLOCK)
        band_rows = slice(blk * BLOCK, blk * BLOCK + band)
        valid = in_window
        if blk == 0:
            valid = valid & (kj >= jnp.where(t == 0, BLOCK, 0))
        if blk == n_blk - 1:
            valid = valid & (kj < jnp.where(t == nt - 1, 2 * BLOCK, band))
        zeros = jnp.zeros((band, HEAD_DIM), BF16)
        for g in range(N_KV_HEADS):
            for src, dst in ((kext, kbd), (vext, vbd)):
                for gg in range(N_KV_HEADS):
                    cols = slice(gg * HEAD_DIM, (gg + 1) * HEAD_DIM)
                    dst[blk, g * band:(g + 1) * band, cols] = (
                        src[band_rows, cols] if gg == g else zeros)
        qcat = jnp.concatenate(
            [jnp.concatenate(
                [q_ref[rows, (g * GROUP + h) * HEAD_DIM:(g * GROUP + h + 1) * HEAD_DIM]
                 for g in range(N_KV_HEADS)], axis=1) for h in range(GROUP)], axis=0)
        s_all = lax.dot_general(qcat, kbd[blk], (((1,), (1,)), ((), ())),
                                preferred_element_type=F32)
        probs, denoms = [], []
        for g in range(N_KV_HEADS):
            s = s_all[:, g * band:(g + 1) * band].reshape(GROUP, BLOCK, band)
            s = jnp.where(valid[None], s, MASK_VALUE)
            sink = jnp.full((GROUP, BLOCK, 1), sink_ref[g * GROUP] * LOG2E, F32)
            for h in range(1, GROUP):
                sink = jnp.where(head == h, sink_ref[g * GROUP + h] * LOG2E, sink)
            m = jnp.maximum(jnp.max(s, axis=-1, keepdims=True), sink)
            p = jnp.exp2(s - m)
            denoms.append(jnp.sum(p, axis=-1, keepdims=True) + jnp.exp2(sink - m))
            probs.append(p.reshape(GROUP * BLOCK, band).astype(BF16))
        o_all = jnp.dot(jnp.concatenate(probs, axis=1), vbd[blk],
                        preferred_element_type=F32)
        for g in range(N_KV_HEADS):
            o = o_all[:, g * HEAD_DIM:(g + 1) * HEAD_DIM].reshape(GROUP, BLOCK, HEAD_DIM)
            o = o / denoms[g]
            for h in range(GROUP):
                cols = slice((g * GROUP + h) * HEAD_DIM, (g * GROUP + h + 1) * HEAD_DIM)
                attn_scr[rows, cols] = (o[h] * ga_ref[rows, cols].astype(F32)).astype(BF16)

        if (blk + 1) % MIX_BLOCKS == 0 or blk == n_blk - 1:
            lo_row = (blk // MIX_BLOCKS) * MIX_BLOCKS * BLOCK
            rows = slice(lo_row, (blk + 1) * BLOCK)
            a = jnp.dot(attn_scr[rows, :], wa_ref[...], preferred_element_type=F32)
            r = jnp.dot(rnn_ref[rows, :], wr_ref[...], preferred_element_type=F32)
            mixed = (mg_ref[rows, 0:D_MODEL].astype(F32) * a
                     + mg_ref[rows, D_MODEL:].astype(F32) * r)
            y = jnp.dot(mixed.astype(BF16), wo_ref[...], preferred_element_type=F32)
            out = x_ref[rows, :] + mod_ref[:, 2 * D_MODEL:] * y
            if final:
                ms = jnp.mean(out * out, axis=-1, keepdims=True)
                o_ref[rows, :] = out * lax.rsqrt(ms + EPS) * ng_ref[...]
            else:
                o_ref[rows, :] = out
                h_ref[rows, :] = _adaln(out, ng_ref[...], nmod_ref).astype(BF16)


def _attend_and_mix(sink, q, k, v, ga, rnn, mg, x, mod, w_attn, w_rnn, w_out, next_gain,
                    next_mod, batch_off, final):
    bsz, seq, _ = x.shape
    tq = _seq_tile(seq, 512)
    per = tq // BLOCK
    n_blocks = seq // BLOCK

    def tile(width):
        return pl.BlockSpec((None, tq, width), lambda b, t: (b, t, 0))

    prev_kv = pl.BlockSpec((None, BLOCK, KV_DIM), lambda b, t: (b, jnp.maximum(t * per - 1, 0), 0))
    next_kv = pl.BlockSpec((None, BLOCK, KV_DIM),
                           lambda b, t: (b, jnp.minimum((t + 1) * per, n_blocks - 1), 0))
    mod_spec = pl.BlockSpec((None, 1, 3 * D_MODEL), lambda b, t: (b + batch_off, 0, 0))
    out_specs = [tile(D_MODEL)]
    out_shape = [jax.ShapeDtypeStruct((bsz, seq, D_MODEL), F32)]
    if not final:
        out_specs.append(tile(D_MODEL))
        out_shape.append(jax.ShapeDtypeStruct((bsz, seq, D_MODEL), BF16))
    return pl.pallas_call(
        functools.partial(_attn_out_kernel, final=final),
        grid=(bsz, seq // tq),
        in_specs=[
            pl.BlockSpec(memory_space=pltpu.SMEM),
            tile(ATTN_DIM),
            tile(KV_DIM), prev_kv, next_kv,
            tile(KV_DIM), prev_kv, next_kv,
            tile(ATTN_DIM),
            tile(D_RNN),
            tile(2 * D_MODEL),
            tile(D_MODEL),
            mod_spec,
            _resident((ATTN_DIM, D_MODEL)),
            _resident((D_RNN, D_MODEL)),
            _resident((D_MODEL, D_MODEL)),
            _resident((1, D_MODEL)),
            mod_spec,
        ],
        out_specs=out_specs,
        out_shape=out_shape,
        scratch_shapes=[
            pltpu.VMEM((tq + 2 * BLOCK, KV_DIM), BF16),
            pltpu.VMEM((tq + 2 * BLOCK, KV_DIM), BF16),
            pltpu.VMEM((per, N_KV_HEADS * 3 * BLOCK, KV_DIM), BF16),
            pltpu.VMEM((per, N_KV_HEADS * 3 * BLOCK, KV_DIM), BF16),
            pltpu.VMEM((tq, ATTN_DIM), BF16),
        ],
        compiler_params=_params(2),
        name="attn_out",
    )(sink, q, k, k, k, v, v, v, ga, rnn, mg, x, mod, w_attn, w_rnn, w_out, next_gain, next_mod)


def _trunk(x, batch_off, mods, tables, layers, final_gain):
    cos, sin = tables
    seq = x.shape[1]
    cos, sin = cos[:seq], sin[:seq]
    depth = len(layers)
    h = _prenorm(x, mods[0], layers[0]["norm_gain"], batch_off)
    for l, p in enumerate(layers):
        final = l == depth - 1
        q, k, v, ga, rx, gr, mg = _project(h, p["w_in"], p["w_merge"], p["b_merge"], cos, sin)
        rnn = _rglru(rx, gr, p["conv_w"], p["conv_b"], p["w_gate"], p["lam"])
        next_gain = final_gain if final else layers[l + 1]["norm_gain"]
        next_mod = mods[l] if final else mods[l + 1]
        res = _attend_and_mix(p["sink"], q, k, v, ga, rnn, mg, x, mods[l], p["w_attn"],
                              p["w_rnn"], p["w_out"], next_gain, next_mod, batch_off, final)
        x = res[0]
        h = None if final else res[1]
    return x


def kernel(x_prompt, x_sample, c_prompt, c_sample, norm_gain, w_ada, b_ada, w_in, attn_sink, conv_w, conv_b, rg_w_a, rg_b_a, rg_w_x, rg_b_x, rg_lambda, w_attn_proj, w_rnn_proj, w_merge, b_merge, w_out, final_gain):
    depth = w_in.shape[0]
    n_prompt = x_prompt.shape[0]
    cond = jnp.concatenate([c_prompt, c_sample], axis=0)
    mods = _modulation(cond, w_ada, b_ada)
    mods = mods.reshape(depth, cond.shape[0], 1, 3 * D_MODEL)
    tables = _rope_tables(max(x_prompt.shape[1], x_sample.shape[1]))

    w_gate = jnp.concatenate([rg_w_a, rg_w_x], axis=-1).astype(BF16)
    b_half = 0.5 * jnp.concatenate(
        [rg_b_a.reshape(depth, 2, N_RNN_BLOCKS, 1, RNN_BLOCK),
         rg_b_x.reshape(depth, 2, N_RNN_BLOCKS, 1, RNN_BLOCK)], axis=-1)
    b_hi = b_half.astype(BF16)
    b_lo = (b_half - b_hi.astype(F32)).astype(BF16)
    w_gate = jnp.concatenate(
        [w_gate, b_hi, b_lo,
         jnp.zeros((depth, 2, N_RNN_BLOCKS, RNN_BLOCK - 2, 2 * RNN_BLOCK), BF16)], axis=-2)
    layers = []
    for l in range(depth):
        layers.append(dict(
            norm_gain=norm_gain[l].reshape(1, D_MODEL),
            w_in=w_in[l].astype(BF16),
            w_merge=w_merge[l].astype(BF16),
            b_merge=b_merge[l].reshape(1, 2 * D_MODEL),
            sink=attn_sink[l],
            conv_w=0.5 * conv_w[l],
            conv_b=0.5 * conv_b[l].reshape(1, D_RNN),
            w_gate=w_gate[l], lam=rg_lambda[l],
            w_attn=w_attn_proj[l].astype(BF16),
            w_rnn=w_rnn_proj[l].astype(BF16),
            w_out=w_out[l].astype(BF16),
        ))
    fg = final_gain.reshape(1, D_MODEL)
    y_prompt = _trunk(x_prompt, 0, mods, tables, layers, fg)
    y_sample = _trunk(x_sample, n_prompt, mods, tables, layers, fg)
    return (y_prompt, y_sample)
```

```python
import functools
import math

import jax
import jax.numpy as jnp
from jax import lax
from jax.experimental import pallas as pl
from jax.experimental.pallas import tpu as pltpu

D_MODEL = 1024
HEAD_DIM = 128
N_HEADS = 8
N_KV_HEADS = 2
GROUP = N_HEADS // N_KV_HEADS
ATTN_DIM = N_HEADS * HEAD_DIM
KV_DIM = N_KV_HEADS * HEAD_DIM
BLOCK = 128
ROPE_THETA = 10000.0
D_RNN = 3 * D_MODEL // 2
N_RNN_BLOCKS = 12
RNN_BLOCK = D_RNN // N_RNN_BLOCKS
CONV_WIDTH = 4
CONV_LEFT = 2
RGLRU_C = 8.0
EPS = 1e-6
MASK_VALUE = -1e30
LOG2E = math.log2(math.e)

SUBLANES = 8
LANES = 128
MXU_COLS = 256
VMEM_LIMIT_BYTES = 56 * 1024 * 1024

F32 = jnp.float32
BF16 = jnp.bfloat16


def _sigmoid(x):
    return 0.5 * jnp.tanh(0.5 * x) + 0.5


def _silu(x):
    return x * _sigmoid(x)


def _resident(shape):
    zeros = (0,) * len(shape)
    return pl.BlockSpec(shape, lambda *_: zeros, pipeline_mode=pl.Buffered(1))


def _params(n_axes):
    return pltpu.CompilerParams(dimension_semantics=("arbitrary",) * n_axes,
                                vmem_limit_bytes=VMEM_LIMIT_BYTES)


def _mod_kernel(c_ref, w_ref, b_ref, o_ref):
    c = c_ref[...]
    s = c * jax.nn.sigmoid(c)
    o_ref[...] = jnp.dot(s, w_ref[...], precision=lax.Precision.HIGHEST,
                         preferred_element_type=F32) + b_ref[...]


def _modulation(cond, w_ada, b_ada):
    depth = w_ada.shape[0]
    nb = cond.shape[0]
    n_col = 3 * D_MODEL // D_MODEL
    return pl.pallas_call(
        _mod_kernel,
        grid=(depth, n_col),
        in_specs=[
            pl.BlockSpec((nb, D_MODEL), lambda l, j: (0, 0)),
            pl.BlockSpec((None, D_MODEL, D_MODEL), lambda l, j: (l, 0, j)),
            pl.BlockSpec((None, 1, D_MODEL), lambda l, j: (l, 0, j)),
        ],
        out_specs=pl.BlockSpec((None, nb, D_MODEL), lambda l, j: (l, 0, j)),
        out_shape=jax.ShapeDtypeStruct((depth, nb, 3 * D_MODEL), F32),
        compiler_params=_params(2),
        name="adaln_mod",
    )(cond, w_ada, b_ada.reshape(depth, 1, 3 * D_MODEL))


def _rope_kernel(cos_ref, sin_ref):
    rows = cos_ref.shape[0]
    t = lax.broadcasted_iota(jnp.int32, (rows, HEAD_DIM), 0) + pl.program_id(0) * rows
    j = lax.broadcasted_iota(jnp.int32, (rows, HEAD_DIM), 1)
    half = HEAD_DIM // 2
    jj = jnp.where(j < half, j, j - half).astype(F32)
    inv_freq = jnp.exp(jj * (-2.0 / HEAD_DIM * math.log(ROPE_THETA)))
    ang = t.astype(F32) * inv_freq
    s = jnp.sin(ang)
    cos_ref[...] = jnp.cos(ang)
    sin_ref[...] = jnp.where(j < half, -s, s)


def _rope_tables(seq):
    rows = min(seq, 512)
    return pl.pallas_call(
        _rope_kernel,
        grid=(seq // rows,),
        out_specs=[pl.BlockSpec((rows, HEAD_DIM), lambda i: (i, 0))] * 2,
        out_shape=[jax.ShapeDtypeStruct((seq, HEAD_DIM), F32)] * 2,
        compiler_params=_params(1),
        name="rope_tables",
    )()


NORM_ROWS = 32


def _adaln(x, gain, mod_ref):
    ms = jnp.mean(x * x, axis=-1, keepdims=True)
    y = x * lax.rsqrt(ms + EPS) * gain
    return y * (1.0 + mod_ref[:, D_MODEL:2 * D_MODEL]) + mod_ref[:, 0:D_MODEL]


def _prenorm_kernel(x_ref, mod_ref, ng_ref, h_ref):
    gain = ng_ref[...]

    def norm_rows(r, carry):
        rows = pl.ds(pl.multiple_of(r * NORM_ROWS, NORM_ROWS), NORM_ROWS)
        h_ref[rows, :] = _adaln(x_ref[rows, :], gain, mod_ref).astype(BF16)
        return carry

    lax.fori_loop(0, x_ref.shape[0] // NORM_ROWS, norm_rows, 0, unroll=4)


def _prenorm(x, mod, norm_gain, batch_off):
    bsz, seq, _ = x.shape
    ts = _seq_tile(seq, 1024)
    tile = pl.BlockSpec((None, ts, D_MODEL), lambda b, t: (b, t, 0))
    return pl.pallas_call(
        _prenorm_kernel,
        grid=(bsz, seq // ts),
        in_specs=[tile,
                  pl.BlockSpec((None, 1, 3 * D_MODEL), lambda b, t: (b + batch_off, 0, 0)),
                  _resident((1, D_MODEL))],
        out_specs=tile,
        out_shape=jax.ShapeDtypeStruct((bsz, seq, D_MODEL), BF16),
        compiler_params=_params(2),
        name="prenorm",
    )(x, mod, norm_gain)


def _proj_kernel(h_ref, win_ref, wm_ref, bm_ref, cos_ref, sin_ref,
                 q_ref, k_ref, v_ref, ga_ref, rx_ref, gr_ref, mg_ref, *, c_len):
    ts = h_ref.shape[0]
    cos = cos_ref[...]
    sin = sin_ref[...]

    def rotary(p):
        return p * cos + pltpu.roll(p, HEAD_DIM // 2, axis=1) * sin

    def chunk(w_ref, c0):
        return jnp.dot(h_ref[...], w_ref[:, c0:c0 + MXU_COLS], preferred_element_type=F32)

    qk_scale = HEAD_DIM ** -0.5 * LOG2E
    col = 0
    for c in range(ATTN_DIM // MXU_COLS):
        p = chunk(win_ref, col + c * MXU_COLS)
        for hh in range(MXU_COLS // HEAD_DIM):
            dst = slice(c * MXU_COLS + hh * HEAD_DIM, c * MXU_COLS + (hh + 1) * HEAD_DIM)
            q_ref[:, dst] = (rotary(p[:, hh * HEAD_DIM:(hh + 1) * HEAD_DIM]) * qk_scale).astype(BF16)
    col += ATTN_DIM
    p = chunk(win_ref, col)
    for hh in range(N_KV_HEADS):
        dst = slice(hh * HEAD_DIM, (hh + 1) * HEAD_DIM)
        k_ref[:, dst] = rotary(p[:, dst]).astype(BF16)
    col += KV_DIM
    v_ref[...] = chunk(win_ref, col).astype(BF16)
    col += KV_DIM
    for c in range(ATTN_DIM // MXU_COLS):
        dst = slice(c * MXU_COLS, (c + 1) * MXU_COLS)
        ga_ref[:, dst] = _silu(chunk(win_ref, col + c * MXU_COLS)).astype(BF16)
    col += ATTN_DIM
    pitch = rx_ref.shape[0] * c_len // ts
    for c in range(D_RNN // MXU_COLS):
        dst = slice(c * MXU_COLS, (c + 1) * MXU_COLS)
        p = chunk(win_ref, col + c * MXU_COLS)
        for kk in range(ts // c_len):
            rx_ref[kk * pitch:kk * pitch + c_len, dst] = p[kk * c_len:(kk + 1) * c_len]
            rx_ref[kk * pitch + c_len:(kk + 1) * pitch, dst] = jnp.zeros(
                (pitch - c_len, MXU_COLS), F32)
    col += D_RNN
    for c in range(D_RNN // MXU_COLS):
        dst = slice(c * MXU_COLS, (c + 1) * MXU_COLS)
        gr_ref[:, dst] = _silu(chunk(win_ref, col + c * MXU_COLS)).astype(BF16)
    for c in range(2 * D_MODEL // MXU_COLS):
        dst = slice(c * MXU_COLS, (c + 1) * MXU_COLS)
        mg_ref[:, dst] = _sigmoid(chunk(wm_ref, c * MXU_COLS) + bm_ref[:, dst]).astype(BF16)


def _seq_tile(seq, want):
    return want if seq % want == 0 else seq


def _project(h, w_in, w_merge, b_merge, cos, sin):
    bsz, seq, _ = h.shape
    ts = _seq_tile(seq, 512)
    d_in = w_in.shape[1]
    c_len, pitch = _chunking(seq)
    assert ts % c_len == 0
    rx_rows = ts // c_len * pitch

    def tile(width, rows=ts):
        return pl.BlockSpec((None, rows, width), lambda b, t: (b, t, 0))

    outs = [(ATTN_DIM, BF16, seq, ts), (KV_DIM, BF16, seq, ts), (KV_DIM, BF16, seq, ts),
            (ATTN_DIM, BF16, seq, ts), (D_RNN, F32, N_CHUNKS * pitch, rx_rows),
            (D_RNN, BF16, seq, ts), (2 * D_MODEL, BF16, seq, ts)]
    return pl.pallas_call(
        functools.partial(_proj_kernel, c_len=c_len),
        grid=(bsz, seq // ts),
        in_specs=[
            tile(D_MODEL),
            _resident((D_MODEL, d_in)),
            _resident((D_MODEL, 2 * D_MODEL)),
            _resident((1, 2 * D_MODEL)),
            pl.BlockSpec((ts, HEAD_DIM), lambda b, t: (t, 0)),
            pl.BlockSpec((ts, HEAD_DIM), lambda b, t: (t, 0)),
        ],
        out_specs=[tile(w, rows) for w, _, _, rows in outs],
        out_shape=[jax.ShapeDtypeStruct((bsz, total, w), dt) for w, dt, total, _ in outs],
        compiler_params=_params(2),
        name="in_proj",
    )(h, w_in, w_merge, b_merge, cos, sin)


N_SETS = 8
N_CHUNKS = N_SETS * SUBLANES
STEP_ROWS = N_SETS * SUBLANES
PITCH_PAD = SUBLANES
EXT_SLOTS = CONV_WIDTH - 1
PHASE_ROWS = 256
SCAN_STEPS = 4
GATHER_STEPS = PHASE_ROWS // STEP_ROWS
TINY = 1e-30


def _chunking(seq):
    c_len = seq // N_CHUNKS
    assert seq % N_CHUNKS == 0 and (c_len * STEP_ROWS) % PHASE_ROWS == 0
    assert c_len % SCAN_STEPS == 0 and c_len % GATHER_STEPS == 0
    assert c_len % (2 * SUBLANES) == 0
    return c_len, c_len + PITCH_PAD


def _sublane_scan(a, b, reverse):
    row = lax.broadcasted_iota(jnp.int32, a.shape, 0)
    for d in (1, 2, 4):
        if reverse:
            keep, shift = row < SUBLANES - d, SUBLANES - d
        else:
            keep, shift = row >= d, d
        a_s = pltpu.roll(a, shift, axis=0)
        b_s = pltpu.roll(b, shift, axis=0)
        b = jnp.where(keep, a * b_s + b, b)
        a = jnp.where(keep, a * a_s, a)
    return a, b


def _rglru_kernel(rx_ref, gr_ref, cw_ref, cb_ref, wg_ref, lam_ref, out_ref,
                  xs, ext, a_scr, b_scr, hs):
    seq = out_ref.shape[0]
    c_len, pitch = _chunking(seq)
    row8 = lax.broadcasted_iota(jnp.int32, (SUBLANES, LANES), 0)
    zero8 = jnp.zeros((SUBLANES, LANES), F32)

    def sets(v):
        return [v[s * SUBLANES:(s + 1) * SUBLANES] for s in range(N_SETS)]

    def slot(i):
        return slice((i + CONV_LEFT) * STEP_ROWS, (i + CONV_LEFT + 1) * STEP_ROWS)

    n_gather = c_len // GATHER_STEPS

    def gather(g):
        i0 = g * GATHER_STEPS
        vals = []
        for kk in range(GATHER_STEPS):
            for s in range(N_SETS):
                vals.append(rx_ref[pl.ds(s * SUBLANES * pitch + i0 + kk, SUBLANES, stride=pitch), :])
        dst = pl.multiple_of((i0 + CONV_LEFT) * STEP_ROWS, STEP_ROWS)
        ext[pl.ds(dst, GATHER_STEPS * STEP_ROWS), :] = jnp.concatenate(vals, axis=0)

    for g in sorted({0, min(1, n_gather - 1), n_gather - 1}):
        gather(g)

    for back in (1, 2):
        tail = [pltpu.roll(v, 1, axis=0) for v in sets(ext[slot(c_len - back), :])]
        ext[slot(-back), :] = jnp.concatenate(
            [jnp.where(row8 == 0, tail[s - 1] if s > 0 else zero8, tail[s])
             for s in range(N_SETS)], axis=0)
    head = [pltpu.roll(v, SUBLANES - 1, axis=0) for v in sets(ext[slot(0), :])]
    ext[slot(c_len), :] = jnp.concatenate(
        [jnp.where(row8 == SUBLANES - 1, head[s + 1] if s < N_SETS - 1 else zero8, head[s])
         for s in range(N_SETS)], axis=0)

    neg = -lam_ref[...]
    softplus = jnp.maximum(neg, 0.0) + jnp.log(1.0 + jnp.exp(-jnp.abs(neg)))
    kexp = softplus * (-RGLRU_C * 0.5 * math.log2(math.e))
    cw = cw_ref[...]
    cb = cb_ref[...]
    ones16 = jnp.ones((PHASE_ROWS, RNN_BLOCK), BF16)

    def order(d):
        return range(SCAN_STEPS - 1, -1, -1) if d else range(SCAN_STEPS)

    def piece(v, kk, s):
        r0 = (kk * N_SETS + s) * SUBLANES
        return v[r0:r0 + SUBLANES]

    def advance(d, av, bv, carry):
        carry = list(carry)
        for kk in order(d):
            for s in range(N_SETS):
                a = piece(av, kk, s)
                h, p = carry[s]
                carry[s] = (a * h + piece(bv, kk, s), a * p)
        return tuple(carry)

    def phase(n, carry):
        gather(jnp.minimum(n + 2, n_gather - 1))
        src = n * PHASE_ROWS
        dst = pl.ds(pl.multiple_of(src, PHASE_ROWS), PHASE_ROWS)
        y = cb
        for tap in range(CONV_WIDTH):
            rows = pl.ds(pl.multiple_of(src + tap * STEP_ROWS, STEP_ROWS), PHASE_ROWS)
            y = y + ext[rows, :] * cw[tap:tap + 1, :]
        lhs = jnp.concatenate([y.astype(BF16), ones16], axis=1)
        for d in range(2):
            g = jnp.dot(lhs, wg_ref[d], preferred_element_type=F32)
            ta = jnp.tanh(g[:, :RNN_BLOCK])
            ti = jnp.tanh(g[:, RNN_BLOCK:])
            k = kexp[d:d + 1, :]
            a = jnp.exp2(k * ta + k)
            z = 1.0 - a * a
            m = z * lax.rsqrt(jnp.maximum(z, TINY))
            b = m * ((ti + 1.0) * y)
            a_scr[d, dst, :] = a
            b_scr[d, dst, :] = b
            if d == 0:
                carry = advance(0, a, b, carry)
        return carry

    ones8 = jnp.ones((SUBLANES, LANES), F32)
    fresh = tuple((zero8, ones8) for _ in range(N_SETS))
    assert PHASE_ROWS == SCAN_STEPS * STEP_ROWS
    ends_f = lax.fori_loop(0, seq // PHASE_ROWS, phase, fresh, unroll=32)

    n_blocks = c_len // SCAN_STEPS
    block_rows = SCAN_STEPS * STEP_ROWS

    def block(d, n):
        blk = (n_blocks - 1 - n) if d else n
        return blk, pl.ds(pl.multiple_of(blk * block_rows, block_rows), block_rows)

    def summarise(d, n, carry):
        _, rows = block(d, n)
        return advance(d, a_scr[d, rows, :], b_scr[d, rows, :], carry)

    def resolve(d, ends):
        edge = zero8
        start = [None] * N_SETS
        for s in (range(N_SETS - 1, -1, -1) if d else range(N_SETS)):
            p_cum, h_cum = _sublane_scan(ends[s][1], ends[s][0], bool(d))
            after = p_cum * edge + h_cum
            if d:
                start[s] = jnp.where(row8 == SUBLANES - 1, edge,
                                     pltpu.roll(after, SUBLANES - 1, axis=0))
                edge = jnp.broadcast_to(after[0:1, :], (SUBLANES, LANES))
            else:
                start[s] = jnp.where(row8 == 0, edge, pltpu.roll(after, 1, axis=0))
                edge = jnp.broadcast_to(after[SUBLANES - 1:SUBLANES, :], (SUBLANES, LANES))
        return tuple(start)

    def rescan(d, n, carry):
        blk, rows = block(d, n)
        av = a_scr[d, rows, :]
        bv = b_scr[d, rows, :]
        hv = hs[rows, :] if d else None
        carry = list(carry)
        out = [None] * (SCAN_STEPS * N_SETS)
        for kk in order(d):
            for s in range(N_SETS):
                h = piece(av, kk, s) * carry[s] + piece(bv, kk, s)
                carry[s] = h
                if d:
                    i = blk * SCAN_STEPS + kk
                    xs[pl.ds(s * SUBLANES * pitch + i, SUBLANES, stride=pitch), :] = (
                        h + piece(hv, kk, s))
                else:
                    out[kk * N_SETS + s] = h
        if not d:
            hs[rows, :] = jnp.concatenate(out, axis=0)
        return tuple(carry)

    _, ends_b = lax.fori_loop(
        0, n_blocks, lambda n, c: (rescan(0, n, c[0]), summarise(1, n, c[1])),
        (resolve(0, ends_f), fresh))
    lax.fori_loop(0, n_blocks, functools.partial(rescan, 1), resolve(1, ends_b))

    for ch in range(N_CHUNKS):
        rows = slice(ch * c_len, (ch + 1) * c_len)
        h = xs[ch * pitch:ch * pitch + c_len, :]
        out_ref[rows, :] = (h * gr_ref[rows, :].astype(F32)).astype(BF16)


def _rglru(rnn_x, gate, conv_w, conv_b, w_gate, lam):
    bsz, seq, _ = gate.shape
    c_len, pitch = _chunking(seq)
    assert rnn_x.shape[1] == N_CHUNKS * pitch

    def col_block(rows):
        return pl.BlockSpec((None, rows, RNN_BLOCK), lambda b, c: (b, 0, c))

    return pl.pallas_call(
        _rglru_kernel,
        grid=(bsz, N_RNN_BLOCKS),
        in_specs=[
            col_block(N_CHUNKS * pitch),
            col_block(seq),
            pl.BlockSpec((CONV_WIDTH, RNN_BLOCK), lambda b, c: (0, c)),
            pl.BlockSpec((1, RNN_BLOCK), lambda b, c: (0, c)),
            pl.BlockSpec((2, None, 2 * RNN_BLOCK, 2 * RNN_BLOCK), lambda b, c: (0, c, 0, 0)),
            pl.BlockSpec((2, RNN_BLOCK), lambda b, c: (0, c)),
        ],
        out_specs=col_block(seq),
        out_shape=jax.ShapeDtypeStruct((bsz, seq, D_RNN), BF16),
        scratch_shapes=[
            pltpu.VMEM((N_CHUNKS * pitch, RNN_BLOCK), F32),
            pltpu.VMEM(((c_len + EXT_SLOTS) * STEP_ROWS, RNN_BLOCK), F32),
            pltpu.VMEM((2, seq, RNN_BLOCK), F32),
            pltpu.VMEM((2, seq, RNN_BLOCK), F32),
            pltpu.VMEM((seq, RNN_BLOCK), F32),
        ],
        compiler_params=_params(2),
        name="rglru",
    )(rnn_x, gate, conv_w, conv_b, w_gate, lam)


MIX_BLOCKS = 4

def _attn_out_kernel(sink_ref, q_ref, k_ref, kp_ref, kn_ref, v_ref, vp_ref, vn_ref, ga_ref,
                     rnn_ref, mg_ref, x_ref, mod_ref, wa_ref, wr_ref, wo_ref, ng_ref, nmod_ref,
                     *rest, final):
    if final:
        o_ref, kext, vext, kbd, vbd, attn_scr = rest
    else:
        o_ref, h_ref, kext, vext, kbd, vbd, attn_scr = rest
    tq = q_ref.shape[0]
    t = pl.program_id(1)
    nt = pl.num_programs(1)
    n_blk = tq // BLOCK

    kext[0:BLOCK, :] = kp_ref[...]
    kext[BLOCK:BLOCK + tq, :] = k_ref[...]
    kext[BLOCK + tq:, :] = kn_ref[...]
    vext[0:BLOCK, :] = vp_ref[...]
    vext[BLOCK:BLOCK + tq, :] = v_ref[...]
    vext[BLOCK + tq:, :] = vn_ref[...]

    band = 3 * BLOCK
    qi = lax.broadcasted_iota(jnp.int32, (BLOCK, band), 0)
    kj = lax.broadcasted_iota(jnp.int32, (BLOCK, band), 1)
    in_window = (kj >= qi) & (kj <= qi + 2 * BLOCK)
    head = lax.broadcasted_iota(jnp.int32, (GROUP, BLOCK, 1), 0)

    for blk in range(n_blk):
        rows = slice(blk * BLOCK, (blk + 1) * BLOCK)
        band_rows = slice(blk * BLOCK, blk * BLOCK + band)
        valid = in_window
        if blk == 0:
            valid = valid & (kj >= jnp.where(t == 0, BLOCK, 0))
        if blk == n_blk - 1:
            valid = valid & (kj < jnp.where(t == nt - 1, 2 * BLOCK, band))
        zeros = jnp.zeros((band, HEAD_DIM), BF16)
        for g in range(N_KV_HEADS):
            for src, dst in ((kext, kbd), (vext, vbd)):
                for gg in range(N_KV_HEADS):
                    cols = slice(gg * HEAD_DIM, (gg + 1) * HEAD_DIM)
                    dst[blk, g * band:(g + 1) * band, cols] = (
                        src[band_rows, cols] if gg == g else zeros)
        qcat = jnp.concatenate(
            [jnp.concatenate(
                [q_ref[rows, (g * GROUP + h) * HEAD_DIM:(g * GROUP + h + 1) * HEAD_DIM]
                 for g in range(N_KV_HEADS)], axis=1) for h in range(GROUP)], axis=0)
        s_all = lax.dot_general(qcat, kbd[blk], (((1,), (1,)), ((), ())),
                                preferred_element_type=F32)
        probs, denoms = [], []
        for g in range(N_KV_HEADS):
            s = s_all[:, g * band:(g + 1) * band].reshape(GROUP, BLOCK, band)
            s = jnp.where(valid[None], s, MASK_VALUE)
            sink = jnp.full((GROUP, BLOCK, 1), sink_ref[g * GROUP] * LOG2E, F32)
            for h in range(1, GROUP):
                sink = jnp.where(head == h, sink_ref[g * GROUP + h] * LOG2E, sink)
            m = jnp.maximum(jnp.max(s, axis=-1, keepdims=True), sink)
            p = jnp.exp2(s - m)
            denoms.append(jnp.sum(p, axis=-1, keepdims=True) + jnp.exp2(sink - m))
            probs.append(p.reshape(GROUP * BLOCK, band).astype(BF16))
        o_all = jnp.dot(jnp.concatenate(probs, axis=1), vbd[blk],
                        preferred_element_type=F32)
        for g in range(N_KV_HEADS):
            o = o_all[:, g * HEAD_DIM:(g + 1) * HEAD_DIM].reshape(GROUP, BLOCK, HEAD_DIM)
            o = o / denoms[g]
            for h in range(GROUP):
                cols = slice((g * GROUP + h) * HEAD_DIM, (g * GROUP + h + 1) * HEAD_DIM)
                attn_scr[rows, cols] = (o[h] * ga_ref[rows, cols].astype(F32)).astype(BF16)

        if (blk + 1) % MIX_BLOCKS == 0 or blk == n_blk - 1:
            lo_row = (blk // MIX_BLOCKS) * MIX_BLOCKS * BLOCK
            rows = slice(lo_row, (blk + 1) * BLOCK)
            a = jnp.dot(attn_scr[rows, :], wa_ref[...], preferred_element_type=F32)
            r = jnp.dot(rnn_ref[rows, :], wr_ref[...], preferred_element_type=F32)
            mixed = (mg_ref[rows, 0:D_MODEL].astype(F32) * a
                     + mg_ref[rows, D_MODEL:].astype(F32) * r)
            y = jnp.dot(mixed.astype(BF16), wo_ref[...], preferred_element_type=F32)
            out = x_ref[rows, :] + mod_ref[:, 2 * D_MODEL:] * y
            if final:
                ms = jnp.mean(out * out, axis=-1, keepdims=True)
                o_ref[rows, :] = out * lax.rsqrt(ms + EPS) * ng_ref[...]
            else:
                o_ref[rows, :] = out
                h_ref[rows, :] = _adaln(out, ng_ref[...], nmod_ref).astype(BF16)


def _attend_and_mix(sink, q, k, v, ga, rnn, mg, x, mod, w_attn, w_rnn, w_out, next_gain,
                    next_mod, batch_off, final):
    bsz, seq, _ = x.shape
    tq = _seq_tile(seq, 512)
    per = tq // BLOCK
    n_blocks = seq // BLOCK

    def tile(width):
        return pl.BlockSpec((None, tq, width), lambda b, t: (b, t, 0))

    prev_kv = pl.BlockSpec((None, BLOCK, KV_DIM), lambda b, t: (b, jnp.maximum(t * per - 1, 0), 0))
    next_kv = pl.BlockSpec((None, BLOCK, KV_DIM),
                           lambda b, t: (b, jnp.minimum((t + 1) * per, n_blocks - 1), 0))
    mod_spec = pl.BlockSpec((None, 1, 3 * D_MODEL), lambda b, t: (b + batch_off, 0, 0))
    out_specs = [tile(D_MODEL)]
    out_shape = [jax.ShapeDtypeStruct((bsz, seq, D_MODEL), F32)]
    if not final:
        out_specs.append(tile(D_MODEL))
        out_shape.append(jax.ShapeDtypeStruct((bsz, seq, D_MODEL), BF16))
    return pl.pallas_call(
        functools.partial(_attn_out_kernel, final=final),
        grid=(bsz, seq // tq),
        in_specs=[
            pl.BlockSpec(memory_space=pltpu.SMEM),
            tile(ATTN_DIM),
            tile(KV_DIM), prev_kv, next_kv,
            tile(KV_DIM), prev_kv, next_kv,
            tile(ATTN_DIM),
            tile(D_RNN),
            tile(2 * D_MODEL),
            tile(D_MODEL),
            mod_spec,
            _resident((ATTN_DIM, D_MODEL)),
            _resident((D_RNN, D_MODEL)),
            _resident((D_MODEL, D_MODEL)),
            _resident((1, D_MODEL)),
            mod_spec,
        ],
        out_specs=out_specs,
        out_shape=out_shape,
        scratch_shapes=[
            pltpu.VMEM((tq + 2 * BLOCK, KV_DIM), BF16),
            pltpu.VMEM((tq + 2 * BLOCK, KV_DIM), BF16),
            pltpu.VMEM((per, N_KV_HEADS * 3 * BLOCK, KV_DIM), BF16),
            pltpu.VMEM((per, N_KV_HEADS * 3 * BLOCK, KV_DIM), BF16),
            pltpu.VMEM((tq, ATTN_DIM), BF16),
        ],
        compiler_params=_params(2),
        name="attn_out",
    )(sink, q, k, k, k, v, v, v, ga, rnn, mg, x, mod, w_attn, w_rnn, w_out, next_gain, next_mod)


def _trunk(x, batch_off, mods, tables, layers, final_gain):
    cos, sin = tables
    seq = x.shape[1]
    cos, sin = cos[:seq], sin[:seq]
    depth = len(layers)
    h = _prenorm(x, mods[0], layers[0]["norm_gain"], batch_off)
    for l, p in enumerate(layers):
        final = l == depth - 1
        q, k, v, ga, rx, gr, mg = _project(h, p["w_in"], p["w_merge"], p["b_merge"], cos, sin)
        rnn = _rglru(rx, gr, p["conv_w"], p["conv_b"], p["w_gate"], p["lam"])
        next_gain = final_gain if final else layers[l + 1]["norm_gain"]
        next_mod = mods[l] if final else mods[l + 1]
        res = _attend_and_mix(p["sink"], q, k, v, ga, rnn, mg, x, mods[l], p["w_attn"],
                              p["w_rnn"], p["w_out"], next_gain, next_mod, batch_off, final)
        x = res[0]
        h = None if final else res[1]
    return x


def kernel(x_prompt, x_sample, c_prompt, c_sample, norm_gain, w_ada, b_ada, w_in, attn_sink, conv_w, conv_b, rg_w_a, rg_b_a, rg_w_x, rg_b_x, rg_lambda, w_attn_proj, w_rnn_proj, w_merge, b_merge, w_out, final_gain):
    depth = w_in.shape[0]
    n_prompt = x_prompt.shape[0]
    cond = jnp.concatenate([c_prompt, c_sample], axis=0)
    mods = _modulation(cond, w_ada, b_ada)
    mods = mods.reshape(depth, cond.shape[0], 1, 3 * D_MODEL)
    tables = _rope_tables(max(x_prompt.shape[1], x_sample.shape[1]))

    w_gate = jnp.concatenate([rg_w_a, rg_w_x], axis=-1).astype(BF16)
    b_half = 0.5 * jnp.concatenate(
        [rg_b_a.reshape(depth, 2, N_RNN_BLOCKS, 1, RNN_BLOCK),
         rg_b_x.reshape(depth, 2, N_RNN_BLOCKS, 1, RNN_BLOCK)], axis=-1)
    b_hi = b_half.astype(BF16)
    b_lo = (b_half - b_hi.astype(F32)).astype(BF16)
    w_gate = jnp.concatenate(
        [w_gate, b_hi, b_lo,
         jnp.zeros((depth, 2, N_RNN_BLOCKS, RNN_BLOCK - 2, 2 * RNN_BLOCK), BF16)], axis=-2)
    layers = []
    for l in range(depth):
        layers.append(dict(
            norm_gain=norm_gain[l].reshape(1, D_MODEL),
            w_in=w_in[l].astype(BF16),
            w_merge=w_merge[l].astype(BF16),
            b_merge=b_merge[l].reshape(1, 2 * D_MODEL),
            sink=attn_sink[l],
            conv_w=0.5 * conv_w[l],
            conv_b=0.5 * conv_b[l].reshape(1, D_RNN),
            w_gate=w_gate[l], lam=rg_lambda[l],
            w_attn=w_attn_proj[l].astype(BF16),
            w_rnn=w_rnn_proj[l].astype(BF16),
            w_out=w_out[l].astype(BF16),
        ))
    fg = final_gain.reshape(1, D_MODEL)
    y_prompt = _trunk(x_prompt, 0, mods, tables, layers, fg)
    y_sample = _trunk(x_sample, n_prompt, mods, tables, layers, fg)
    return (y_prompt, y_sample)
```

```python
import functools
import math

import jax
import jax.numpy as jnp
from jax import lax
from jax.experimental import pallas as pl
from jax.experimental.pallas import tpu as pltpu

D_MODEL = 1024
HEAD_DIM = 128
N_HEADS = 8
N_KV_HEADS = 2
GROUP = N_HEADS // N_KV_HEADS
ATTN_DIM = N_HEADS * HEAD_DIM
KV_DIM = N_KV_HEADS * HEAD_DIM
BLOCK = 128
ROPE_THETA = 10000.0
D_RNN = 3 * D_MODEL // 2
N_RNN_BLOCKS = 12
RNN_BLOCK = D_RNN // N_RNN_BLOCKS
CONV_WIDTH = 4
CONV_LEFT = 2
RGLRU_C = 8.0
EPS = 1e-6
MASK_VALUE = -1e30
LOG2E = math.log2(math.e)

SUBLANES = 8
LANES = 128
MXU_COLS = 256
VMEM_LIMIT_BYTES = 56 * 1024 * 1024

F32 = jnp.float32
BF16 = jnp.bfloat16


def _sigmoid(x):
    return 0.5 * jnp.tanh(0.5 * x) + 0.5


def _silu(x):
    return x * _sigmoid(x)


def _resident(shape):
    zeros = (0,) * len(shape)
    return pl.BlockSpec(shape, lambda *_: zeros, pipeline_mode=pl.Buffered(1))


def _params(n_axes):
    return pltpu.CompilerParams(dimension_semantics=("arbitrary",) * n_axes,
                                vmem_limit_bytes=VMEM_LIMIT_BYTES)


def _mod_kernel(c_ref, w_ref, b_ref, o_ref):
    c = c_ref[...]
    s = c * jax.nn.sigmoid(c)
    o_ref[...] = jnp.dot(s, w_ref[...], precision=lax.Precision.HIGHEST,
                         preferred_element_type=F32) + b_ref[...]


def _modulation(cond, w_ada, b_ada):
    depth = w_ada.shape[0]
    nb = cond.shape[0]
    n_col = 3 * D_MODEL // D_MODEL
    return pl.pallas_call(
        _mod_kernel,
        grid=(depth, n_col),
        in_specs=[
            pl.BlockSpec((nb, D_MODEL), lambda l, j: (0, 0)),
            pl.BlockSpec((None, D_MODEL, D_MODEL), lambda l, j: (l, 0, j)),
            pl.BlockSpec((None, 1, D_MODEL), lambda l, j: (l, 0, j)),
        ],
        out_specs=pl.BlockSpec((None, nb, D_MODEL), lambda l, j: (l, 0, j)),
        out_shape=jax.ShapeDtypeStruct((depth, nb, 3 * D_MODEL), F32),
        compiler_params=_params(2),
        name="adaln_mod",
    )(cond, w_ada, b_ada.reshape(depth, 1, 3 * D_MODEL))


def _rope_kernel(cos_ref, sin_ref):
    rows = cos_ref.shape[0]
    t = lax.broadcasted_iota(jnp.int32, (rows, HEAD_DIM), 0) + pl.program_id(0) * rows
    j = lax.broadcasted_iota(jnp.int32, (rows, HEAD_DIM), 1)
    half = HEAD_DIM // 2
    jj = jnp.where(j < half, j, j - half).astype(F32)
    inv_freq = jnp.exp(jj * (-2.0 / HEAD_DIM * math.log(ROPE_THETA)))
    ang = t.astype(F32) * inv_freq
    s = jnp.sin(ang)
    cos_ref[...] = jnp.cos(ang)
    sin_ref[...] = jnp.where(j < half, -s, s)


def _rope_tables(seq):
    rows = min(seq, 512)
    return pl.pallas_call(
        _rope_kernel,
        grid=(seq // rows,),
        out_specs=[pl.BlockSpec((rows, HEAD_DIM), lambda i: (i, 0))] * 2,
        out_shape=[jax.ShapeDtypeStruct((seq, HEAD_DIM), F32)] * 2,
        compiler_params=_params(1),
        name="rope_tables",
    )()


NORM_ROWS = 32


def _adaln(x, gain, mod_ref):
    ms = jnp.mean(x * x, axis=-1, keepdims=True)
    y = x * lax.rsqrt(ms + EPS) * gain
    return y * (1.0 + mod_ref[:, D_MODEL:2 * D_MODEL]) + mod_ref[:, 0:D_MODEL]


def _prenorm_kernel(x_ref, mod_ref, ng_ref, h_ref):
    gain = ng_ref[...]

    def norm_rows(r, carry):
        rows = pl.ds(pl.multiple_of(r * NORM_ROWS, NORM_ROWS), NORM_ROWS)
        h_ref[rows, :] = _adaln(x_ref[rows, :], gain, mod_ref).astype(BF16)
        return carry

    lax.fori_loop(0, x_ref.shape[0] // NORM_ROWS, norm_rows, 0, unroll=4)


def _prenorm(x, mod, norm_gain, batch_off):
    bsz, seq, _ = x.shape
    ts = _seq_tile(seq, 1024)
    tile = pl.BlockSpec((None, ts, D_MODEL), lambda b, t: (b, t, 0))
    return pl.pallas_call(
        _prenorm_kernel,
        grid=(bsz, seq // ts),
        in_specs=[tile,
                  pl.BlockSpec((None, 1, 3 * D_MODEL), lambda b, t: (b + batch_off, 0, 0)),
                  _resident((1, D_MODEL))],
        out_specs=tile,
        out_shape=jax.ShapeDtypeStruct((bsz, seq, D_MODEL), BF16),
        compiler_params=_params(2),
        name="prenorm",
    )(x, mod, norm_gain)


Q_COL = 0
K_COL = Q_COL + ATTN_DIM
V_COL = K_COL + KV_DIM
GA_COL = V_COL + KV_DIM
MG_COL = GA_COL + ATTN_DIM
ATTN_IN_COLS = MG_COL + 2 * D_MODEL


def _attn_in_views(a_ref):
    return (a_ref.at[:, Q_COL:K_COL], a_ref.at[:, K_COL:V_COL], a_ref.at[:, V_COL:GA_COL],
            a_ref.at[:, GA_COL:MG_COL], a_ref.at[:, MG_COL:ATTN_IN_COLS])


def _proj_kernel(h_ref, win_ref, wm_ref, bm_ref, cos_ref, sin_ref,
                 a_ref, rx_ref, gr_ref, *, c_len):
    q_ref, k_ref, v_ref, ga_ref, mg_ref = _attn_in_views(a_ref)
    ts = h_ref.shape[0]
    cos = cos_ref[...]
    sin = sin_ref[...]

    def rotary(p):
        return p * cos + pltpu.roll(p, HEAD_DIM // 2, axis=1) * sin

    def chunk(w_ref, c0):
        return jnp.dot(h_ref[...], w_ref[:, c0:c0 + MXU_COLS], preferred_element_type=F32)

    qk_scale = HEAD_DIM ** -0.5 * LOG2E
    col = 0
    for c in range(ATTN_DIM // MXU_COLS):
        p = chunk(win_ref, col + c * MXU_COLS)
        for hh in range(MXU_COLS // HEAD_DIM):
            dst = slice(c * MXU_COLS + hh * HEAD_DIM, c * MXU_COLS + (hh + 1) * HEAD_DIM)
            q_ref[:, dst] = (rotary(p[:, hh * HEAD_DIM:(hh + 1) * HEAD_DIM]) * qk_scale).astype(BF16)
    col += ATTN_DIM
    p = chunk(win_ref, col)
    for hh in range(N_KV_HEADS):
        dst = slice(hh * HEAD_DIM, (hh + 1) * HEAD_DIM)
        k_ref[:, dst] = rotary(p[:, dst]).astype(BF16)
    col += KV_DIM
    v_ref[...] = chunk(win_ref, col).astype(BF16)
    col += KV_DIM
    for c in range(ATTN_DIM // MXU_COLS):
        dst = slice(c * MXU_COLS, (c + 1) * MXU_COLS)
        ga_ref[:, dst] = _silu(chunk(win_ref, col + c * MXU_COLS)).astype(BF16)
    col += ATTN_DIM
    pitch = rx_ref.shape[0] * c_len // ts
    for c in range(D_RNN // MXU_COLS):
        dst = slice(c * MXU_COLS, (c + 1) * MXU_COLS)
        p = chunk(win_ref, col + c * MXU_COLS)
        for kk in range(ts // c_len):
            rx_ref[kk * pitch:kk * pitch + c_len, dst] = p[kk * c_len:(kk + 1) * c_len]
            rx_ref[kk * pitch + c_len:(kk + 1) * pitch, dst] = jnp.zeros(
                (pitch - c_len, MXU_COLS), F32)
    col += D_RNN
    for c in range(D_RNN // MXU_COLS):
        dst = slice(c * MXU_COLS, (c + 1) * MXU_COLS)
        gr_ref[:, dst] = _silu(chunk(win_ref, col + c * MXU_COLS)).astype(BF16)
    for c in range(2 * D_MODEL // MXU_COLS):
        dst = slice(c * MXU_COLS, (c + 1) * MXU_COLS)
        mg_ref[:, dst] = _sigmoid(chunk(wm_ref, c * MXU_COLS) + bm_ref[:, dst]).astype(BF16)


def _seq_tile(seq, want):
    return want if seq % want == 0 else seq


def _project(h, w_in, w_merge, b_merge, cos, sin):
    bsz, seq, _ = h.shape
    ts = _seq_tile(seq, 512)
    d_in = w_in.shape[1]
    c_len, pitch = _chunking(seq)
    assert ts % c_len == 0
    rx_rows = ts // c_len * pitch

    def tile(width, rows=ts):
        return pl.BlockSpec((None, rows, width), lambda b, t: (b, t, 0))

    outs = [(ATTN_IN_COLS, BF16, seq, ts), (D_RNN, F32, N_CHUNKS * pitch, rx_rows),
            (D_RNN, BF16, seq, ts)]
    return pl.pallas_call(
        functools.partial(_proj_kernel, c_len=c_len),
        grid=(bsz, seq // ts),
        in_specs=[
            tile(D_MODEL),
            _resident((D_MODEL, d_in)),
            _resident((D_MODEL, 2 * D_MODEL)),
            _resident((1, 2 * D_MODEL)),
            pl.BlockSpec((ts, HEAD_DIM), lambda b, t: (t, 0)),
            pl.BlockSpec((ts, HEAD_DIM), lambda b, t: (t, 0)),
        ],
        out_specs=[tile(w, rows) for w, _, _, rows in outs],
        out_shape=[jax.ShapeDtypeStruct((bsz, total, w), dt) for w, dt, total, _ in outs],
        compiler_params=_params(2),
        name="in_proj",
    )(h, w_in, w_merge, b_merge, cos, sin)


N_SETS = 8
N_CHUNKS = N_SETS * SUBLANES
STEP_ROWS = N_SETS * SUBLANES
PITCH_PAD = SUBLANES
EXT_SLOTS = CONV_WIDTH - 1
PHASE_ROWS = 256
SCAN_STEPS = 4
GATHER_STEPS = PHASE_ROWS // STEP_ROWS
TINY = 1e-30


def _chunking(seq):
    c_len = seq // N_CHUNKS
    assert seq % N_CHUNKS == 0 and (c_len * STEP_ROWS) % PHASE_ROWS == 0
    assert c_len % SCAN_STEPS == 0 and c_len % GATHER_STEPS == 0
    assert c_len % (2 * SUBLANES) == 0
    return c_len, c_len + PITCH_PAD


def _sublane_scan(a, b, reverse):
    row = lax.broadcasted_iota(jnp.int32, a.shape, 0)
    for d in (1, 2, 4):
        if reverse:
            keep, shift = row < SUBLANES - d, SUBLANES - d
        else:
            keep, shift = row >= d, d
        a_s = pltpu.roll(a, shift, axis=0)
        b_s = pltpu.roll(b, shift, axis=0)
        b = jnp.where(keep, a * b_s + b, b)
        a = jnp.where(keep, a * a_s, a)
    return a, b


def _rglru_kernel(rx_ref, gr_ref, cw_ref, cb_ref, wg_ref, lam_ref, out_ref,
                  xs, ext, a_scr, b_scr, hs):
    seq = out_ref.shape[0]
    c_len, pitch = _chunking(seq)
    row8 = lax.broadcasted_iota(jnp.int32, (SUBLANES, LANES), 0)
    zero8 = jnp.zeros((SUBLANES, LANES), F32)

    def sets(v):
        return [v[s * SUBLANES:(s + 1) * SUBLANES] for s in range(N_SETS)]

    def slot(i):
        return slice((i + CONV_LEFT) * STEP_ROWS, (i + CONV_LEFT + 1) * STEP_ROWS)

    n_gather = c_len // GATHER_STEPS

    def gather(g):
        i0 = g * GATHER_STEPS
        vals = []
        for kk in range(GATHER_STEPS):
            for s in range(N_SETS):
                vals.append(rx_ref[pl.ds(s * SUBLANES * pitch + i0 + kk, SUBLANES, stride=pitch), :])
        dst = pl.multiple_of((i0 + CONV_LEFT) * STEP_ROWS, STEP_ROWS)
        ext[pl.ds(dst, GATHER_STEPS * STEP_ROWS), :] = jnp.concatenate(vals, axis=0)

    for g in sorted({0, min(1, n_gather - 1), n_gather - 1}):
        gather(g)

    for back in (1, 2):
        tail = [pltpu.roll(v, 1, axis=0) for v in sets(ext[slot(c_len - back), :])]
        ext[slot(-back), :] = jnp.concatenate(
            [jnp.where(row8 == 0, tail[s - 1] if s > 0 else zero8, tail[s])
             for s in range(N_SETS)], axis=0)
    head = [pltpu.roll(v, SUBLANES - 1, axis=0) for v in sets(ext[slot(0), :])]
    ext[slot(c_len), :] = jnp.concatenate(
        [jnp.where(row8 == SUBLANES - 1, head[s + 1] if s < N_SETS - 1 else zero8, head[s])
         for s in range(N_SETS)], axis=0)

    neg = -lam_ref[...]
    softplus = jnp.maximum(neg, 0.0) + jnp.log(1.0 + jnp.exp(-jnp.abs(neg)))
    kexp = softplus * (-RGLRU_C * 0.5 * math.log2(math.e))
    cw = cw_ref[...]
    cb = cb_ref[...]
    ones16 = jnp.ones((PHASE_ROWS, RNN_BLOCK), BF16)

    def order(d):
        return range(SCAN_STEPS - 1, -1, -1) if d else range(SCAN_STEPS)

    def piece(v, kk, s):
        r0 = (kk * N_SETS + s) * SUBLANES
        return v[r0:r0 + SUBLANES]

    def advance(d, av, bv, carry):
        carry = list(carry)
        for kk in order(d):
            for s in range(N_SETS):
                a = piece(av, kk, s)
                h, p = carry[s]
                carry[s] = (a * h + piece(bv, kk, s), a * p)
        return tuple(carry)

    def phase(n, carry):
        gather(jnp.minimum(n + 2, n_gather - 1))
        src = n * PHASE_ROWS
        dst = pl.ds(pl.multiple_of(src, PHASE_ROWS), PHASE_ROWS)
        y = cb
        for tap in range(CONV_WIDTH):
            rows = pl.ds(pl.multiple_of(src + tap * STEP_ROWS, STEP_ROWS), PHASE_ROWS)
            y = y + ext[rows, :] * cw[tap:tap + 1, :]
        lhs = jnp.concatenate([y.astype(BF16), ones16], axis=1)
        for d in range(2):
            g = jnp.dot(lhs, wg_ref[d], preferred_element_type=F32)
            ta = jnp.tanh(g[:, :RNN_BLOCK])
            ti = jnp.tanh(g[:, RNN_BLOCK:])
            k = kexp[d:d + 1, :]
            a = jnp.exp2(k * ta + k)
            z = 1.0 - a * a
            m = z * lax.rsqrt(jnp.maximum(z, TINY))
            b = m * ((ti + 1.0) * y)
            a_scr[d, dst, :] = a
            b_scr[d, dst, :] = b
            if d == 0:
                carry = advance(0, a, b, carry)
        return carry

    ones8 = jnp.ones((SUBLANES, LANES), F32)
    fresh = tuple((zero8, ones8) for _ in range(N_SETS))
    assert PHASE_ROWS == SCAN_STEPS * STEP_ROWS
    ends_f = lax.fori_loop(0, seq // PHASE_ROWS, phase, fresh, unroll=32)

    n_blocks = c_len // SCAN_STEPS
    block_rows = SCAN_STEPS * STEP_ROWS

    def block(d, n):
        blk = (n_blocks - 1 - n) if d else n
        return blk, pl.ds(pl.multiple_of(blk * block_rows, block_rows), block_rows)

    def summarise(d, n, carry):
        _, rows = block(d, n)
        return advance(d, a_scr[d, rows, :], b_scr[d, rows, :], carry)

    def resolve(d, ends):
        edge = zero8
        start = [None] * N_SETS
        for s in (range(N_SETS - 1, -1, -1) if d else range(N_SETS)):
            p_cum, h_cum = _sublane_scan(ends[s][1], ends[s][0], bool(d))
            after = p_cum * edge + h_cum
            if d:
                start[s] = jnp.where(row8 == SUBLANES - 1, edge,
                                     pltpu.roll(after, SUBLANES - 1, axis=0))
                edge = jnp.broadcast_to(after[0:1, :], (SUBLANES, LANES))
            else:
                start[s] = jnp.where(row8 == 0, edge, pltpu.roll(after, 1, axis=0))
                edge = jnp.broadcast_to(after[SUBLANES - 1:SUBLANES, :], (SUBLANES, LANES))
        return tuple(start)

    def rescan(d, n, carry):
        blk, rows = block(d, n)
        av = a_scr[d, rows, :]
        bv = b_scr[d, rows, :]
        hv = hs[rows, :] if d else None
        carry = list(carry)
        out = [None] * (SCAN_STEPS * N_SETS)
        for kk in order(d):
            for s in range(N_SETS):
                h = piece(av, kk, s) * carry[s] + piece(bv, kk, s)
                carry[s] = h
                if d:
                    i = blk * SCAN_STEPS + kk
                    xs[pl.ds(s * SUBLANES * pitch + i, SUBLANES, stride=pitch), :] = (
                        h + piece(hv, kk, s))
                else:
                    out[kk * N_SETS + s] = h
        if not d:
            hs[rows, :] = jnp.concatenate(out, axis=0)
        return tuple(carry)

    _, ends_b = lax.fori_loop(
        0, n_blocks, lambda n, c: (rescan(0, n, c[0]), summarise(1, n, c[1])),
        (resolve(0, ends_f), fresh))
    lax.fori_loop(0, n_blocks, functools.partial(rescan, 1), resolve(1, ends_b))

    for ch in range(N_CHUNKS):
        rows = slice(ch * c_len, (ch + 1) * c_len)
        h = xs[ch * pitch:ch * pitch + c_len, :]
        out_ref[rows, :] = (h * gr_ref[rows, :].astype(F32)).astype(BF16)


def _rglru(rnn_x, gate, conv_w, conv_b, w_gate, lam):
    bsz, seq, _ = gate.shape
    c_len, pitch = _chunking(seq)
    assert rnn_x.shape[1] == N_CHUNKS * pitch

    def col_block(rows):
        return pl.BlockSpec((None, rows, RNN_BLOCK), lambda b, c: (b, 0, c))

    return pl.pallas_call(
        _rglru_kernel,
        grid=(bsz, N_RNN_BLOCKS),
        in_specs=[
            col_block(N_CHUNKS * pitch),
            col_block(seq),
            pl.BlockSpec((CONV_WIDTH, RNN_BLOCK), lambda b, c: (0, c)),
            pl.BlockSpec((1, RNN_BLOCK), lambda b, c: (0, c)),
            pl.BlockSpec((2, None, 2 * RNN_BLOCK, 2 * RNN_BLOCK), lambda b, c: (0, c, 0, 0)),
            pl.BlockSpec((2, RNN_BLOCK), lambda b, c: (0, c)),
        ],
        out_specs=col_block(seq),
        out_shape=jax.ShapeDtypeStruct((bsz, seq, D_RNN), BF16),
        scratch_shapes=[
            pltpu.VMEM((N_CHUNKS * pitch, RNN_BLOCK), F32),
            pltpu.VMEM(((c_len + EXT_SLOTS) * STEP_ROWS, RNN_BLOCK), F32),
            pltpu.VMEM((2, seq, RNN_BLOCK), F32),
            pltpu.VMEM((2, seq, RNN_BLOCK), F32),
            pltpu.VMEM((seq, RNN_BLOCK), F32),
        ],
        compiler_params=_params(2),
        name="rglru",
    )(rnn_x, gate, conv_w, conv_b, w_gate, lam)


MIX_BLOCKS = 4

def _attn_out_kernel(sink_ref, a_ref, kvp_ref, kvn_ref, rnn_ref, x_ref, mod_ref,
                     wa_ref, wr_ref, wo_ref, ng_ref, nmod_ref, *rest, final):
    q_ref, k_ref, v_ref, ga_ref, mg_ref = _attn_in_views(a_ref)
    kp_ref, vp_ref = kvp_ref.at[:, 0:KV_DIM], kvp_ref.at[:, KV_DIM:2 * KV_DIM]
    kn_ref, vn_ref = kvn_ref.at[:, 0:KV_DIM], kvn_ref.at[:, KV_DIM:2 * KV_DIM]
    if final:
        o_ref, kext, vext, kbd, vbd, attn_scr = rest
    else:
        o_ref, h_ref, kext, vext, kbd, vbd, attn_scr = rest
    tq = q_ref.shape[0]
    t = pl.program_id(1)
    nt = pl.num_programs(1)
    n_blk = tq // BLOCK

    kext[0:BLOCK, :] = kp_ref[...]
    kext[BLOCK:BLOCK + tq, :] = k_ref[...]
    kext[BLOCK + tq:, :] = kn_ref[...]
    vext[0:BLOCK, :] = vp_ref[...]
    vext[BLOCK:BLOCK + tq, :] = v_ref[...]
    vext[BLOCK + tq:, :] = vn_ref[...]

    band = 3 * BLOCK
    qi = lax.broadcasted_iota(jnp.int32, (BLOCK, band), 0)
    kj = lax.broadcasted_iota(jnp.int32, (BLOCK, band), 1)
    in_window = (kj >= qi) & (kj <= qi + 2 * BLOCK)
    head = lax.broadcasted_iota(jnp.int32, (GROUP, BLOCK, 1), 0)

    for blk in range(n_blk):
        rows = slice(blk * BLOCK, (blk + 1) * BLOCK)
        band_rows = slice(blk * BLOCK, blk * BLOCK + band)
        valid = in_window
        if blk == 0:
            valid = valid & (kj >= jnp.where(t == 0, BLOCK, 0))
        if blk == n_blk - 1:
            valid = valid & (kj < jnp.where(t == nt - 1, 2 * BLOCK, band))
        zeros = jnp.zeros((band, HEAD_DIM), BF16)
        for g in range(N_KV_HEADS):
            for src, dst in ((kext, kbd), (vext, vbd)):
                for gg in range(N_KV_HEADS):
                    cols = slice(gg * HEAD_DIM, (gg + 1) * HEAD_DIM)
                    dst[blk, g * band:(g + 1) * band, cols] = (
                        src[band_rows, cols] if gg == g else zeros)
        qcat = jnp.concatenate(
            [jnp.concatenate(
                [q_ref[rows, (g * GROUP + h) * HEAD_DIM:(g * GROUP + h + 1) * HEAD_DIM]
                 for g in range(N_KV_HEADS)], axis=1) for h in range(GROUP)], axis=0)
        s_all = lax.dot_general(qcat, kbd[blk], (((1,), (1,)), ((), ())),
                                preferred_element_type=F32)
        probs, denoms = [], []
        for g in range(N_KV_HEADS):
            s = s_all[:, g * band:(g + 1) * band].reshape(GROUP, BLOCK, band)
            s = jnp.where(valid[None], s, MASK_VALUE)
            sink = jnp.full((GROUP, BLOCK, 1), sink_ref[g * GROUP] * LOG2E, F32)
            for h in range(1, GROUP):
                sink = jnp.where(head == h, sink_ref[g * GROUP + h] * LOG2E, sink)
            m = jnp.maximum(jnp.max(s, axis=-1, keepdims=True), sink)
            p = jnp.exp2(s - m)
            denoms.append(jnp.sum(p, axis=-1, keepdims=True) + jnp.exp2(sink - m))
            probs.append(p.reshape(GROUP * BLOCK, band).astype(BF16))
        o_all = jnp.dot(jnp.concatenate(probs, axis=1), vbd[blk],
                        preferred_element_type=F32)
        for g in range(N_KV_HEADS):
            o = o_all[:, g * HEAD_DIM:(g + 1) * HEAD_DIM].reshape(GROUP, BLOCK, HEAD_DIM)
            o = o / denoms[g]
            for h in range(GROUP):
                cols = slice((g * GROUP + h) * HEAD_DIM, (g * GROUP + h + 1) * HEAD_DIM)
                attn_scr[rows, cols] = (o[h] * ga_ref[rows, cols].astype(F32)).astype(BF16)

        if (blk + 1) % MIX_BLOCKS == 0 or blk == n_blk - 1:
            lo_row = (blk // MIX_BLOCKS) * MIX_BLOCKS * BLOCK
            rows = slice(lo_row, (blk + 1) * BLOCK)
            a = jnp.dot(attn_scr[rows, :], wa_ref[...], preferred_element_type=F32)
            r = jnp.dot(rnn_ref[rows, :], wr_ref[...], preferred_element_type=F32)
            mixed = (mg_ref[rows, 0:D_MODEL].astype(F32) * a
                     + mg_ref[rows, D_MODEL:].astype(F32) * r)
            y = jnp.dot(mixed.astype(BF16), wo_ref[...], preferred_element_type=F32)
            out = x_ref[rows, :] + mod_ref[:, 2 * D_MODEL:] * y
            if final:
                ms = jnp.mean(out * out, axis=-1, keepdims=True)
                o_ref[rows, :] = out * lax.rsqrt(ms + EPS) * ng_ref[...]
            else:
                o_ref[rows, :] = out
                h_ref[rows, :] = _adaln(out, ng_ref[...], nmod_ref).astype(BF16)


def _attend_and_mix(sink, attn_in, rnn, x, mod, w_attn, w_rnn, w_out, next_gain,
                    next_mod, batch_off, final):
    bsz, seq, _ = x.shape
    tq = _seq_tile(seq, 512)
    per = tq // BLOCK
    n_blocks = seq // BLOCK

    def tile(width):
        return pl.BlockSpec((None, tq, width), lambda b, t: (b, t, 0))

    kv_blk = K_COL // (2 * KV_DIM)
    assert K_COL % (2 * KV_DIM) == 0 and V_COL == K_COL + KV_DIM
    prev_kv = pl.BlockSpec((None, BLOCK, 2 * KV_DIM),
                           lambda b, t: (b, jnp.maximum(t * per - 1, 0), kv_blk))
    next_kv = pl.BlockSpec((None, BLOCK, 2 * KV_DIM),
                           lambda b, t: (b, jnp.minimum((t + 1) * per, n_blocks - 1), kv_blk))
    mod_spec = pl.BlockSpec((None, 1, 3 * D_MODEL), lambda b, t: (b + batch_off, 0, 0))
    out_specs = [tile(D_MODEL)]
    out_shape = [jax.ShapeDtypeStruct((bsz, seq, D_MODEL), F32)]
    if not final:
        out_specs.append(tile(D_MODEL))
        out_shape.append(jax.ShapeDtypeStruct((bsz, seq, D_MODEL), BF16))
    return pl.pallas_call(
        functools.partial(_attn_out_kernel, final=final),
        grid=(bsz, seq // tq),
        in_specs=[
            pl.BlockSpec(memory_space=pltpu.SMEM),
            tile(ATTN_IN_COLS), prev_kv, next_kv,
            tile(D_RNN),
            tile(D_MODEL),
            mod_spec,
            _resident((ATTN_DIM, D_MODEL)),
            _resident((D_RNN, D_MODEL)),
            _resident((D_MODEL, D_MODEL)),
            _resident((1, D_MODEL)),
            mod_spec,
        ],
        out_specs=out_specs,
        out_shape=out_shape,
        scratch_shapes=[
            pltpu.VMEM((tq + 2 * BLOCK, KV_DIM), BF16),
            pltpu.VMEM((tq + 2 * BLOCK, KV_DIM), BF16),
            pltpu.VMEM((per, N_KV_HEADS * 3 * BLOCK, KV_DIM), BF16),
            pltpu.VMEM((per, N_KV_HEADS * 3 * BLOCK, KV_DIM), BF16),
            pltpu.VMEM((tq, ATTN_DIM), BF16),
        ],
        compiler_params=_params(2),
        name="attn_out",
    )(sink, attn_in, attn_in, attn_in, rnn, x, mod, w_attn, w_rnn, w_out, next_gain, next_mod)


def _trunk(x, batch_off, mods, tables, layers, final_gain):
    cos, sin = tables
    seq = x.shape[1]
    cos, sin = cos[:seq], sin[:seq]
    depth = len(layers)
    h = _prenorm(x, mods[0], layers[0]["norm_gain"], batch_off)
    for l, p in enumerate(layers):
        final = l == depth - 1
        attn_in, rx, gr = _project(h, p["w_in"], p["w_merge"], p["b_merge"], cos, sin)
        rnn = _rglru(rx, gr, p["conv_w"], p["conv_b"], p["w_gate"], p["lam"])
        next_gain = final_gain if final else layers[l + 1]["norm_gain"]
        next_mod = mods[l] if final else mods[l + 1]
        res = _attend_and_mix(p["sink"], attn_in, rnn, x, mods[l], p["w_attn"],
                              p["w_rnn"], p["w_out"], next_gain, next_mod, batch_off, final)
        x = res[0]
        h = None if final else res[1]
    return x


def kernel(x_prompt, x_sample, c_prompt, c_sample, norm_gain, w_ada, b_ada, w_in, attn_sink, conv_w, conv_b, rg_w_a, rg_b_a, rg_w_x, rg_b_x, rg_lambda, w_attn_proj, w_rnn_proj, w_merge, b_merge, w_out, final_gain):
    depth = w_in.shape[0]
    n_prompt = x_prompt.shape[0]
    cond = jnp.concatenate([c_prompt, c_sample], axis=0)
    mods = _modulation(cond, w_ada, b_ada)
    mods = mods.reshape(depth, cond.shape[0], 1, 3 * D_MODEL)
    tables = _rope_tables(max(x_prompt.shape[1], x_sample.shape[1]))

    w_gate = jnp.concatenate([rg_w_a, rg_w_x], axis=-1).astype(BF16)
    b_half = 0.5 * jnp.concatenate(
        [rg_b_a.reshape(depth, 2, N_RNN_BLOCKS, 1, RNN_BLOCK),
         rg_b_x.reshape(depth, 2, N_RNN_BLOCKS, 1, RNN_BLOCK)], axis=-1)
    b_hi = b_half.astype(BF16)
    b_lo = (b_half - b_hi.astype(F32)).astype(BF16)
    w_gate = jnp.concatenate(
        [w_gate, b_hi, b_lo,
         jnp.zeros((depth, 2, N_RNN_BLOCKS, RNN_BLOCK - 2, 2 * RNN_BLOCK), BF16)], axis=-2)
    layers = []
    for l in range(depth):
        layers.append(dict(
            norm_gain=norm_gain[l].reshape(1, D_MODEL),
            w_in=w_in[l].astype(BF16),
            w_merge=w_merge[l].astype(BF16),
            b_merge=b_merge[l].reshape(1, 2 * D_MODEL),
            sink=attn_sink[l],
            conv_w=0.5 * conv_w[l],
            conv_b=0.5 * conv_b[l].reshape(1, D_RNN),
            w_gate=w_gate[l], lam=rg_lambda[l],
            w_attn=w_attn_proj[l].astype(BF16),
            w_rnn=w_rnn_proj[l].astype(BF16),
            w_out=w_out[l].astype(BF16),
        ))
    fg = final_gain.reshape(1, D_MODEL)
    y_prompt = _trunk(x_prompt, 0, mods, tables, layers, fg)
    y_sample = _trunk(x_sample, n_prompt, mods, tables, layers, fg)
    return (y_prompt, y_sample)
```

```python
import functools
import math

import jax
import jax.numpy as jnp
from jax import lax
from jax.experimental import pallas as pl
from jax.experimental.pallas import tpu as pltpu

D_MODEL = 1024
HEAD_DIM = 128
N_HEADS = 8
N_KV_HEADS = 2
GROUP = N_HEADS // N_KV_HEADS
ATTN_DIM = N_HEADS * HEAD_DIM
KV_DIM = N_KV_HEADS * HEAD_DIM
BLOCK = 128
ROPE_THETA = 10000.0
D_RNN = 3 * D_MODEL // 2
N_RNN_BLOCKS = 12
RNN_BLOCK = D_RNN // N_RNN_BLOCKS
CONV_WIDTH = 4
CONV_LEFT = 2
RGLRU_C = 8.0
EPS = 1e-6
MASK_VALUE = -1e30
LOG2E = math.log2(math.e)

SUBLANES = 8
LANES = 128
MXU_COLS = 256
VMEM_LIMIT_BYTES = 56 * 1024 * 1024

F32 = jnp.float32
BF16 = jnp.bfloat16


def _sigmoid(x):
    return 0.5 * jnp.tanh(0.5 * x) + 0.5


def _silu(x):
    return x * _sigmoid(x)


def _resident(shape):
    zeros = (0,) * len(shape)
    return pl.BlockSpec(shape, lambda *_: zeros, pipeline_mode=pl.Buffered(1))


def _params(n_axes):
    return pltpu.CompilerParams(dimension_semantics=("arbitrary",) * n_axes,
                                vmem_limit_bytes=VMEM_LIMIT_BYTES)


def _mod_kernel(c_ref, w_ref, b_ref, o_ref):
    c = c_ref[...]
    s = c * jax.nn.sigmoid(c)
    o_ref[...] = jnp.dot(s, w_ref[...], precision=lax.Precision.HIGHEST,
                         preferred_element_type=F32) + b_ref[...]


def _modulation(cond, w_ada, b_ada):
    depth = w_ada.shape[0]
    nb = cond.shape[0]
    n_col = 3 * D_MODEL // D_MODEL
    return pl.pallas_call(
        _mod_kernel,
        grid=(depth, n_col),
        in_specs=[
            pl.BlockSpec((nb, D_MODEL), lambda l, j: (0, 0)),
            pl.BlockSpec((None, D_MODEL, D_MODEL), lambda l, j: (l, 0, j)),
            pl.BlockSpec((None, 1, D_MODEL), lambda l, j: (l, 0, j)),
        ],
        out_specs=pl.BlockSpec((None, nb, D_MODEL), lambda l, j: (l, 0, j)),
        out_shape=jax.ShapeDtypeStruct((depth, nb, 3 * D_MODEL), F32),
        compiler_params=_params(2),
        name="adaln_mod",
    )(cond, w_ada, b_ada.reshape(depth, 1, 3 * D_MODEL))


def _rope_kernel(cos_ref, sin_ref):
    rows = cos_ref.shape[0]
    t = lax.broadcasted_iota(jnp.int32, (rows, HEAD_DIM), 0) + pl.program_id(0) * rows
    j = lax.broadcasted_iota(jnp.int32, (rows, HEAD_DIM), 1)
    half = HEAD_DIM // 2
    jj = jnp.where(j < half, j, j - half).astype(F32)
    inv_freq = jnp.exp(jj * (-2.0 / HEAD_DIM * math.log(ROPE_THETA)))
    ang = t.astype(F32) * inv_freq
    s = jnp.sin(ang)
    cos_ref[...] = jnp.cos(ang)
    sin_ref[...] = jnp.where(j < half, -s, s)


def _rope_tables(seq):
    rows = min(seq, 512)
    return pl.pallas_call(
        _rope_kernel,
        grid=(seq // rows,),
        out_specs=[pl.BlockSpec((rows, HEAD_DIM), lambda i: (i, 0))] * 2,
        out_shape=[jax.ShapeDtypeStruct((seq, HEAD_DIM), F32)] * 2,
        compiler_params=_params(1),
        name="rope_tables",
    )()


NORM_ROWS = 32


def _adaln(x, gain, mod_ref):
    ms = jnp.mean(x * x, axis=-1, keepdims=True)
    y = x * lax.rsqrt(ms + EPS) * gain
    return y * (1.0 + mod_ref[:, D_MODEL:2 * D_MODEL]) + mod_ref[:, 0:D_MODEL]


def _prenorm_kernel(x_ref, mod_ref, ng_ref, h_ref):
    gain = ng_ref[...]

    def norm_rows(r, carry):
        rows = pl.ds(pl.multiple_of(r * NORM_ROWS, NORM_ROWS), NORM_ROWS)
        h_ref[rows, :] = _adaln(x_ref[rows, :], gain, mod_ref).astype(BF16)
        return carry

    lax.fori_loop(0, x_ref.shape[0] // NORM_ROWS, norm_rows, 0, unroll=4)


def _prenorm(x, mod, norm_gain, batch_off):
    bsz, seq, _ = x.shape
    ts = _seq_tile(seq, 1024)
    tile = pl.BlockSpec((None, ts, D_MODEL), lambda b, t: (b, t, 0))
    return pl.pallas_call(
        _prenorm_kernel,
        grid=(bsz, seq // ts),
        in_specs=[tile,
                  pl.BlockSpec((None, 1, 3 * D_MODEL), lambda b, t: (b + batch_off, 0, 0)),
                  _resident((1, D_MODEL))],
        out_specs=tile,
        out_shape=jax.ShapeDtypeStruct((bsz, seq, D_MODEL), BF16),
        compiler_params=_params(2),
        name="prenorm",
    )(x, mod, norm_gain)


Q_COL = 0
K_COL = Q_COL + ATTN_DIM
V_COL = K_COL + KV_DIM
GA_COL = V_COL + KV_DIM
MG_COL = GA_COL + ATTN_DIM
ATTN_IN_COLS = MG_COL + 2 * D_MODEL


def _attn_in_views(a_ref):
    return (a_ref.at[:, Q_COL:K_COL], a_ref.at[:, K_COL:V_COL], a_ref.at[:, V_COL:GA_COL],
            a_ref.at[:, GA_COL:MG_COL], a_ref.at[:, MG_COL:ATTN_IN_COLS])


def _proj_kernel(h_ref, win_ref, wm_ref, bm_ref, cos_ref, sin_ref,
                 a_ref, rx_ref, gr_ref, *, c_len):
    q_ref, k_ref, v_ref, ga_ref, mg_ref = _attn_in_views(a_ref)
    ts = h_ref.shape[0]
    cos = cos_ref[...]
    sin = sin_ref[...]

    def rotary(p):
        return p * cos + pltpu.roll(p, HEAD_DIM // 2, axis=1) * sin

    def chunk(w_ref, c0):
        return jnp.dot(h_ref[...], w_ref[:, c0:c0 + MXU_COLS], preferred_element_type=F32)

    qk_scale = HEAD_DIM ** -0.5 * LOG2E
    col = 0
    for c in range(ATTN_DIM // MXU_COLS):
        p = chunk(win_ref, col + c * MXU_COLS)
        for hh in range(MXU_COLS // HEAD_DIM):
            dst = slice(c * MXU_COLS + hh * HEAD_DIM, c * MXU_COLS + (hh + 1) * HEAD_DIM)
            q_ref[:, dst] = (rotary(p[:, hh * HEAD_DIM:(hh + 1) * HEAD_DIM]) * qk_scale).astype(BF16)
    col += ATTN_DIM
    p = chunk(win_ref, col)
    for hh in range(N_KV_HEADS):
        dst = slice(hh * HEAD_DIM, (hh + 1) * HEAD_DIM)
        k_ref[:, dst] = rotary(p[:, dst]).astype(BF16)
    col += KV_DIM
    v_ref[...] = chunk(win_ref, col).astype(BF16)
    col += KV_DIM
    for c in range(ATTN_DIM // MXU_COLS):
        dst = slice(c * MXU_COLS, (c + 1) * MXU_COLS)
        ga_ref[:, dst] = _silu(chunk(win_ref, col + c * MXU_COLS)).astype(BF16)
    col += ATTN_DIM
    pitch = rx_ref.shape[0] * c_len // ts
    for c in range(D_RNN // MXU_COLS):
        dst = slice(c * MXU_COLS, (c + 1) * MXU_COLS)
        p = chunk(win_ref, col + c * MXU_COLS)
        for kk in range(ts // c_len):
            rx_ref[kk * pitch:kk * pitch + c_len, dst] = p[kk * c_len:(kk + 1) * c_len]
            rx_ref[kk * pitch + c_len:(kk + 1) * pitch, dst] = jnp.zeros(
                (pitch - c_len, MXU_COLS), F32)
    col += D_RNN
    for c in range(D_RNN // MXU_COLS):
        dst = slice(c * MXU_COLS, (c + 1) * MXU_COLS)
        gr_ref[:, dst] = _silu(chunk(win_ref, col + c * MXU_COLS)).astype(BF16)
    for c in range(2 * D_MODEL // MXU_COLS):
        dst = slice(c * MXU_COLS, (c + 1) * MXU_COLS)
        mg_ref[:, dst] = _sigmoid(chunk(wm_ref, c * MXU_COLS) + bm_ref[:, dst]).astype(BF16)


def _seq_tile(seq, want):
    return want if seq % want == 0 else seq


def _project(h, w_in, w_merge, b_merge, cos, sin):
    bsz, seq, _ = h.shape
    ts = _seq_tile(seq, 512)
    d_in = w_in.shape[1]
    c_len, pitch = _chunking(seq)
    assert ts % c_len == 0
    rx_rows = ts // c_len * pitch

    def tile(width, rows=ts):
        return pl.BlockSpec((None, rows, width), lambda t, b: (b, t, 0))

    outs = [(ATTN_IN_COLS, BF16, seq, ts), (D_RNN, F32, N_CHUNKS * pitch, rx_rows),
            (D_RNN, BF16, seq, ts)]
    return pl.pallas_call(
        functools.partial(_proj_kernel, c_len=c_len),
        grid=(seq // ts, bsz),
        in_specs=[
            tile(D_MODEL),
            _resident((D_MODEL, d_in)),
            _resident((D_MODEL, 2 * D_MODEL)),
            _resident((1, 2 * D_MODEL)),
            pl.BlockSpec((ts, HEAD_DIM), lambda t, b: (t, 0)),
            pl.BlockSpec((ts, HEAD_DIM), lambda t, b: (t, 0)),
        ],
        out_specs=[tile(w, rows) for w, _, _, rows in outs],
        out_shape=[jax.ShapeDtypeStruct((bsz, total, w), dt) for w, dt, total, _ in outs],
        compiler_params=_params(2),
        name="in_proj",
    )(h, w_in, w_merge, b_merge, cos, sin)


N_SETS = 8
N_CHUNKS = N_SETS * SUBLANES
STEP_ROWS = N_SETS * SUBLANES
PITCH_PAD = SUBLANES
EXT_SLOTS = CONV_WIDTH - 1
PHASE_ROWS = 256
SCAN_STEPS = 4
GATHER_STEPS = PHASE_ROWS // STEP_ROWS
TINY = 1e-30


def _chunking(seq):
    c_len = seq // N_CHUNKS
    assert seq % N_CHUNKS == 0 and (c_len * STEP_ROWS) % PHASE_ROWS == 0
    assert c_len % SCAN_STEPS == 0 and c_len % GATHER_STEPS == 0
    assert c_len % (2 * SUBLANES) == 0
    return c_len, c_len + PITCH_PAD


def _sublane_scan(a, b, reverse):
    row = lax.broadcasted_iota(jnp.int32, a.shape, 0)
    for d in (1, 2, 4):
        if reverse:
            keep, shift = row < SUBLANES - d, SUBLANES - d
        else:
            keep, shift = row >= d, d
        a_s = pltpu.roll(a, shift, axis=0)
        b_s = pltpu.roll(b, shift, axis=0)
        b = jnp.where(keep, a * b_s + b, b)
        a = jnp.where(keep, a * a_s, a)
    return a, b


def _rglru_kernel(rx_ref, gr_ref, cw_ref, cb_ref, wg_ref, lam_ref, out_ref,
                  xs, ext, a_scr, b_scr, hs):
    seq = out_ref.shape[0]
    c_len, pitch = _chunking(seq)
    row8 = lax.broadcasted_iota(jnp.int32, (SUBLANES, LANES), 0)
    zero8 = jnp.zeros((SUBLANES, LANES), F32)

    def sets(v):
        return [v[s * SUBLANES:(s + 1) * SUBLANES] for s in range(N_SETS)]

    def slot(i):
        return slice((i + CONV_LEFT) * STEP_ROWS, (i + CONV_LEFT + 1) * STEP_ROWS)

    n_gather = c_len // GATHER_STEPS

    def gather(g):
        i0 = g * GATHER_STEPS
        vals = []
        for kk in range(GATHER_STEPS):
            for s in range(N_SETS):
                vals.append(rx_ref[pl.ds(s * SUBLANES * pitch + i0 + kk, SUBLANES, stride=pitch), :])
        dst = pl.multiple_of((i0 + CONV_LEFT) * STEP_ROWS, STEP_ROWS)
        ext[pl.ds(dst, GATHER_STEPS * STEP_ROWS), :] = jnp.concatenate(vals, axis=0)

    for g in sorted({0, min(1, n_gather - 1), n_gather - 1}):
        gather(g)

    for back in (1, 2):
        tail = [pltpu.roll(v, 1, axis=0) for v in sets(ext[slot(c_len - back), :])]
        ext[slot(-back), :] = jnp.concatenate(
            [jnp.where(row8 == 0, tail[s - 1] if s > 0 else zero8, tail[s])
             for s in range(N_SETS)], axis=0)
    head = [pltpu.roll(v, SUBLANES - 1, axis=0) for v in sets(ext[slot(0), :])]
    ext[slot(c_len), :] = jnp.concatenate(
        [jnp.where(row8 == SUBLANES - 1, head[s + 1] if s < N_SETS - 1 else zero8, head[s])
         for s in range(N_SETS)], axis=0)

    neg = -lam_ref[...]
    softplus = jnp.maximum(neg, 0.0) + jnp.log(1.0 + jnp.exp(-jnp.abs(neg)))
    kexp = softplus * (-RGLRU_C * 0.5 * math.log2(math.e))
    cw = cw_ref[...]
    cb = cb_ref[...]
    ones16 = jnp.ones((PHASE_ROWS, RNN_BLOCK), BF16)

    def order(d):
        return range(SCAN_STEPS - 1, -1, -1) if d else range(SCAN_STEPS)

    def piece(v, kk, s):
        r0 = (kk * N_SETS + s) * SUBLANES
        return v[r0:r0 + SUBLANES]

    def advance(d, av, bv, carry):
        carry = list(carry)
        for kk in order(d):
            for s in range(N_SETS):
                a = piece(av, kk, s)
                h, p = carry[s]
                carry[s] = (a * h + piece(bv, kk, s), a * p)
        return tuple(carry)

    def phase(n, carry):
        gather(jnp.minimum(n + 2, n_gather - 1))
        src = n * PHASE_ROWS
        dst = pl.ds(pl.multiple_of(src, PHASE_ROWS), PHASE_ROWS)
        y = cb
        for tap in range(CONV_WIDTH):
            rows = pl.ds(pl.multiple_of(src + tap * STEP_ROWS, STEP_ROWS), PHASE_ROWS)
            y = y + ext[rows, :] * cw[tap:tap + 1, :]
        lhs = jnp.concatenate([y.astype(BF16), ones16], axis=1)
        for d in range(2):
            g = jnp.dot(lhs, wg_ref[d], preferred_element_type=F32)
            ta = jnp.tanh(g[:, :RNN_BLOCK])
            ti = jnp.tanh(g[:, RNN_BLOCK:])
            k = kexp[d:d + 1, :]
            a = jnp.exp2(k * ta + k)
            z = 1.0 - a * a
            m = z * lax.rsqrt(jnp.maximum(z, TINY))
            b = m * ((ti + 1.0) * y)
            a_scr[d, dst, :] = a
            b_scr[d, dst, :] = b
            if d == 0:
                carry = advance(0, a, b, carry)
        return carry

    ones8 = jnp.ones((SUBLANES, LANES), F32)
    fresh = tuple((zero8, ones8) for _ in range(N_SETS))
    assert PHASE_ROWS == SCAN_STEPS * STEP_ROWS
    ends_f = lax.fori_loop(0, seq // PHASE_ROWS, phase, fresh, unroll=32)

    n_blocks = c_len // SCAN_STEPS
    block_rows = SCAN_STEPS * STEP_ROWS

    def block(d, n):
        blk = (n_blocks - 1 - n) if d else n
        return blk, pl.ds(pl.multiple_of(blk * block_rows, block_rows), block_rows)

    def summarise(d, n, carry):
        _, rows = block(d, n)
        return advance(d, a_scr[d, rows, :], b_scr[d, rows, :], carry)

    def resolve(d, ends):
        edge = zero8
        start = [None] * N_SETS
        for s in (range(N_SETS - 1, -1, -1) if d else range(N_SETS)):
            p_cum, h_cum = _sublane_scan(ends[s][1], ends[s][0], bool(d))
            after = p_cum * edge + h_cum
            if d:
                start[s] = jnp.where(row8 == SUBLANES - 1, edge,
                                     pltpu.roll(after, SUBLANES - 1, axis=0))
                edge = jnp.broadcast_to(after[0:1, :], (SUBLANES, LANES))
            else:
                start[s] = jnp.where(row8 == 0, edge, pltpu.roll(after, 1, axis=0))
                edge = jnp.broadcast_to(after[SUBLANES - 1:SUBLANES, :], (SUBLANES, LANES))
        return tuple(start)

    def rescan(d, n, carry):
        blk, rows = block(d, n)
        av = a_scr[d, rows, :]
        bv = b_scr[d, rows, :]
        hv = hs[rows, :] if d else None
        carry = list(carry)
        out = [None] * (SCAN_STEPS * N_SETS)
        for kk in order(d):
            for s in range(N_SETS):
                h = piece(av, kk, s) * carry[s] + piece(bv, kk, s)
                carry[s] = h
                if d:
                    i = blk * SCAN_STEPS + kk
                    xs[pl.ds(s * SUBLANES * pitch + i, SUBLANES, stride=pitch), :] = (
                        h + piece(hv, kk, s))
                else:
                    out[kk * N_SETS + s] = h
        if not d:
            hs[rows, :] = jnp.concatenate(out, axis=0)
        return tuple(carry)

    _, ends_b = lax.fori_loop(
        0, n_blocks, lambda n, c: (rescan(0, n, c[0]), summarise(1, n, c[1])),
        (resolve(0, ends_f), fresh))
    lax.fori_loop(0, n_blocks, functools.partial(rescan, 1), resolve(1, ends_b))

    for ch in range(N_CHUNKS):
        rows = slice(ch * c_len, (ch + 1) * c_len)
        h = xs[ch * pitch:ch * pitch + c_len, :]
        out_ref[rows, :] = (h * gr_ref[rows, :].astype(F32)).astype(BF16)


def _rglru(rnn_x, gate, conv_w, conv_b, w_gate, lam):
    bsz, seq, _ = gate.shape
    c_len, pitch = _chunking(seq)
    assert rnn_x.shape[1] == N_CHUNKS * pitch

    def col_block(rows):
        return pl.BlockSpec((None, rows, RNN_BLOCK), lambda c, b: (b, 0, c))

    return pl.pallas_call(
        _rglru_kernel,
        grid=(N_RNN_BLOCKS, bsz),
        in_specs=[
            col_block(N_CHUNKS * pitch),
            col_block(seq),
            pl.BlockSpec((CONV_WIDTH, RNN_BLOCK), lambda c, b: (0, c)),
            pl.BlockSpec((1, RNN_BLOCK), lambda c, b: (0, c)),
            pl.BlockSpec((2, None, 2 * RNN_BLOCK, 2 * RNN_BLOCK), lambda c, b: (0, c, 0, 0)),
            pl.BlockSpec((2, RNN_BLOCK), lambda c, b: (0, c)),
        ],
        out_specs=col_block(seq),
        out_shape=jax.ShapeDtypeStruct((bsz, seq, D_RNN), BF16),
        scratch_shapes=[
            pltpu.VMEM((N_CHUNKS * pitch, RNN_BLOCK), F32),
            pltpu.VMEM(((c_len + EXT_SLOTS) * STEP_ROWS, RNN_BLOCK), F32),
            pltpu.VMEM((2, seq, RNN_BLOCK), F32),
            pltpu.VMEM((2, seq, RNN_BLOCK), F32),
            pltpu.VMEM((seq, RNN_BLOCK), F32),
        ],
        compiler_params=_params(2),
        name="rglru",
    )(rnn_x, gate, conv_w, conv_b, w_gate, lam)


MIX_BLOCKS = 4

def _attn_out_kernel(sink_ref, a_ref, kvp_ref, kvn_ref, rnn_ref, x_ref, mod_ref,
                     wa_ref, wr_ref, wo_ref, ng_ref, nmod_ref, *rest, final):
    q_ref, k_ref, v_ref, ga_ref, mg_ref = _attn_in_views(a_ref)
    kp_ref, vp_ref = kvp_ref.at[:, 0:KV_DIM], kvp_ref.at[:, KV_DIM:2 * KV_DIM]
    kn_ref, vn_ref = kvn_ref.at[:, 0:KV_DIM], kvn_ref.at[:, KV_DIM:2 * KV_DIM]
    if final:
        o_ref, kext, vext, kbd, vbd, attn_scr = rest
    else:
        o_ref, h_ref, kext, vext, kbd, vbd, attn_scr = rest
    tq = q_ref.shape[0]
    t = pl.program_id(1)
    nt = pl.num_programs(1)
    n_blk = tq // BLOCK

    kext[0:BLOCK, :] = kp_ref[...]
    kext[BLOCK:BLOCK + tq, :] = k_ref[...]
    kext[BLOCK + tq:, :] = kn_ref[...]
    vext[0:BLOCK, :] = vp_ref[...]
    vext[BLOCK:BLOCK + tq, :] = v_ref[...]
    vext[BLOCK + tq:, :] = vn_ref[...]

    band = 3 * BLOCK
    qi = lax.broadcasted_iota(jnp.int32, (BLOCK, band), 0)
    kj = lax.broadcasted_iota(jnp.int32, (BLOCK, band), 1)
    in_window = (kj >= qi) & (kj <= qi + 2 * BLOCK)
    head = lax.broadcasted_iota(jnp.int32, (GROUP, BLOCK, 1), 0)

    for blk in range(n_blk):
        rows = slice(blk * BLOCK, (blk + 1) * BLOCK)
        band_rows = slice(blk * BLOCK, blk * BLOCK + band)
        valid = in_window
        if blk == 0:
            valid = valid & (kj >= jnp.where(t == 0, BLOCK, 0))
        if blk == n_blk - 1:
            valid = valid & (kj < jnp.where(t == nt - 1, 2 * BLOCK, band))
        zeros = jnp.zeros((band, HEAD_DIM), BF16)
        for g in range(N_KV_HEADS):
            for src, dst in ((kext, kbd), (vext, vbd)):
                for gg in range(N_KV_HEADS):
                    cols = slice(gg * HEAD_DIM, (gg + 1) * HEAD_DIM)
                    dst[blk, g * band:(g + 1) * band, cols] = (
                        src[band_rows, cols] if gg == g else zeros)
        qcat = jnp.concatenate(
            [jnp.concatenate(
                [q_ref[rows, (g * GROUP + h) * HEAD_DIM:(g * GROUP + h + 1) * HEAD_DIM]
                 for g in range(N_KV_HEADS)], axis=1) for h in range(GROUP)], axis=0)
        s_all = lax.dot_general(qcat, kbd[blk], (((1,), (1,)), ((), ())),
                                preferred_element_type=F32)
        probs, denoms = [], []
        for g in range(N_KV_HEADS):
            s = s_all[:, g * band:(g + 1) * band].reshape(GROUP, BLOCK, band)
            s = jnp.where(valid[None], s, MASK_VALUE)
            sink = jnp.full((GROUP, BLOCK, 1), sink_ref[g * GROUP] * LOG2E, F32)
            for h in range(1, GROUP):
                sink = jnp.where(head == h, sink_ref[g * GROUP + h] * LOG2E, sink)
            m = jnp.maximum(jnp.max(s, axis=-1, keepdims=True), sink)
            p = jnp.exp2(s - m)
            denoms.append(jnp.sum(p, axis=-1, keepdims=True) + jnp.exp2(sink - m))
            probs.append(p.reshape(GROUP * BLOCK, band).astype(BF16))
        o_all = jnp.dot(jnp.concatenate(probs, axis=1), vbd[blk],
                        preferred_element_type=F32)
        for g in range(N_KV_HEADS):
            o = o_all[:, g * HEAD_DIM:(g + 1) * HEAD_DIM].reshape(GROUP, BLOCK, HEAD_DIM)
            o = o / denoms[g]
            for h in range(GROUP):
                cols = slice((g * GROUP + h) * HEAD_DIM, (g * GROUP + h + 1) * HEAD_DIM)
                attn_scr[rows, cols] = (o[h] * ga_ref[rows, cols].astype(F32)).astype(BF16)

        if (blk + 1) % MIX_BLOCKS == 0 or blk == n_blk - 1:
            lo_row = (blk // MIX_BLOCKS) * MIX_BLOCKS * BLOCK
            rows = slice(lo_row, (blk + 1) * BLOCK)
            a = jnp.dot(attn_scr[rows, :], wa_ref[...], preferred_element_type=F32)
            r = jnp.dot(rnn_ref[rows, :], wr_ref[...], preferred_element_type=F32)
            mixed = (mg_ref[rows, 0:D_MODEL].astype(F32) * a
                     + mg_ref[rows, D_MODEL:].astype(F32) * r)
            y = jnp.dot(mixed.astype(BF16), wo_ref[...], preferred_element_type=F32)
            out = x_ref[rows, :] + mod_ref[:, 2 * D_MODEL:] * y
            if final:
                ms = jnp.mean(out * out, axis=-1, keepdims=True)
                o_ref[rows, :] = out * lax.rsqrt(ms + EPS) * ng_ref[...]
            else:
                o_ref[rows, :] = out
                h_ref[rows, :] = _adaln(out, ng_ref[...], nmod_ref).astype(BF16)


def _attend_and_mix(sink, attn_in, rnn, x, mod, w_attn, w_rnn, w_out, next_gain,
                    next_mod, batch_off, final):
    bsz, seq, _ = x.shape
    tq = _seq_tile(seq, 512)
    per = tq // BLOCK
    n_blocks = seq // BLOCK

    def tile(width):
        return pl.BlockSpec((None, tq, width), lambda b, t: (b, t, 0))

    kv_blk = K_COL // (2 * KV_DIM)
    assert K_COL % (2 * KV_DIM) == 0 and V_COL == K_COL + KV_DIM
    prev_kv = pl.BlockSpec((None, BLOCK, 2 * KV_DIM),
                           lambda b, t: (b, jnp.maximum(t * per - 1, 0), kv_blk))
    next_kv = pl.BlockSpec((None, BLOCK, 2 * KV_DIM),
                           lambda b, t: (b, jnp.minimum((t + 1) * per, n_blocks - 1), kv_blk))
    mod_spec = pl.BlockSpec((None, 1, 3 * D_MODEL), lambda b, t: (b + batch_off, 0, 0))
    out_specs = [tile(D_MODEL)]
    out_shape = [jax.ShapeDtypeStruct((bsz, seq, D_MODEL), F32)]
    if not final:
        out_specs.append(tile(D_MODEL))
        out_shape.append(jax.ShapeDtypeStruct((bsz, seq, D_MODEL), BF16))
    return pl.pallas_call(
        functools.partial(_attn_out_kernel, final=final),
        grid=(bsz, seq // tq),
        in_specs=[
            pl.BlockSpec(memory_space=pltpu.SMEM),
            tile(ATTN_IN_COLS), prev_kv, next_kv,
            tile(D_RNN),
            tile(D_MODEL),
            mod_spec,
            _resident((ATTN_DIM, D_MODEL)),
            _resident((D_RNN, D_MODEL)),
            _resident((D_MODEL, D_MODEL)),
            _resident((1, D_MODEL)),
            mod_spec,
        ],
        out_specs=out_specs,
        out_shape=out_shape,
        scratch_shapes=[
            pltpu.VMEM((tq + 2 * BLOCK, KV_DIM), BF16),
            pltpu.VMEM((tq + 2 * BLOCK, KV_DIM), BF16),
            pltpu.VMEM((per, N_KV_HEADS * 3 * BLOCK, KV_DIM), BF16),
            pltpu.VMEM((per, N_KV_HEADS * 3 * BLOCK, KV_DIM), BF16),
            pltpu.VMEM((tq, ATTN_DIM), BF16),
        ],
        compiler_params=_params(2),
        name="attn_out",
    )(sink, attn_in, attn_in, attn_in, rnn, x, mod, w_attn, w_rnn, w_out, next_gain, next_mod)


def _trunk(x, batch_off, mods, tables, layers, final_gain):
    cos, sin = tables
    seq = x.shape[1]
    cos, sin = cos[:seq], sin[:seq]
    depth = len(layers)
    h = _prenorm(x, mods[0], layers[0]["norm_gain"], batch_off)
    for l, p in enumerate(layers):
        final = l == depth - 1
        attn_in, rx, gr = _project(h, p["w_in"], p["w_merge"], p["b_merge"], cos, sin)
        rnn = _rglru(rx, gr, p["conv_w"], p["conv_b"], p["w_gate"], p["lam"])
        next_gain = final_gain if final else layers[l + 1]["norm_gain"]
        next_mod = mods[l] if final else mods[l + 1]
        res = _attend_and_mix(p["sink"], attn_in, rnn, x, mods[l], p["w_attn"],
                              p["w_rnn"], p["w_out"], next_gain, next_mod, batch_off, final)
        x = res[0]
        h = None if final else res[1]
    return x


def kernel(x_prompt, x_sample, c_prompt, c_sample, norm_gain, w_ada, b_ada, w_in, attn_sink, conv_w, conv_b, rg_w_a, rg_b_a, rg_w_x, rg_b_x, rg_lambda, w_attn_proj, w_rnn_proj, w_merge, b_merge, w_out, final_gain):
    depth = w_in.shape[0]
    n_prompt = x_prompt.shape[0]
    cond = jnp.concatenate([c_prompt, c_sample], axis=0)
    mods = _modulation(cond, w_ada, b_ada)
    mods = mods.reshape(depth, cond.shape[0], 1, 3 * D_MODEL)
    tables = _rope_tables(max(x_prompt.shape[1], x_sample.shape[1]))

    w_gate = jnp.concatenate([rg_w_a, rg_w_x], axis=-1).astype(BF16)
    b_half = 0.5 * jnp.concatenate(
        [rg_b_a.reshape(depth, 2, N_RNN_BLOCKS, 1, RNN_BLOCK),
         rg_b_x.reshape(depth, 2, N_RNN_BLOCKS, 1, RNN_BLOCK)], axis=-1)
    b_hi = b_half.astype(BF16)
    b_lo = (b_half - b_hi.astype(F32)).astype(BF16)
    w_gate = jnp.concatenate(
        [w_gate, b_hi, b_lo,
         jnp.zeros((depth, 2, N_RNN_BLOCKS, RNN_BLOCK - 2, 2 * RNN_BLOCK), BF16)], axis=-2)
    layers = []
    for l in range(depth):
        layers.append(dict(
            norm_gain=norm_gain[l].reshape(1, D_MODEL),
            w_in=w_in[l].astype(BF16),
            w_merge=w_merge[l].astype(BF16),
            b_merge=b_merge[l].reshape(1, 2 * D_MODEL),
            sink=attn_sink[l],
            conv_w=0.5 * conv_w[l],
            conv_b=0.5 * conv_b[l].reshape(1, D_RNN),
            w_gate=w_gate[l], lam=rg_lambda[l],
            w_attn=w_attn_proj[l].astype(BF16),
            w_rnn=w_rnn_proj[l].astype(BF16),
            w_out=w_out[l].astype(BF16),
        ))
    fg = final_gain.reshape(1, D_MODEL)
    y_prompt = _trunk(x_prompt, 0, mods, tables, layers, fg)
    y_sample = _trunk(x_sample, n_prompt, mods, tables, layers, fg)
    return (y_prompt, y_sample)
```

```python
import functools
import math

import jax
import jax.numpy as jnp
from jax import lax
from jax.experimental import pallas as pl
from jax.experimental.pallas import tpu as pltpu

D_MODEL = 1024
HEAD_DIM = 128
N_HEADS = 8
N_KV_HEADS = 2
GROUP = N_HEADS // N_KV_HEADS
ATTN_DIM = N_HEADS * HEAD_DIM
KV_DIM = N_KV_HEADS * HEAD_DIM
BLOCK = 128
ROPE_THETA = 10000.0
D_RNN = 3 * D_MODEL // 2
N_RNN_BLOCKS = 12
RNN_BLOCK = D_RNN // N_RNN_BLOCKS
CONV_WIDTH = 4
CONV_LEFT = 2
RGLRU_C = 8.0
EPS = 1e-6
MASK_VALUE = -1e30
LOG2E = math.log2(math.e)

SUBLANES = 8
LANES = 128
MXU_COLS = 256
VMEM_LIMIT_BYTES = 56 * 1024 * 1024

F32 = jnp.float32
BF16 = jnp.bfloat16


def _sigmoid_of_double(xh):
    return 0.5 * jnp.tanh(xh) + 0.5


def _silu_of_double(xh):
    return xh * jnp.tanh(xh) + xh


def _resident(shape):
    zeros = (0,) * len(shape)
    return pl.BlockSpec(shape, lambda *_: zeros, pipeline_mode=pl.Buffered(1))


def _params(n_axes):
    return pltpu.CompilerParams(dimension_semantics=("arbitrary",) * n_axes,
                                vmem_limit_bytes=VMEM_LIMIT_BYTES)


def _mod_kernel(c_ref, w_ref, b_ref, o_ref):
    c = c_ref[...]
    s = c * jax.nn.sigmoid(c)
    o_ref[...] = jnp.dot(s, w_ref[...], precision=lax.Precision.HIGHEST,
                         preferred_element_type=F32) + b_ref[...]


def _modulation(cond, w_ada, b_ada):
    depth = w_ada.shape[0]
    nb = cond.shape[0]
    n_col = 3 * D_MODEL // D_MODEL
    return pl.pallas_call(
        _mod_kernel,
        grid=(depth, n_col),
        in_specs=[
            pl.BlockSpec((nb, D_MODEL), lambda l, j: (0, 0)),
            pl.BlockSpec((None, D_MODEL, D_MODEL), lambda l, j: (l, 0, j)),
            pl.BlockSpec((None, 1, D_MODEL), lambda l, j: (l, 0, j)),
        ],
        out_specs=pl.BlockSpec((None, nb, D_MODEL), lambda l, j: (l, 0, j)),
        out_shape=jax.ShapeDtypeStruct((depth, nb, 3 * D_MODEL), F32),
        compiler_params=_params(2),
        name="adaln_mod",
    )(cond, w_ada, b_ada.reshape(depth, 1, 3 * D_MODEL))


def _rope_kernel(cos_ref, sin_ref):
    rows = cos_ref.shape[0]
    t = lax.broadcasted_iota(jnp.int32, (rows, HEAD_DIM), 0) + pl.program_id(0) * rows
    j = lax.broadcasted_iota(jnp.int32, (rows, HEAD_DIM), 1)
    half = HEAD_DIM // 2
    jj = jnp.where(j < half, j, j - half).astype(F32)
    inv_freq = jnp.exp(jj * (-2.0 / HEAD_DIM * math.log(ROPE_THETA)))
    ang = t.astype(F32) * inv_freq
    s = jnp.sin(ang)
    cos_ref[...] = jnp.cos(ang)
    sin_ref[...] = jnp.where(j < half, -s, s)


def _rope_tables(seq):
    rows = min(seq, 512)
    return pl.pallas_call(
        _rope_kernel,
        grid=(seq // rows,),
        out_specs=[pl.BlockSpec((rows, HEAD_DIM), lambda i: (i, 0))] * 2,
        out_shape=[jax.ShapeDtypeStruct((seq, HEAD_DIM), F32)] * 2,
        compiler_params=_params(1),
        name="rope_tables",
    )()


NORM_ROWS = 32


def _adaln(x, gain, mod_ref):
    ms = jnp.mean(x * x, axis=-1, keepdims=True)
    gain_scale = gain * (1.0 + mod_ref[:, D_MODEL:2 * D_MODEL])
    return x * lax.rsqrt(ms + EPS) * gain_scale + mod_ref[:, 0:D_MODEL]


def _prenorm_kernel(x_ref, mod_ref, ng_ref, h_ref):
    gain = ng_ref[...]

    def norm_rows(r, carry):
        rows = pl.ds(pl.multiple_of(r * NORM_ROWS, NORM_ROWS), NORM_ROWS)
        h_ref[rows, :] = _adaln(x_ref[rows, :], gain, mod_ref).astype(BF16)
        return carry

    lax.fori_loop(0, x_ref.shape[0] // NORM_ROWS, norm_rows, 0, unroll=4)


def _prenorm(x, mod, norm_gain, batch_off):
    bsz, seq, _ = x.shape
    ts = _seq_tile(seq, 1024)
    tile = pl.BlockSpec((None, ts, D_MODEL), lambda b, t: (b, t, 0))
    return pl.pallas_call(
        _prenorm_kernel,
        grid=(bsz, seq // ts),
        in_specs=[tile,
                  pl.BlockSpec((None, 1, 3 * D_MODEL), lambda b, t: (b + batch_off, 0, 0)),
                  _resident((1, D_MODEL))],
        out_specs=tile,
        out_shape=jax.ShapeDtypeStruct((bsz, seq, D_MODEL), BF16),
        compiler_params=_params(2),
        name="prenorm",
    )(x, mod, norm_gain)


Q_COL = 0
K_COL = Q_COL + ATTN_DIM
V_COL = K_COL + KV_DIM
GA_COL = V_COL + KV_DIM
MG_COL = GA_COL + ATTN_DIM
ATTN_IN_COLS = MG_COL + 2 * D_MODEL


def _attn_in_views(a_ref):
    return (a_ref.at[:, Q_COL:K_COL], a_ref.at[:, K_COL:V_COL], a_ref.at[:, V_COL:GA_COL],
            a_ref.at[:, GA_COL:MG_COL], a_ref.at[:, MG_COL:ATTN_IN_COLS])


def _proj_kernel(h_ref, win_ref, wm_ref, bm_ref, cos_ref, sin_ref,
                 a_ref, rx_ref, gr_ref, *, c_len):
    q_ref, k_ref, v_ref, ga_ref, mg_ref = _attn_in_views(a_ref)
    ts = h_ref.shape[0]
    cos = cos_ref[...]
    sin = sin_ref[...]

    def rotary(p):
        return p * cos + pltpu.roll(p, HEAD_DIM // 2, axis=1) * sin

    def chunk(w_ref, c0):
        return jnp.dot(h_ref[...], w_ref[:, c0:c0 + MXU_COLS], preferred_element_type=F32)

    qk_scale = HEAD_DIM ** -0.5 * LOG2E
    col = 0
    for c in range(ATTN_DIM // MXU_COLS):
        p = chunk(win_ref, col + c * MXU_COLS)
        for hh in range(MXU_COLS // HEAD_DIM):
            dst = slice(c * MXU_COLS + hh * HEAD_DIM, c * MXU_COLS + (hh + 1) * HEAD_DIM)
            q_ref[:, dst] = (rotary(p[:, hh * HEAD_DIM:(hh + 1) * HEAD_DIM]) * qk_scale).astype(BF16)
    col += ATTN_DIM
    p = chunk(win_ref, col)
    for hh in range(N_KV_HEADS):
        dst = slice(hh * HEAD_DIM, (hh + 1) * HEAD_DIM)
        k_ref[:, dst] = rotary(p[:, dst]).astype(BF16)
    col += KV_DIM
    v_ref[...] = chunk(win_ref, col).astype(BF16)
    col += KV_DIM
    for c in range(ATTN_DIM // MXU_COLS):
        dst = slice(c * MXU_COLS, (c + 1) * MXU_COLS)
        ga_ref[:, dst] = _silu_of_double(chunk(win_ref, col + c * MXU_COLS)).astype(BF16)
    col += ATTN_DIM
    pitch = rx_ref.shape[0] * c_len // ts
    for c in range(D_RNN // MXU_COLS):
        dst = slice(c * MXU_COLS, (c + 1) * MXU_COLS)
        p = chunk(win_ref, col + c * MXU_COLS)
        for kk in range(ts // c_len):
            rx_ref[kk * pitch:kk * pitch + c_len, dst] = p[kk * c_len:(kk + 1) * c_len]
            rx_ref[kk * pitch + c_len:(kk + 1) * pitch, dst] = jnp.zeros(
                (pitch - c_len, MXU_COLS), F32)
    col += D_RNN
    for c in range(D_RNN // MXU_COLS):
        dst = slice(c * MXU_COLS, (c + 1) * MXU_COLS)
        gr_ref[:, dst] = _silu_of_double(chunk(win_ref, col + c * MXU_COLS)).astype(BF16)
    for c in range(2 * D_MODEL // MXU_COLS):
        dst = slice(c * MXU_COLS, (c + 1) * MXU_COLS)
        mg_ref[:, dst] = _sigmoid_of_double(
            chunk(wm_ref, c * MXU_COLS) + bm_ref[:, dst]).astype(BF16)


def _seq_tile(seq, want):
    return want if seq % want == 0 else seq


def _project(h, w_in, w_merge, b_merge, cos, sin):
    bsz, seq, _ = h.shape
    ts = _seq_tile(seq, 512)
    d_in = w_in.shape[1]
    c_len, pitch = _chunking(seq)
    assert ts % c_len == 0
    rx_rows = ts // c_len * pitch

    def tile(width, rows=ts):
        return pl.BlockSpec((None, rows, width), lambda t, b: (b, t, 0))

    outs = [(ATTN_IN_COLS, BF16, seq, ts), (D_RNN, F32, N_CHUNKS * pitch, rx_rows),
            (D_RNN, BF16, seq, ts)]
    return pl.pallas_call(
        functools.partial(_proj_kernel, c_len=c_len),
        grid=(seq // ts, bsz),
        in_specs=[
            tile(D_MODEL),
            _resident((D_MODEL, d_in)),
            _resident((D_MODEL, 2 * D_MODEL)),
            _resident((1, 2 * D_MODEL)),
            pl.BlockSpec((ts, HEAD_DIM), lambda t, b: (t, 0)),
            pl.BlockSpec((ts, HEAD_DIM), lambda t, b: (t, 0)),
        ],
        out_specs=[tile(w, rows) for w, _, _, rows in outs],
        out_shape=[jax.ShapeDtypeStruct((bsz, total, w), dt) for w, dt, total, _ in outs],
        compiler_params=_params(2),
        name="in_proj",
    )(h, w_in, w_merge, b_merge, cos, sin)


N_SETS = 8
N_CHUNKS = N_SETS * SUBLANES
STEP_ROWS = N_SETS * SUBLANES
PITCH_PAD = SUBLANES
EXT_SLOTS = CONV_WIDTH - 1
PHASE_ROWS = 256
SCAN_STEPS = 4
GATHER_STEPS = PHASE_ROWS // STEP_ROWS
TINY = 1e-30


def _chunking(seq):
    c_len = seq // N_CHUNKS
    assert seq % N_CHUNKS == 0 and (c_len * STEP_ROWS) % PHASE_ROWS == 0
    assert c_len % SCAN_STEPS == 0 and c_len % GATHER_STEPS == 0
    assert c_len % (2 * SUBLANES) == 0
    return c_len, c_len + PITCH_PAD


def _sublane_scan(a, b, reverse):
    row = lax.broadcasted_iota(jnp.int32, a.shape, 0)
    for d in (1, 2, 4):
        if reverse:
            keep, shift = row < SUBLANES - d, SUBLANES - d
        else:
            keep, shift = row >= d, d
        a_s = pltpu.roll(a, shift, axis=0)
        b_s = pltpu.roll(b, shift, axis=0)
        b = jnp.where(keep, a * b_s + b, b)
        a = jnp.where(keep, a * a_s, a)
    return a, b


def _rglru_kernel(rx_ref, gr_ref, cw_ref, cb_ref, wg_ref, lam_ref, out_ref,
                  xs, ext, a_scr, b_scr, hs):
    seq = out_ref.shape[0]
    c_len, pitch = _chunking(seq)
    row8 = lax.broadcasted_iota(jnp.int32, (SUBLANES, LANES), 0)
    zero8 = jnp.zeros((SUBLANES, LANES), F32)

    def sets(v):
        return [v[s * SUBLANES:(s + 1) * SUBLANES] for s in range(N_SETS)]

    def slot(i):
        return slice((i + CONV_LEFT) * STEP_ROWS, (i + CONV_LEFT + 1) * STEP_ROWS)

    n_gather = c_len // GATHER_STEPS

    def gather(g):
        i0 = g * GATHER_STEPS
        vals = []
        for kk in range(GATHER_STEPS):
            for s in range(N_SETS):
                vals.append(rx_ref[pl.ds(s * SUBLANES * pitch + i0 + kk, SUBLANES, stride=pitch), :])
        dst = pl.multiple_of((i0 + CONV_LEFT) * STEP_ROWS, STEP_ROWS)
        ext[pl.ds(dst, GATHER_STEPS * STEP_ROWS), :] = jnp.concatenate(vals, axis=0)

    for g in sorted({0, min(1, n_gather - 1), n_gather - 1}):
        gather(g)

    for back in (1, 2):
        tail = [pltpu.roll(v, 1, axis=0) for v in sets(ext[slot(c_len - back), :])]
        ext[slot(-back), :] = jnp.concatenate(
            [jnp.where(row8 == 0, tail[s - 1] if s > 0 else zero8, tail[s])
             for s in range(N_SETS)], axis=0)
    head = [pltpu.roll(v, SUBLANES - 1, axis=0) for v in sets(ext[slot(0), :])]
    ext[slot(c_len), :] = jnp.concatenate(
        [jnp.where(row8 == SUBLANES - 1, head[s + 1] if s < N_SETS - 1 else zero8, head[s])
         for s in range(N_SETS)], axis=0)

    neg = -lam_ref[...]
    softplus = jnp.maximum(neg, 0.0) + jnp.log(1.0 + jnp.exp(-jnp.abs(neg)))
    kexp = softplus * (-RGLRU_C * 0.5 * math.log2(math.e))
    cw = cw_ref[...]
    cb = cb_ref[...]
    ones16 = jnp.ones((PHASE_ROWS, RNN_BLOCK), BF16)

    def order(d):
        return range(SCAN_STEPS - 1, -1, -1) if d else range(SCAN_STEPS)

    def piece(v, kk, s):
        r0 = (kk * N_SETS + s) * SUBLANES
        return v[r0:r0 + SUBLANES]

    def advance(d, av, bv, carry):
        carry = list(carry)
        for kk in order(d):
            for s in range(N_SETS):
                a = piece(av, kk, s)
                h, p = carry[s]
                carry[s] = (a * h + piece(bv, kk, s), a * p)
        return tuple(carry)

    def phase(n, carry):
        gather(jnp.minimum(n + 2, n_gather - 1))
        src = n * PHASE_ROWS
        dst = pl.ds(pl.multiple_of(src, PHASE_ROWS), PHASE_ROWS)
        y = cb
        for tap in range(CONV_WIDTH):
            rows = pl.ds(pl.multiple_of(src + tap * STEP_ROWS, STEP_ROWS), PHASE_ROWS)
            y = y + ext[rows, :] * cw[tap:tap + 1, :]
        lhs = jnp.concatenate([y.astype(BF16), ones16], axis=1)
        for d in range(2):
            g = jnp.dot(lhs, wg_ref[d], preferred_element_type=F32)
            ta = jnp.tanh(g[:, :RNN_BLOCK])
            ti = jnp.tanh(g[:, RNN_BLOCK:])
            k = kexp[d:d + 1, :]
            a = jnp.exp2(k * ta + k)
            z = 1.0 - a * a
            m = z * lax.rsqrt(jnp.maximum(z, TINY))
            b = m * ((ti + 1.0) * y)
            a_scr[d, dst, :] = a
            b_scr[d, dst, :] = b
            if d == 0:
                carry = advance(0, a, b, carry)
        return carry

    ones8 = jnp.ones((SUBLANES, LANES), F32)
    fresh = tuple((zero8, ones8) for _ in range(N_SETS))
    assert PHASE_ROWS == SCAN_STEPS * STEP_ROWS
    ends_f = lax.fori_loop(0, seq // PHASE_ROWS, phase, fresh, unroll=32)

    n_blocks = c_len // SCAN_STEPS
    block_rows = SCAN_STEPS * STEP_ROWS

    def block(d, n):
        blk = (n_blocks - 1 - n) if d else n
        return blk, pl.ds(pl.multiple_of(blk * block_rows, block_rows), block_rows)

    def summarise(d, n, carry):
        _, rows = block(d, n)
        return advance(d, a_scr[d, rows, :], b_scr[d, rows, :], carry)

    def resolve(d, ends):
        edge = zero8
        start = [None] * N_SETS
        for s in (range(N_SETS - 1, -1, -1) if d else range(N_SETS)):
            p_cum, h_cum = _sublane_scan(ends[s][1], ends[s][0], bool(d))
            after = p_cum * edge + h_cum
            if d:
                start[s] = jnp.where(row8 == SUBLANES - 1, edge,
                                     pltpu.roll(after, SUBLANES - 1, axis=0))
                edge = jnp.broadcast_to(after[0:1, :], (SUBLANES, LANES))
            else:
                start[s] = jnp.where(row8 == 0, edge, pltpu.roll(after, 1, axis=0))
                edge = jnp.broadcast_to(after[SUBLANES - 1:SUBLANES, :], (SUBLANES, LANES))
        return tuple(start)

    def rescan(d, n, carry):
        blk, rows = block(d, n)
        av = a_scr[d, rows, :]
        bv = b_scr[d, rows, :]
        hv = hs[rows, :] if d else None
        carry = list(carry)
        out = [None] * (SCAN_STEPS * N_SETS)
        for kk in order(d):
            for s in range(N_SETS):
                h = piece(av, kk, s) * carry[s] + piece(bv, kk, s)
                carry[s] = h
                if d:
                    i = blk * SCAN_STEPS + kk
                    xs[pl.ds(s * SUBLANES * pitch + i, SUBLANES, stride=pitch), :] = (
                        h + piece(hv, kk, s))
                else:
                    out[kk * N_SETS + s] = h
        if not d:
            hs[rows, :] = jnp.concatenate(out, axis=0)
        return tuple(carry)

    _, ends_b = lax.fori_loop(
        0, n_blocks, lambda n, c: (rescan(0, n, c[0]), summarise(1, n, c[1])),
        (resolve(0, ends_f), fresh))
    lax.fori_loop(0, n_blocks, functools.partial(rescan, 1), resolve(1, ends_b))

    for ch in range(N_CHUNKS):
        rows = slice(ch * c_len, (ch + 1) * c_len)
        h = xs[ch * pitch:ch * pitch + c_len, :]
        out_ref[rows, :] = (h * gr_ref[rows, :].astype(F32)).astype(BF16)


def _rglru(rnn_x, gate, conv_w, conv_b, w_gate, lam):
    bsz, seq, _ = gate.shape
    c_len, pitch = _chunking(seq)
    assert rnn_x.shape[1] == N_CHUNKS * pitch

    def col_block(rows):
        return pl.BlockSpec((None, rows, RNN_BLOCK), lambda c, b: (b, 0, c))

    return pl.pallas_call(
        _rglru_kernel,
        grid=(N_RNN_BLOCKS, bsz),
        in_specs=[
            col_block(N_CHUNKS * pitch),
            col_block(seq),
            pl.BlockSpec((CONV_WIDTH, RNN_BLOCK), lambda c, b: (0, c)),
            pl.BlockSpec((1, RNN_BLOCK), lambda c, b: (0, c)),
            pl.BlockSpec((2, None, 2 * RNN_BLOCK, 2 * RNN_BLOCK), lambda c, b: (0, c, 0, 0)),
            pl.BlockSpec((2, RNN_BLOCK), lambda c, b: (0, c)),
        ],
        out_specs=col_block(seq),
        out_shape=jax.ShapeDtypeStruct((bsz, seq, D_RNN), BF16),
        scratch_shapes=[
            pltpu.VMEM((N_CHUNKS * pitch, RNN_BLOCK), F32),
            pltpu.VMEM(((c_len + EXT_SLOTS) * STEP_ROWS, RNN_BLOCK), F32),
            pltpu.VMEM((2, seq, RNN_BLOCK), F32),
            pltpu.VMEM((2, seq, RNN_BLOCK), F32),
            pltpu.VMEM((seq, RNN_BLOCK), F32),
        ],
        compiler_params=_params(2),
        name="rglru",
    )(rnn_x, gate, conv_w, conv_b, w_gate, lam)


MIX_BLOCKS = 4

def _attn_out_kernel(sink_ref, a_ref, kvp_ref, kvn_ref, rnn_ref, x_ref, mod_ref,
                     wa_ref, wr_ref, wo_ref, ng_ref, nmod_ref, *rest, final):
    q_ref, k_ref, v_ref, ga_ref, mg_ref = _attn_in_views(a_ref)
    kp_ref, vp_ref = kvp_ref.at[:, 0:KV_DIM], kvp_ref.at[:, KV_DIM:2 * KV_DIM]
    kn_ref, vn_ref = kvn_ref.at[:, 0:KV_DIM], kvn_ref.at[:, KV_DIM:2 * KV_DIM]
    if final:
        o_ref, kbd, vbd, attn_scr = rest
    else:
        o_ref, h_ref, kbd, vbd, attn_scr = rest
    tq = q_ref.shape[0]
    t = pl.program_id(1)
    nt = pl.num_programs(1)
    n_blk = tq // BLOCK

    def band_piece(tile_ref, prev_ref, next_ref, j, cols):
        if j < 0:
            return prev_ref[:, cols]
        if j >= n_blk:
            return next_ref[:, cols]
        return tile_ref[j * BLOCK:(j + 1) * BLOCK, cols]

    band = 3 * BLOCK
    qi = lax.broadcasted_iota(jnp.int32, (BLOCK, band), 0)
    kj = lax.broadcasted_iota(jnp.int32, (BLOCK, band), 1)
    in_window = (kj >= qi) & (kj <= qi + 2 * BLOCK)
    head = lax.broadcasted_iota(jnp.int32, (GROUP, BLOCK, 1), 0)

    for blk in range(n_blk):
        rows = slice(blk * BLOCK, (blk + 1) * BLOCK)
        valid = in_window
        if blk == 0:
            valid = valid & (kj >= jnp.where(t == 0, BLOCK, 0))
        if blk == n_blk - 1:
            valid = valid & (kj < jnp.where(t == nt - 1, 2 * BLOCK, band))
        zeros = jnp.zeros((BLOCK, HEAD_DIM), BF16)
        for g in range(N_KV_HEADS):
            for src, dst in (((k_ref, kp_ref, kn_ref), kbd), ((v_ref, vp_ref, vn_ref), vbd)):
                for gg in range(N_KV_HEADS):
                    cols = slice(gg * HEAD_DIM, (gg + 1) * HEAD_DIM)
                    for j in range(3):
                        r0 = g * band + j * BLOCK
                        dst[blk, r0:r0 + BLOCK, cols] = (
                            band_piece(*src, blk - 1 + j, cols) if gg == g else zeros)
        qcat = jnp.concatenate(
            [jnp.concatenate(
                [q_ref[rows, (g * GROUP + h) * HEAD_DIM:(g * GROUP + h + 1) * HEAD_DIM]
                 for g in range(N_KV_HEADS)], axis=1) for h in range(GROUP)], axis=0)
        s_all = lax.dot_general(qcat, kbd[blk], (((1,), (1,)), ((), ())),
                                preferred_element_type=F32)
        probs, denoms = [], []
        for g in range(N_KV_HEADS):
            s = s_all[:, g * band:(g + 1) * band].reshape(GROUP, BLOCK, band)
            s = jnp.where(valid[None], s, MASK_VALUE)
            sink = jnp.full((GROUP, BLOCK, 1), sink_ref[g * GROUP] * LOG2E, F32)
            for h in range(1, GROUP):
                sink = jnp.where(head == h, sink_ref[g * GROUP + h] * LOG2E, sink)
            m = jnp.maximum(jnp.max(s, axis=-1, keepdims=True), sink)
            p = jnp.exp2(s - m)
            denoms.append(jnp.sum(p, axis=-1, keepdims=True) + jnp.exp2(sink - m))
            probs.append(p.reshape(GROUP * BLOCK, band).astype(BF16))
        o_all = jnp.dot(jnp.concatenate(probs, axis=1), vbd[blk],
                        preferred_element_type=F32)
        for g in range(N_KV_HEADS):
            o = o_all[:, g * HEAD_DIM:(g + 1) * HEAD_DIM].reshape(GROUP, BLOCK, HEAD_DIM)
            o = o / denoms[g]
            for h in range(GROUP):
                cols = slice((g * GROUP + h) * HEAD_DIM, (g * GROUP + h + 1) * HEAD_DIM)
                attn_scr[rows, cols] = (o[h] * ga_ref[rows, cols].astype(F32)).astype(BF16)

        if (blk + 1) % MIX_BLOCKS == 0 or blk == n_blk - 1:
            lo_row = (blk // MIX_BLOCKS) * MIX_BLOCKS * BLOCK
            rows = slice(lo_row, (blk + 1) * BLOCK)
            a = jnp.dot(attn_scr[rows, :], wa_ref[...], preferred_element_type=F32)
            r = jnp.dot(rnn_ref[rows, :], wr_ref[...], preferred_element_type=F32)
            mixed = (mg_ref[rows, 0:D_MODEL].astype(F32) * a
                     + mg_ref[rows, D_MODEL:].astype(F32) * r)
            y = jnp.dot(mixed.astype(BF16), wo_ref[...], preferred_element_type=F32)
            out = x_ref[rows, :] + mod_ref[:, 2 * D_MODEL:] * y
            if final:
                ms = jnp.mean(out * out, axis=-1, keepdims=True)
                o_ref[rows, :] = out * lax.rsqrt(ms + EPS) * ng_ref[...]
            else:
                o_ref[rows, :] = out
                h_ref[rows, :] = _adaln(out, ng_ref[...], nmod_ref).astype(BF16)


def _attend_and_mix(sink, attn_in, rnn, x, mod, w_attn, w_rnn, w_out, next_gain,
                    next_mod, batch_off, final):
    bsz, seq, _ = x.shape
    tq = _seq_tile(seq, 512)
    per = tq // BLOCK
    n_blocks = seq // BLOCK

    def tile(width):
        return pl.BlockSpec((None, tq, width), lambda b, t: (b, t, 0))

    kv_blk = K_COL // (2 * KV_DIM)
    assert K_COL % (2 * KV_DIM) == 0 and V_COL == K_COL + KV_DIM
    prev_kv = pl.BlockSpec((None, BLOCK, 2 * KV_DIM),
                           lambda b, t: (b, jnp.maximum(t * per - 1, 0), kv_blk))
    next_kv = pl.BlockSpec((None, BLOCK, 2 * KV_DIM),
                           lambda b, t: (b, jnp.minimum((t + 1) * per, n_blocks - 1), kv_blk))
    mod_spec = pl.BlockSpec((None, 1, 3 * D_MODEL), lambda b, t: (b + batch_off, 0, 0))
    out_specs = [tile(D_MODEL)]
    out_shape = [jax.ShapeDtypeStruct((bsz, seq, D_MODEL), F32)]
    if not final:
        out_specs.append(tile(D_MODEL))
        out_shape.append(jax.ShapeDtypeStruct((bsz, seq, D_MODEL), BF16))
    return pl.pallas_call(
        functools.partial(_attn_out_kernel, final=final),
        grid=(bsz, seq // tq),
        in_specs=[
            pl.BlockSpec(memory_space=pltpu.SMEM),
            tile(ATTN_IN_COLS), prev_kv, next_kv,
            tile(D_RNN),
            tile(D_MODEL),
            mod_spec,
            _resident((ATTN_DIM, D_MODEL)),
            _resident((D_RNN, D_MODEL)),
            _resident((D_MODEL, D_MODEL)),
            _resident((1, D_MODEL)),
            mod_spec,
        ],
        out_specs=out_specs,
        out_shape=out_shape,
        scratch_shapes=[
            pltpu.VMEM((per, N_KV_HEADS * 3 * BLOCK, KV_DIM), BF16),
            pltpu.VMEM((per, N_KV_HEADS * 3 * BLOCK, KV_DIM), BF16),
            pltpu.VMEM((tq, ATTN_DIM), BF16),
        ],
        compiler_params=_params(2),
        name="attn_out",
    )(sink, attn_in, attn_in, attn_in, rnn, x, mod, w_attn, w_rnn, w_out, next_gain, next_mod)


def _trunk(x, batch_off, mods, tables, layers, final_gain):
    cos, sin = tables
    seq = x.shape[1]
    cos, sin = cos[:seq], sin[:seq]
    depth = len(layers)
    h = _prenorm(x, mods[0], layers[0]["norm_gain"], batch_off)
    for l, p in enumerate(layers):
        final = l == depth - 1
        attn_in, rx, gr = _project(h, p["w_in"], p["w_merge"], p["b_merge"], cos, sin)
        rnn = _rglru(rx, gr, p["conv_w"], p["conv_b"], p["w_gate"], p["lam"])
        next_gain = final_gain if final else layers[l + 1]["norm_gain"]
        next_mod = mods[l] if final else mods[l + 1]
        res = _attend_and_mix(p["sink"], attn_in, rnn, x, mods[l], p["w_attn"],
                              p["w_rnn"], p["w_out"], next_gain, next_mod, batch_off, final)
        x = res[0]
        h = None if final else res[1]
    return x


def kernel(x_prompt, x_sample, c_prompt, c_sample, norm_gain, w_ada, b_ada, w_in, attn_sink, conv_w, conv_b, rg_w_a, rg_b_a, rg_w_x, rg_b_x, rg_lambda, w_attn_proj, w_rnn_proj, w_merge, b_merge, w_out, final_gain):
    depth = w_in.shape[0]
    n_prompt = x_prompt.shape[0]
    cond = jnp.concatenate([c_prompt, c_sample], axis=0)
    mods = _modulation(cond, w_ada, b_ada)
    mods = mods.reshape(depth, cond.shape[0], 1, 3 * D_MODEL)
    tables = _rope_tables(max(x_prompt.shape[1], x_sample.shape[1]))

    w_gate = jnp.concatenate([rg_w_a, rg_w_x], axis=-1).astype(BF16)
    b_half = 0.5 * jnp.concatenate(
        [rg_b_a.reshape(depth, 2, N_RNN_BLOCKS, 1, RNN_BLOCK),
         rg_b_x.reshape(depth, 2, N_RNN_BLOCKS, 1, RNN_BLOCK)], axis=-1)
    b_hi = b_half.astype(BF16)
    b_lo = (b_half - b_hi.astype(F32)).astype(BF16)
    w_gate = jnp.concatenate(
        [w_gate, b_hi, b_lo,
         jnp.zeros((depth, 2, N_RNN_BLOCKS, RNN_BLOCK - 2, 2 * RNN_BLOCK), BF16)], axis=-2)
    gate_cols = jnp.zeros((w_in.shape[-1],), bool)
    gate_cols = gate_cols.at[GA_COL:GA_COL + ATTN_DIM].set(True)
    gate_cols = gate_cols.at[w_in.shape[-1] - D_RNN:].set(True)
    in_col_scale = jnp.where(gate_cols, 0.5, 1.0).astype(F32)
    layers = []
    for l in range(depth):
        layers.append(dict(
            norm_gain=norm_gain[l].reshape(1, D_MODEL),
            w_in=(w_in[l] * in_col_scale).astype(BF16),
            w_merge=(0.5 * w_merge[l]).astype(BF16),
            b_merge=0.5 * b_merge[l].reshape(1, 2 * D_MODEL),
            sink=attn_sink[l],
            conv_w=0.5 * conv_w[l],
            conv_b=0.5 * conv_b[l].reshape(1, D_RNN),
            w_gate=w_gate[l], lam=rg_lambda[l],
            w_attn=w_attn_proj[l].astype(BF16),
            w_rnn=w_rnn_proj[l].astype(BF16),
            w_out=w_out[l].astype(BF16),
        ))
    fg = final_gain.reshape(1, D_MODEL)
    y_prompt = _trunk(x_prompt, 0, mods, tables, layers, fg)
    y_sample = _trunk(x_sample, n_prompt, mods, tables, layers, fg)
    return (y_prompt, y_sample)
```

```python
import functools
import math

import jax
import jax.numpy as jnp
from jax import lax
from jax.experimental import pallas as pl
from jax.experimental.pallas import tpu as pltpu

D_MODEL = 1024
HEAD_DIM = 128
N_HEADS = 8
N_KV_HEADS = 2
GROUP = N_HEADS // N_KV_HEADS
ATTN_DIM = N_HEADS * HEAD_DIM
KV_DIM = N_KV_HEADS * HEAD_DIM
BLOCK = 128
ROPE_THETA = 10000.0
D_RNN = 3 * D_MODEL // 2
N_RNN_BLOCKS = 12
RNN_BLOCK = D_RNN // N_RNN_BLOCKS
CONV_WIDTH = 4
CONV_LEFT = 2
RGLRU_C = 8.0
EPS = 1e-6
MASK_VALUE = -1e30
LOG2E = math.log2(math.e)

SUBLANES = 8
LANES = 128
MXU_COLS = 256
VMEM_LIMIT_BYTES = 56 * 1024 * 1024

F32 = jnp.float32
BF16 = jnp.bfloat16


def _sigmoid_of_double(xh):
    return 0.5 * jnp.tanh(xh) + 0.5


def _silu_of_double(xh):
    return xh * jnp.tanh(xh) + xh


def _resident(shape):
    zeros = (0,) * len(shape)
    return pl.BlockSpec(shape, lambda *_: zeros, pipeline_mode=pl.Buffered(1))


def _params(n_axes):
    return pltpu.CompilerParams(dimension_semantics=("arbitrary",) * n_axes,
                                vmem_limit_bytes=VMEM_LIMIT_BYTES)


def _mod_kernel(c_ref, w_ref, b_ref, o_ref):
    c = c_ref[...]
    s = c * jax.nn.sigmoid(c)
    o_ref[...] = jnp.dot(s, w_ref[...], precision=lax.Precision.HIGHEST,
                         preferred_element_type=F32) + b_ref[...]


def _modulation(cond, w_ada, b_ada):
    depth = w_ada.shape[0]
    nb = cond.shape[0]
    n_col = 3 * D_MODEL // D_MODEL
    return pl.pallas_call(
        _mod_kernel,
        grid=(depth, n_col),
        in_specs=[
            pl.BlockSpec((nb, D_MODEL), lambda l, j: (0, 0)),
            pl.BlockSpec((None, D_MODEL, D_MODEL), lambda l, j: (l, 0, j)),
            pl.BlockSpec((None, 1, D_MODEL), lambda l, j: (l, 0, j)),
        ],
        out_specs=pl.BlockSpec((None, nb, D_MODEL), lambda l, j: (l, 0, j)),
        out_shape=jax.ShapeDtypeStruct((depth, nb, 3 * D_MODEL), F32),
        compiler_params=_params(2),
        name="adaln_mod",
    )(cond, w_ada, b_ada.reshape(depth, 1, 3 * D_MODEL))


QK_SCALE = HEAD_DIM ** -0.5 * LOG2E
ROPE_COLS = 4 * HEAD_DIM


def _rope_kernel(tab_ref):
    rows = tab_ref.shape[0]
    t = lax.broadcasted_iota(jnp.int32, (rows, HEAD_DIM), 0) + pl.program_id(0) * rows
    j = lax.broadcasted_iota(jnp.int32, (rows, HEAD_DIM), 1)
    half = HEAD_DIM // 2
    jj = jnp.where(j < half, j, j - half).astype(F32)
    inv_freq = jnp.exp(jj * (-2.0 / HEAD_DIM * math.log(ROPE_THETA)))
    ang = t.astype(F32) * inv_freq
    s = jnp.sin(ang)
    cos = jnp.cos(ang)
    sin = jnp.where(j < half, -s, s)
    for n, v in enumerate((cos, sin, cos * QK_SCALE, sin * QK_SCALE)):
        tab_ref[:, n * HEAD_DIM:(n + 1) * HEAD_DIM] = v


def _rope_tables(seq):
    rows = min(seq, 512)
    return pl.pallas_call(
        _rope_kernel,
        grid=(seq // rows,),
        out_specs=pl.BlockSpec((rows, ROPE_COLS), lambda i: (i, 0)),
        out_shape=jax.ShapeDtypeStruct((seq, ROPE_COLS), F32),
        compiler_params=_params(1),
        name="rope_tables",
    )()


NORM_ROWS = 32


def _adaln(x, gain, mod_ref):
    ms = jnp.mean(x * x, axis=-1, keepdims=True)
    gain_scale = gain * (1.0 + mod_ref[:, D_MODEL:2 * D_MODEL])
    return x * lax.rsqrt(ms + EPS) * gain_scale + mod_ref[:, 0:D_MODEL]


def _prenorm_kernel(x_ref, mod_ref, ng_ref, h_ref):
    gain = ng_ref[...]

    def norm_rows(r, carry):
        rows = pl.ds(pl.multiple_of(r * NORM_ROWS, NORM_ROWS), NORM_ROWS)
        h_ref[rows, :] = _adaln(x_ref[rows, :], gain, mod_ref).astype(BF16)
        return carry

    lax.fori_loop(0, x_ref.shape[0] // NORM_ROWS, norm_rows, 0, unroll=4)


def _prenorm(x, mod, norm_gain, batch_off):
    bsz, seq, _ = x.shape
    ts = _seq_tile(seq, 1024)
    tile = pl.BlockSpec((None, ts, D_MODEL), lambda b, t: (b, t, 0))
    return pl.pallas_call(
        _prenorm_kernel,
        grid=(bsz, seq // ts),
        in_specs=[tile,
                  pl.BlockSpec((None, 1, 3 * D_MODEL), lambda b, t: (b + batch_off, 0, 0)),
                  _resident((1, D_MODEL))],
        out_specs=tile,
        out_shape=jax.ShapeDtypeStruct((bsz, seq, D_MODEL), BF16),
        compiler_params=_params(2),
        name="prenorm",
    )(x, mod, norm_gain)


Q_COL = 0
K_COL = Q_COL + ATTN_DIM
V_COL = K_COL + KV_DIM
GA_COL = V_COL + KV_DIM
MG_COL = GA_COL + ATTN_DIM
ATTN_IN_COLS = MG_COL + 2 * D_MODEL


def _attn_in_views(a_ref):
    return (a_ref.at[:, Q_COL:K_COL], a_ref.at[:, K_COL:V_COL], a_ref.at[:, V_COL:GA_COL],
            a_ref.at[:, GA_COL:MG_COL], a_ref.at[:, MG_COL:ATTN_IN_COLS])


def _proj_kernel(h_ref, win_ref, wm_ref, bm_ref, rope_ref,
                 a_ref, rx_ref, gr_ref, *, c_len):
    q_ref, k_ref, v_ref, ga_ref, mg_ref = _attn_in_views(a_ref)
    ts = h_ref.shape[0]
    cos, sin, cos_q, sin_q = (rope_ref[:, n * HEAD_DIM:(n + 1) * HEAD_DIM] for n in range(4))

    def rotary(p, c, s):
        return p * c + pltpu.roll(p, HEAD_DIM // 2, axis=1) * s

    def chunk(w_ref, c0):
        return jnp.dot(h_ref[...], w_ref[:, c0:c0 + MXU_COLS], preferred_element_type=F32)

    col = 0
    for c in range(ATTN_DIM // MXU_COLS):
        p = chunk(win_ref, col + c * MXU_COLS)
        for hh in range(MXU_COLS // HEAD_DIM):
            dst = slice(c * MXU_COLS + hh * HEAD_DIM, c * MXU_COLS + (hh + 1) * HEAD_DIM)
            q_ref[:, dst] = rotary(p[:, hh * HEAD_DIM:(hh + 1) * HEAD_DIM], cos_q, sin_q).astype(BF16)
    col += ATTN_DIM
    p = chunk(win_ref, col)
    for hh in range(N_KV_HEADS):
        dst = slice(hh * HEAD_DIM, (hh + 1) * HEAD_DIM)
        k_ref[:, dst] = rotary(p[:, dst], cos, sin).astype(BF16)
    col += KV_DIM
    v_ref[...] = chunk(win_ref, col).astype(BF16)
    col += KV_DIM
    for c in range(ATTN_DIM // MXU_COLS):
        dst = slice(c * MXU_COLS, (c + 1) * MXU_COLS)
        ga_ref[:, dst] = _silu_of_double(chunk(win_ref, col + c * MXU_COLS)).astype(BF16)
    col += ATTN_DIM
    pitch = rx_ref.shape[0] * c_len // ts
    for c in range(D_RNN // MXU_COLS):
        dst = slice(c * MXU_COLS, (c + 1) * MXU_COLS)
        p = chunk(win_ref, col + c * MXU_COLS)
        for kk in range(ts // c_len):
            rx_ref[kk * pitch:kk * pitch + c_len, dst] = p[kk * c_len:(kk + 1) * c_len]
            rx_ref[kk * pitch + c_len:(kk + 1) * pitch, dst] = jnp.zeros(
                (pitch - c_len, MXU_COLS), F32)
    col += D_RNN
    for c in range(D_RNN // MXU_COLS):
        dst = slice(c * MXU_COLS, (c + 1) * MXU_COLS)
        gr_ref[:, dst] = _silu_of_double(chunk(win_ref, col + c * MXU_COLS)).astype(BF16)
    for c in range(2 * D_MODEL // MXU_COLS):
        dst = slice(c * MXU_COLS, (c + 1) * MXU_COLS)
        mg_ref[:, dst] = _sigmoid_of_double(
            chunk(wm_ref, c * MXU_COLS) + bm_ref[:, dst]).astype(BF16)


def _seq_tile(seq, want):
    return want if seq % want == 0 else seq


def _project(h, w_in, w_merge, b_merge, rope):
    bsz, seq, _ = h.shape
    ts = _seq_tile(seq, 512)
    d_in = w_in.shape[1]
    c_len, pitch = _chunking(seq)
    assert ts % c_len == 0
    rx_rows = ts // c_len * pitch

    def tile(width, rows=ts):
        return pl.BlockSpec((None, rows, width), lambda t, b: (b, t, 0))

    outs = [(ATTN_IN_COLS, BF16, seq, ts), (D_RNN, F32, N_CHUNKS * pitch, rx_rows),
            (D_RNN, BF16, seq, ts)]
    return pl.pallas_call(
        functools.partial(_proj_kernel, c_len=c_len),
        grid=(seq // ts, bsz),
        in_specs=[
            tile(D_MODEL),
            _resident((D_MODEL, d_in)),
            _resident((D_MODEL, 2 * D_MODEL)),
            _resident((1, 2 * D_MODEL)),
            pl.BlockSpec((ts, ROPE_COLS), lambda t, b: (t, 0)),
        ],
        out_specs=[tile(w, rows) for w, _, _, rows in outs],
        out_shape=[jax.ShapeDtypeStruct((bsz, total, w), dt) for w, dt, total, _ in outs],
        compiler_params=_params(2),
        name="in_proj",
    )(h, w_in, w_merge, b_merge, rope)


N_SETS = 8
N_CHUNKS = N_SETS * SUBLANES
STEP_ROWS = N_SETS * SUBLANES
PITCH_PAD = SUBLANES
EXT_SLOTS = CONV_WIDTH - 1
PHASE_ROWS = 256
SCAN_STEPS = 4
GATHER_STEPS = PHASE_ROWS // STEP_ROWS
TINY = 1e-30


def _chunking(seq):
    c_len = seq // N_CHUNKS
    assert seq % N_CHUNKS == 0 and (c_len * STEP_ROWS) % PHASE_ROWS == 0
    assert c_len % SCAN_STEPS == 0 and c_len % GATHER_STEPS == 0
    assert c_len % (2 * SUBLANES) == 0
    return c_len, c_len + PITCH_PAD


def _sublane_scan(a, b, reverse):
    row = lax.broadcasted_iota(jnp.int32, a.shape, 0)
    for d in (1, 2, 4):
        if reverse:
            keep, shift = row < SUBLANES - d, SUBLANES - d
        else:
            keep, shift = row >= d, d
        a_s = pltpu.roll(a, shift, axis=0)
        b_s = pltpu.roll(b, shift, axis=0)
        b = jnp.where(keep, a * b_s + b, b)
        a = jnp.where(keep, a * a_s, a)
    return a, b


def _rglru_kernel(rx_ref, gr_ref, cw_ref, cb_ref, wg_ref, lam_ref, out_ref,
                  xs, ext, a_scr, b_scr, hs):
    seq = out_ref.shape[0]
    c_len, pitch = _chunking(seq)
    row8 = lax.broadcasted_iota(jnp.int32, (SUBLANES, LANES), 0)
    zero8 = jnp.zeros((SUBLANES, LANES), F32)

    def sets(v):
        return [v[s * SUBLANES:(s + 1) * SUBLANES] for s in range(N_SETS)]

    def slot(i):
        return slice((i + CONV_LEFT) * STEP_ROWS, (i + CONV_LEFT + 1) * STEP_ROWS)

    n_gather = c_len // GATHER_STEPS

    def gather(g):
        i0 = g * GATHER_STEPS
        vals = []
        for kk in range(GATHER_STEPS):
            for s in range(N_SETS):
                vals.append(rx_ref[pl.ds(s * SUBLANES * pitch + i0 + kk, SUBLANES, stride=pitch), :])
        dst = pl.multiple_of((i0 + CONV_LEFT) * STEP_ROWS, STEP_ROWS)
        ext[pl.ds(dst, GATHER_STEPS * STEP_ROWS), :] = jnp.concatenate(vals, axis=0)

    for g in sorted({0, min(1, n_gather - 1), n_gather - 1}):
        gather(g)

    for back in (1, 2):
        tail = [pltpu.roll(v, 1, axis=0) for v in sets(ext[slot(c_len - back), :])]
        ext[slot(-back), :] = jnp.concatenate(
            [jnp.where(row8 == 0, tail[s - 1] if s > 0 else zero8, tail[s])
             for s in range(N_SETS)], axis=0)
    head = [pltpu.roll(v, SUBLANES - 1, axis=0) for v in sets(ext[slot(0), :])]
    ext[slot(c_len), :] = jnp.concatenate(
        [jnp.where(row8 == SUBLANES - 1, head[s + 1] if s < N_SETS - 1 else zero8, head[s])
         for s in range(N_SETS)], axis=0)

    neg = -lam_ref[...]
    softplus = jnp.maximum(neg, 0.0) + jnp.log(1.0 + jnp.exp(-jnp.abs(neg)))
    kexp = softplus * (-RGLRU_C * 0.5 * math.log2(math.e))
    cw = cw_ref[...]
    cb = cb_ref[...]
    ones16 = jnp.ones((PHASE_ROWS, RNN_BLOCK), BF16)

    def order(d):
        return range(SCAN_STEPS - 1, -1, -1) if d else range(SCAN_STEPS)

    def piece(v, kk, s):
        r0 = (kk * N_SETS + s) * SUBLANES
        return v[r0:r0 + SUBLANES]

    def advance(d, av, bv, carry):
        carry = list(carry)
        for kk in order(d):
            for s in range(N_SETS):
                a = piece(av, kk, s)
                h, p = carry[s]
                carry[s] = (a * h + piece(bv, kk, s), a * p)
        return tuple(carry)

    def phase(n, carry):
        gather(jnp.minimum(n + 2, n_gather - 1))
        src = n * PHASE_ROWS
        dst = pl.ds(pl.multiple_of(src, PHASE_ROWS), PHASE_ROWS)
        y = cb
        for tap in range(CONV_WIDTH):
            rows = pl.ds(pl.multiple_of(src + tap * STEP_ROWS, STEP_ROWS), PHASE_ROWS)
            y = y + ext[rows, :] * cw[tap:tap + 1, :]
        lhs = jnp.concatenate([y.astype(BF16), ones16], axis=1)
        for d in range(2):
            g = jnp.dot(lhs, wg_ref[d], preferred_element_type=F32)
            ta = jnp.tanh(g[:, :RNN_BLOCK])
            ti = jnp.tanh(g[:, RNN_BLOCK:])
            k = kexp[d:d + 1, :]
            a = jnp.exp2(k * ta + k)
            z = 1.0 - a * a
            m = z * lax.rsqrt(jnp.maximum(z, TINY))
            b = m * ((ti + 1.0) * y)
            a_scr[d, dst, :] = a
            b_scr[d, dst, :] = b
            if d == 0:
                carry = advance(0, a, b, carry)
        return carry

    ones8 = jnp.ones((SUBLANES, LANES), F32)
    fresh = tuple((zero8, ones8) for _ in range(N_SETS))
    assert PHASE_ROWS == SCAN_STEPS * STEP_ROWS
    ends_f = lax.fori_loop(0, seq // PHASE_ROWS, phase, fresh, unroll=32)

    n_blocks = c_len // SCAN_STEPS
    block_rows = SCAN_STEPS * STEP_ROWS

    def block(d, n):
        blk = (n_blocks - 1 - n) if d else n
        return blk, pl.ds(pl.multiple_of(blk * block_rows, block_rows), block_rows)

    def summarise(d, n, carry):
        _, rows = block(d, n)
        return advance(d, a_scr[d, rows, :], b_scr[d, rows, :], carry)

    def resolve(d, ends):
        edge = zero8
        start = [None] * N_SETS
        for s in (range(N_SETS - 1, -1, -1) if d else range(N_SETS)):
            p_cum, h_cum = _sublane_scan(ends[s][1], ends[s][0], bool(d))
            after = p_cum * edge + h_cum
            if d:
                start[s] = jnp.where(row8 == SUBLANES - 1, edge,
                                     pltpu.roll(after, SUBLANES - 1, axis=0))
                edge = jnp.broadcast_to(after[0:1, :], (SUBLANES, LANES))
            else:
                start[s] = jnp.where(row8 == 0, edge, pltpu.roll(after, 1, axis=0))
                edge = jnp.broadcast_to(after[SUBLANES - 1:SUBLANES, :], (SUBLANES, LANES))
        return tuple(start)

    def rescan(d, n, carry):
        blk, rows = block(d, n)
        av = a_scr[d, rows, :]
        bv = b_scr[d, rows, :]
        hv = hs[rows, :] if d else None
        carry = list(carry)
        out = [None] * (SCAN_STEPS * N_SETS)
        for kk in order(d):
            for s in range(N_SETS):
                h = piece(av, kk, s) * carry[s] + piece(bv, kk, s)
                carry[s] = h
                if d:
                    i = blk * SCAN_STEPS + kk
                    xs[pl.ds(s * SUBLANES * pitch + i, SUBLANES, stride=pitch), :] = (
                        h + piece(hv, kk, s))
                else:
                    out[kk * N_SETS + s] = h
        if not d:
            hs[rows, :] = jnp.concatenate(out, axis=0)
        return tuple(carry)

    _, ends_b = lax.fori_loop(
        0, n_blocks, lambda n, c: (rescan(0, n, c[0]), summarise(1, n, c[1])),
        (resolve(0, ends_f), fresh))
    lax.fori_loop(0, n_blocks, functools.partial(rescan, 1), resolve(1, ends_b))

    for ch in range(N_CHUNKS):
        rows = slice(ch * c_len, (ch + 1) * c_len)
        h = xs[ch * pitch:ch * pitch + c_len, :]
        out_ref[rows, :] = (h * gr_ref[rows, :].astype(F32)).astype(BF16)


def _rglru(rnn_x, gate, conv_w, conv_b, w_gate, lam):
    bsz, seq, _ = gate.shape
    c_len, pitch = _chunking(seq)
    assert rnn_x.shape[1] == N_CHUNKS * pitch

    def col_block(rows):
        return pl.BlockSpec((None, rows, RNN_BLOCK), lambda c, b: (b, 0, c))

    return pl.pallas_call(
        _rglru_kernel,
        grid=(N_RNN_BLOCKS, bsz),
        in_specs=[
            col_block(N_CHUNKS * pitch),
            col_block(seq),
            pl.BlockSpec((CONV_WIDTH, RNN_BLOCK), lambda c, b: (0, c)),
            pl.BlockSpec((1, RNN_BLOCK), lambda c, b: (0, c)),
            pl.BlockSpec((2, None, 2 * RNN_BLOCK, 2 * RNN_BLOCK), lambda c, b: (0, c, 0, 0)),
            pl.BlockSpec((2, RNN_BLOCK), lambda c, b: (0, c)),
        ],
        out_specs=col_block(seq),
        out_shape=jax.ShapeDtypeStruct((bsz, seq, D_RNN), BF16),
        scratch_shapes=[
            pltpu.VMEM((N_CHUNKS * pitch, RNN_BLOCK), F32),
            pltpu.VMEM(((c_len + EXT_SLOTS) * STEP_ROWS, RNN_BLOCK), F32),
            pltpu.VMEM((2, seq, RNN_BLOCK), F32),
            pltpu.VMEM((2, seq, RNN_BLOCK), F32),
            pltpu.VMEM((seq, RNN_BLOCK), F32),
        ],
        compiler_params=_params(2),
        name="rglru",
    )(rnn_x, gate, conv_w, conv_b, w_gate, lam)


MIX_BLOCKS = 4

def _attn_out_kernel(sink_ref, a_ref, kvp_ref, kvn_ref, rnn_ref, x_ref, mod_ref,
                     wa_ref, wr_ref, wo_ref, ng_ref, nmod_ref, *rest, final):
    q_ref, k_ref, v_ref, ga_ref, mg_ref = _attn_in_views(a_ref)
    kp_ref, vp_ref = kvp_ref.at[:, 0:KV_DIM], kvp_ref.at[:, KV_DIM:2 * KV_DIM]
    kn_ref, vn_ref = kvn_ref.at[:, 0:KV_DIM], kvn_ref.at[:, KV_DIM:2 * KV_DIM]
    if final:
        o_ref, kbd, vbd, attn_scr = rest
    else:
        o_ref, h_ref, kbd, vbd, attn_scr = rest
    tq = q_ref.shape[0]
    t = pl.program_id(1)
    nt = pl.num_programs(1)
    n_blk = tq // BLOCK

    def band_piece(tile_ref, prev_ref, next_ref, j, cols):
        if j < 0:
            return prev_ref[:, cols]
        if j >= n_blk:
            return next_ref[:, cols]
        return tile_ref[j * BLOCK:(j + 1) * BLOCK, cols]

    band = 3 * BLOCK
    qi = lax.broadcasted_iota(jnp.int32, (BLOCK, band), 0)
    kj = lax.broadcasted_iota(jnp.int32, (BLOCK, band), 1)
    in_window = (kj >= qi) & (kj <= qi + 2 * BLOCK)
    head = lax.broadcasted_iota(jnp.int32, (GROUP, BLOCK, 1), 0)

    for blk in range(n_blk):
        rows = slice(blk * BLOCK, (blk + 1) * BLOCK)
        valid = in_window
        if blk == 0:
            valid = valid & (kj >= jnp.where(t == 0, BLOCK, 0))
        if blk == n_blk - 1:
            valid = valid & (kj < jnp.where(t == nt - 1, 2 * BLOCK, band))
        zeros = jnp.zeros((BLOCK, HEAD_DIM), BF16)
        for g in range(N_KV_HEADS):
            for src, dst in (((k_ref, kp_ref, kn_ref), kbd), ((v_ref, vp_ref, vn_ref), vbd)):
                for gg in range(N_KV_HEADS):
                    cols = slice(gg * HEAD_DIM, (gg + 1) * HEAD_DIM)
                    for j in range(3):
                        r0 = g * band + j * BLOCK
                        dst[blk, r0:r0 + BLOCK, cols] = (
                            band_piece(*src, blk - 1 + j, cols) if gg == g else zeros)
        qcat = jnp.concatenate(
            [jnp.concatenate(
                [q_ref[rows, (g * GROUP + h) * HEAD_DIM:(g * GROUP + h + 1) * HEAD_DIM]
                 for g in range(N_KV_HEADS)], axis=1) for h in range(GROUP)], axis=0)
        s_all = lax.dot_general(qcat, kbd[blk], (((1,), (1,)), ((), ())),
                                preferred_element_type=F32)
        probs, denoms = [], []
        for g in range(N_KV_HEADS):
            s = s_all[:, g * band:(g + 1) * band].reshape(GROUP, BLOCK, band)
            s = jnp.where(valid[None], s, MASK_VALUE)
            sink = jnp.full((GROUP, BLOCK, 1), sink_ref[g * GROUP] * LOG2E, F32)
            for h in range(1, GROUP):
                sink = jnp.where(head == h, sink_ref[g * GROUP + h] * LOG2E, sink)
            m = jnp.maximum(jnp.max(s, axis=-1, keepdims=True), sink)
            p = jnp.exp2(s - m)
            denoms.append(jnp.sum(p, axis=-1, keepdims=True) + jnp.exp2(sink - m))
            probs.append(p.reshape(GROUP * BLOCK, band).astype(BF16))
        o_all = jnp.dot(jnp.concatenate(probs, axis=1), vbd[blk],
                        preferred_element_type=F32)
        for g in range(N_KV_HEADS):
            o = o_all[:, g * HEAD_DIM:(g + 1) * HEAD_DIM].reshape(GROUP, BLOCK, HEAD_DIM)
            o = o / denoms[g]
            for h in range(GROUP):
                cols = slice((g * GROUP + h) * HEAD_DIM, (g * GROUP + h + 1) * HEAD_DIM)
                attn_scr[rows, cols] = (o[h] * ga_ref[rows, cols].astype(F32)).astype(BF16)

        if (blk + 1) % MIX_BLOCKS == 0 or blk == n_blk - 1:
            lo_row = (blk // MIX_BLOCKS) * MIX_BLOCKS * BLOCK
            rows = slice(lo_row, (blk + 1) * BLOCK)
            a = jnp.dot(attn_scr[rows, :], wa_ref[...], preferred_element_type=F32)
            r = jnp.dot(rnn_ref[rows, :], wr_ref[...], preferred_element_type=F32)
            mixed = (mg_ref[rows, 0:D_MODEL].astype(F32) * a
                     + mg_ref[rows, D_MODEL:].astype(F32) * r)
            y = jnp.dot(mixed.astype(BF16), wo_ref[...], preferred_element_type=F32)
            out = x_ref[rows, :] + mod_ref[:, 2 * D_MODEL:] * y
            if final:
                ms = jnp.mean(out * out, axis=-1, keepdims=True)
                o_ref[rows, :] = out * lax.rsqrt(ms + EPS) * ng_ref[...]
            else:
                o_ref[rows, :] = out
                h_ref[rows, :] = _adaln(out, ng_ref[...], nmod_ref).astype(BF16)


def _attend_and_mix(sink, attn_in, rnn, x, mod, w_attn, w_rnn, w_out, next_gain,
                    next_mod, batch_off, final):
    bsz, seq, _ = x.shape
    tq = _seq_tile(seq, 512)
    per = tq // BLOCK
    n_blocks = seq // BLOCK

    def tile(width):
        return pl.BlockSpec((None, tq, width), lambda b, t: (b, t, 0))

    kv_blk = K_COL // (2 * KV_DIM)
    assert K_COL % (2 * KV_DIM) == 0 and V_COL == K_COL + KV_DIM
    prev_kv = pl.BlockSpec((None, BLOCK, 2 * KV_DIM),
                           lambda b, t: (b, jnp.maximum(t * per - 1, 0), kv_blk))
    next_kv = pl.BlockSpec((None, BLOCK, 2 * KV_DIM),
                           lambda b, t: (b, jnp.minimum((t + 1) * per, n_blocks - 1), kv_blk))
    mod_spec = pl.BlockSpec((None, 1, 3 * D_MODEL), lambda b, t: (b + batch_off, 0, 0))
    out_specs = [tile(D_MODEL)]
    out_shape = [jax.ShapeDtypeStruct((bsz, seq, D_MODEL), F32)]
    if not final:
        out_specs.append(tile(D_MODEL))
        out_shape.append(jax.ShapeDtypeStruct((bsz, seq, D_MODEL), BF16))
    return pl.pallas_call(
        functools.partial(_attn_out_kernel, final=final),
        grid=(bsz, seq // tq),
        in_specs=[
            pl.BlockSpec(memory_space=pltpu.SMEM),
            tile(ATTN_IN_COLS), prev_kv, next_kv,
            tile(D_RNN),
            tile(D_MODEL),
            mod_spec,
            _resident((ATTN_DIM, D_MODEL)),
            _resident((D_RNN, D_MODEL)),
            _resident((D_MODEL, D_MODEL)),
            _resident((1, D_MODEL)),
            mod_spec,
        ],
        out_specs=out_specs,
        out_shape=out_shape,
        scratch_shapes=[
            pltpu.VMEM((per, N_KV_HEADS * 3 * BLOCK, KV_DIM), BF16),
            pltpu.VMEM((per, N_KV_HEADS * 3 * BLOCK, KV_DIM), BF16),
            pltpu.VMEM((tq, ATTN_DIM), BF16),
        ],
        compiler_params=_params(2),
        name="attn_out",
    )(sink, attn_in, attn_in, attn_in, rnn, x, mod, w_attn, w_rnn, w_out, next_gain, next_mod)


def _trunk(x, batch_off, mods, tables, layers, final_gain):
    seq = x.shape[1]
    rope = tables[:seq]
    depth = len(layers)
    h = _prenorm(x, mods[0], layers[0]["norm_gain"], batch_off)
    for l, p in enumerate(layers):
        final = l == depth - 1
        attn_in, rx, gr = _project(h, p["w_in"], p["w_merge"], p["b_merge"], rope)
        rnn = _rglru(rx, gr, p["conv_w"], p["conv_b"], p["w_gate"], p["lam"])
        next_gain = final_gain if final else layers[l + 1]["norm_gain"]
        next_mod = mods[l] if final else mods[l + 1]
        res = _attend_and_mix(p["sink"], attn_in, rnn, x, mods[l], p["w_attn"],
                              p["w_rnn"], p["w_out"], next_gain, next_mod, batch_off, final)
        x = res[0]
        h = None if final else res[1]
    return x


def kernel(x_prompt, x_sample, c_prompt, c_sample, norm_gain, w_ada, b_ada, w_in, attn_sink, conv_w, conv_b, rg_w_a, rg_b_a, rg_w_x, rg_b_x, rg_lambda, w_attn_proj, w_rnn_proj, w_merge, b_merge, w_out, final_gain):
    depth = w_in.shape[0]
    n_prompt = x_prompt.shape[0]
    cond = jnp.concatenate([c_prompt, c_sample], axis=0)
    mods = _modulation(cond, w_ada, b_ada)
    mods = mods.reshape(depth, cond.shape[0], 1, 3 * D_MODEL)
    tables = _rope_tables(max(x_prompt.shape[1], x_sample.shape[1]))

    w_gate = jnp.concatenate([rg_w_a, rg_w_x], axis=-1).astype(BF16)
    b_half = 0.5 * jnp.concatenate(
        [rg_b_a.reshape(depth, 2, N_RNN_BLOCKS, 1, RNN_BLOCK),
         rg_b_x.reshape(depth, 2, N_RNN_BLOCKS, 1, RNN_BLOCK)], axis=-1)
    b_hi = b_half.astype(BF16)
    b_lo = (b_half - b_hi.astype(F32)).astype(BF16)
    w_gate = jnp.concatenate(
        [w_gate, b_hi, b_lo,
         jnp.zeros((depth, 2, N_RNN_BLOCKS, RNN_BLOCK - 2, 2 * RNN_BLOCK), BF16)], axis=-2)
    gate_cols = jnp.zeros((w_in.shape[-1],), bool)
    gate_cols = gate_cols.at[GA_COL:GA_COL + ATTN_DIM].set(True)
    gate_cols = gate_cols.at[w_in.shape[-1] - D_RNN:].set(True)
    in_col_scale = jnp.where(gate_cols, 0.5, 1.0).astype(F32)
    layers = []
    for l in range(depth):
        layers.append(dict(
            norm_gain=norm_gain[l].reshape(1, D_MODEL),
            w_in=(w_in[l] * in_col_scale).astype(BF16),
            w_merge=(0.5 * w_merge[l]).astype(BF16),
            b_merge=0.5 * b_merge[l].reshape(1, 2 * D_MODEL),
            sink=attn_sink[l],
            conv_w=0.5 * conv_w[l],
            conv_b=0.5 * conv_b[l].reshape(1, D_RNN),
            w_gate=w_gate[l], lam=rg_lambda[l],
            w_attn=w_attn_proj[l].astype(BF16),
            w_rnn=w_rnn_proj[l].astype(BF16),
            w_out=w_out[l].astype(BF16),
        ))
    fg = final_gain.reshape(1, D_MODEL)
    y_prompt = _trunk(x_prompt, 0, mods, tables, layers, fg)
    y_sample = _trunk(x_sample, n_prompt, mods, tables, layers, fg)
    return (y_prompt, y_sample)
```

```python
import functools
import math

import jax
import jax.numpy as jnp
from jax import lax
from jax.experimental import pallas as pl
from jax.experimental.pallas import tpu as pltpu

D_MODEL = 1024
HEAD_DIM = 128
N_HEADS = 8
N_KV_HEADS = 2
GROUP = N_HEADS // N_KV_HEADS
ATTN_DIM = N_HEADS * HEAD_DIM
KV_DIM = N_KV_HEADS * HEAD_DIM
BLOCK = 128
ROPE_THETA = 10000.0
D_RNN = 3 * D_MODEL // 2
N_RNN_BLOCKS = 12
RNN_BLOCK = D_RNN // N_RNN_BLOCKS
CONV_WIDTH = 4
CONV_LEFT = 2
RGLRU_C = 8.0
EPS = 1e-6
MASK_VALUE = -1e30
LOG2E = math.log2(math.e)

SUBLANES = 8
LANES = 128
MXU_COLS = 256
VMEM_LIMIT_BYTES = 56 * 1024 * 1024

F32 = jnp.float32
BF16 = jnp.bfloat16


def _sigmoid_of_double(xh):
    return 0.5 * jnp.tanh(xh) + 0.5


def _silu_of_double(xh):
    return xh * jnp.tanh(xh) + xh


def _resident(shape):
    zeros = (0,) * len(shape)
    return pl.BlockSpec(shape, lambda *_: zeros, pipeline_mode=pl.Buffered(1))


def _params(n_axes):
    return pltpu.CompilerParams(dimension_semantics=("arbitrary",) * n_axes,
                                vmem_limit_bytes=VMEM_LIMIT_BYTES)


def _mod_kernel(c_ref, w_ref, b_ref, o_ref):
    c = c_ref[...]
    s = c * jax.nn.sigmoid(c)
    o_ref[...] = jnp.dot(s, w_ref[...], precision=lax.Precision.HIGHEST,
                         preferred_element_type=F32) + b_ref[...]


def _modulation(cond, w_ada, b_ada):
    depth = w_ada.shape[0]
    nb = cond.shape[0]
    n_col = 3 * D_MODEL // D_MODEL
    return pl.pallas_call(
        _mod_kernel,
        grid=(depth, n_col),
        in_specs=[
            pl.BlockSpec((nb, D_MODEL), lambda l, j: (0, 0)),
            pl.BlockSpec((None, D_MODEL, D_MODEL), lambda l, j: (l, 0, j)),
            pl.BlockSpec((None, 1, D_MODEL), lambda l, j: (l, 0, j)),
        ],
        out_specs=pl.BlockSpec((None, nb, D_MODEL), lambda l, j: (l, 0, j)),
        out_shape=jax.ShapeDtypeStruct((depth, nb, 3 * D_MODEL), F32),
        compiler_params=_params(2),
        name="adaln_mod",
    )(cond, w_ada, b_ada.reshape(depth, 1, 3 * D_MODEL))


QK_SCALE = HEAD_DIM ** -0.5 * LOG2E
ROPE_COLS = 4 * HEAD_DIM


def _rope_kernel(tab_ref):
    rows = tab_ref.shape[0]
    t = lax.broadcasted_iota(jnp.int32, (rows, HEAD_DIM), 0) + pl.program_id(0) * rows
    j = lax.broadcasted_iota(jnp.int32, (rows, HEAD_DIM), 1)
    half = HEAD_DIM // 2
    jj = jnp.where(j < half, j, j - half).astype(F32)
    inv_freq = jnp.exp(jj * (-2.0 / HEAD_DIM * math.log(ROPE_THETA)))
    ang = t.astype(F32) * inv_freq
    s = jnp.sin(ang)
    cos = jnp.cos(ang)
    sin = jnp.where(j < half, -s, s)
    for n, v in enumerate((cos, sin, cos * QK_SCALE, sin * QK_SCALE)):
        tab_ref[:, n * HEAD_DIM:(n + 1) * HEAD_DIM] = v


def _rope_tables(seq):
    rows = min(seq, 512)
    return pl.pallas_call(
        _rope_kernel,
        grid=(seq // rows,),
        out_specs=pl.BlockSpec((rows, ROPE_COLS), lambda i: (i, 0)),
        out_shape=jax.ShapeDtypeStruct((seq, ROPE_COLS), F32),
        compiler_params=_params(1),
        name="rope_tables",
    )()


NORM_ROWS = 32


def _adaln(x, gain, mod_ref):
    ms = jnp.mean(x * x, axis=-1, keepdims=True)
    gain_scale = gain * (1.0 + mod_ref[:, D_MODEL:2 * D_MODEL])
    return x * lax.rsqrt(ms + EPS) * gain_scale + mod_ref[:, 0:D_MODEL]


def _prenorm_kernel(x_ref, mod_ref, ng_ref, h_ref):
    gain = ng_ref[...]

    def norm_rows(r, carry):
        rows = pl.ds(pl.multiple_of(r * NORM_ROWS, NORM_ROWS), NORM_ROWS)
        h_ref[rows, :] = _adaln(x_ref[rows, :], gain, mod_ref).astype(BF16)
        return carry

    lax.fori_loop(0, x_ref.shape[0] // NORM_ROWS, norm_rows, 0, unroll=4)


def _prenorm(x, mod, norm_gain, batch_off):
    bsz, seq, _ = x.shape
    ts = _seq_tile(seq, 1024)
    tile = pl.BlockSpec((None, ts, D_MODEL), lambda b, t: (b, t, 0))
    return pl.pallas_call(
        _prenorm_kernel,
        grid=(bsz, seq // ts),
        in_specs=[tile,
                  pl.BlockSpec((None, 1, 3 * D_MODEL), lambda b, t: (b + batch_off, 0, 0)),
                  _resident((1, D_MODEL))],
        out_specs=tile,
        out_shape=jax.ShapeDtypeStruct((bsz, seq, D_MODEL), BF16),
        compiler_params=_params(2),
        name="prenorm",
    )(x, mod, norm_gain)


Q_COL = 0
K_COL = Q_COL + ATTN_DIM
V_COL = K_COL + KV_DIM
GA_COL = V_COL + KV_DIM
MG_COL = GA_COL + ATTN_DIM
ATTN_IN_COLS = MG_COL + 2 * D_MODEL


def _attn_in_views(a_ref):
    return (a_ref.at[:, Q_COL:K_COL], a_ref.at[:, K_COL:V_COL], a_ref.at[:, V_COL:GA_COL],
            a_ref.at[:, GA_COL:MG_COL], a_ref.at[:, MG_COL:ATTN_IN_COLS])


def _proj_kernel(h_ref, win_ref, wm_ref, bm_ref, rope_ref,
                 a_ref, rx_ref, gr_ref, *, c_len):
    q_ref, k_ref, v_ref, ga_ref, mg_ref = _attn_in_views(a_ref)
    ts = h_ref.shape[0]
    cos, sin, cos_q, sin_q = (rope_ref[:, n * HEAD_DIM:(n + 1) * HEAD_DIM] for n in range(4))

    def rotary(p, c, s):
        return p * c + pltpu.roll(p, HEAD_DIM // 2, axis=1) * s

    def chunk(w_ref, c0):
        return jnp.dot(h_ref[...], w_ref[:, c0:c0 + MXU_COLS], preferred_element_type=F32)

    col = 0
    for c in range(ATTN_DIM // MXU_COLS):
        p = chunk(win_ref, col + c * MXU_COLS)
        for hh in range(MXU_COLS // HEAD_DIM):
            dst = slice(c * MXU_COLS + hh * HEAD_DIM, c * MXU_COLS + (hh + 1) * HEAD_DIM)
            q_ref[:, dst] = rotary(p[:, hh * HEAD_DIM:(hh + 1) * HEAD_DIM], cos_q, sin_q).astype(BF16)
    col += ATTN_DIM
    p = chunk(win_ref, col)
    for hh in range(N_KV_HEADS):
        dst = slice(hh * HEAD_DIM, (hh + 1) * HEAD_DIM)
        k_ref[:, dst] = rotary(p[:, dst], cos, sin).astype(BF16)
    col += KV_DIM
    v_ref[...] = chunk(win_ref, col).astype(BF16)
    col += KV_DIM
    for c in range(ATTN_DIM // MXU_COLS):
        dst = slice(c * MXU_COLS, (c + 1) * MXU_COLS)
        ga_ref[:, dst] = _silu_of_double(chunk(win_ref, col + c * MXU_COLS)).astype(BF16)
    col += ATTN_DIM
    pitch = rx_ref.shape[0] * c_len // ts
    for c in range(D_RNN // MXU_COLS):
        dst = slice(c * MXU_COLS, (c + 1) * MXU_COLS)
        p = chunk(win_ref, col + c * MXU_COLS)
        for kk in range(ts // c_len):
            rx_ref[kk * pitch:kk * pitch + c_len, dst] = p[kk * c_len:(kk + 1) * c_len]
            rx_ref[kk * pitch + c_len:(kk + 1) * pitch, dst] = jnp.zeros(
                (pitch - c_len, MXU_COLS), F32)
    col += D_RNN
    for c in range(D_RNN // MXU_COLS):
        dst = slice(c * MXU_COLS, (c + 1) * MXU_COLS)
        gr_ref[:, dst] = _silu_of_double(chunk(win_ref, col + c * MXU_COLS)).astype(BF16)
    for c in range(2 * D_MODEL // MXU_COLS):
        dst = slice(c * MXU_COLS, (c + 1) * MXU_COLS)
        mg_ref[:, dst] = _sigmoid_of_double(
            chunk(wm_ref, c * MXU_COLS) + bm_ref[:, dst]).astype(BF16)


def _seq_tile(seq, want):
    return want if seq % want == 0 else seq


def _project(h, w_in, w_merge, b_merge, rope):
    bsz, seq, _ = h.shape
    ts = _seq_tile(seq, 512)
    d_in = w_in.shape[1]
    c_len, pitch = _chunking(seq)
    assert ts % c_len == 0
    rx_rows = ts // c_len * pitch

    def tile(width, rows=ts):
        return pl.BlockSpec((None, rows, width), lambda t, b: (b, t, 0))

    outs = [(ATTN_IN_COLS, BF16, seq, ts), (D_RNN, F32, N_CHUNKS * pitch, rx_rows),
            (D_RNN, BF16, seq, ts)]
    return pl.pallas_call(
        functools.partial(_proj_kernel, c_len=c_len),
        grid=(seq // ts, bsz),
        in_specs=[
            tile(D_MODEL),
            _resident((D_MODEL, d_in)),
            _resident((D_MODEL, 2 * D_MODEL)),
            _resident((1, 2 * D_MODEL)),
            pl.BlockSpec((ts, ROPE_COLS), lambda t, b: (t, 0)),
        ],
        out_specs=[tile(w, rows) for w, _, _, rows in outs],
        out_shape=[jax.ShapeDtypeStruct((bsz, total, w), dt) for w, dt, total, _ in outs],
        compiler_params=_params(2),
        name="in_proj",
    )(h, w_in, w_merge, b_merge, rope)


N_SETS = 4
N_CHUNKS = N_SETS * SUBLANES
STEP_ROWS = N_SETS * SUBLANES
PITCH_PAD = SUBLANES
EXT_SLOTS = CONV_WIDTH - 1
PHASE_ROWS = 256
SCAN_STEPS = 8
GATHER_STEPS = PHASE_ROWS // STEP_ROWS
TINY = 1e-30


def _chunking(seq):
    c_len = seq // N_CHUNKS
    assert seq % N_CHUNKS == 0 and (c_len * STEP_ROWS) % PHASE_ROWS == 0
    assert c_len % SCAN_STEPS == 0 and c_len % GATHER_STEPS == 0
    assert c_len % (2 * SUBLANES) == 0
    return c_len, c_len + PITCH_PAD


def _sublane_scan(a, b, reverse):
    row = lax.broadcasted_iota(jnp.int32, a.shape, 0)
    for d in (1, 2, 4):
        if reverse:
            keep, shift = row < SUBLANES - d, SUBLANES - d
        else:
            keep, shift = row >= d, d
        a_s = pltpu.roll(a, shift, axis=0)
        b_s = pltpu.roll(b, shift, axis=0)
        b = jnp.where(keep, a * b_s + b, b)
        a = jnp.where(keep, a * a_s, a)
    return a, b


def _rglru_kernel(rx_ref, gr_ref, cw_ref, cb_ref, wg_ref, lam_ref, out_ref,
                  xs, ext, a_scr, b_scr, hs):
    seq = out_ref.shape[0]
    c_len, pitch = _chunking(seq)
    row8 = lax.broadcasted_iota(jnp.int32, (SUBLANES, LANES), 0)
    zero8 = jnp.zeros((SUBLANES, LANES), F32)

    def sets(v):
        return [v[s * SUBLANES:(s + 1) * SUBLANES] for s in range(N_SETS)]

    def slot(i):
        return slice((i + CONV_LEFT) * STEP_ROWS, (i + CONV_LEFT + 1) * STEP_ROWS)

    n_gather = c_len // GATHER_STEPS

    def gather(g):
        i0 = g * GATHER_STEPS
        vals = []
        for kk in range(GATHER_STEPS):
            for s in range(N_SETS):
                vals.append(rx_ref[pl.ds(s * SUBLANES * pitch + i0 + kk, SUBLANES, stride=pitch), :])
        dst = pl.multiple_of((i0 + CONV_LEFT) * STEP_ROWS, STEP_ROWS)
        ext[pl.ds(dst, GATHER_STEPS * STEP_ROWS), :] = jnp.concatenate(vals, axis=0)

    for g in sorted({0, min(1, n_gather - 1), n_gather - 1}):
        gather(g)

    for back in (1, 2):
        tail = [pltpu.roll(v, 1, axis=0) for v in sets(ext[slot(c_len - back), :])]
        ext[slot(-back), :] = jnp.concatenate(
            [jnp.where(row8 == 0, tail[s - 1] if s > 0 else zero8, tail[s])
             for s in range(N_SETS)], axis=0)
    head = [pltpu.roll(v, SUBLANES - 1, axis=0) for v in sets(ext[slot(0), :])]
    ext[slot(c_len), :] = jnp.concatenate(
        [jnp.where(row8 == SUBLANES - 1, head[s + 1] if s < N_SETS - 1 else zero8, head[s])
         for s in range(N_SETS)], axis=0)

    neg = -lam_ref[...]
    softplus = jnp.maximum(neg, 0.0) + jnp.log(1.0 + jnp.exp(-jnp.abs(neg)))
    kexp = softplus * (-RGLRU_C * 0.5 * math.log2(math.e))
    cw = cw_ref[...]
    cb = cb_ref[...]
    ones16 = jnp.ones((PHASE_ROWS, RNN_BLOCK), BF16)

    def order(d):
        return range(SCAN_STEPS - 1, -1, -1) if d else range(SCAN_STEPS)

    def piece(v, kk, s):
        r0 = (kk * N_SETS + s) * SUBLANES
        return v[r0:r0 + SUBLANES]

    def advance(d, av, bv, carry):
        carry = list(carry)
        for kk in order(d):
            for s in range(N_SETS):
                a = piece(av, kk, s)
                h, p = carry[s]
                carry[s] = (a * h + piece(bv, kk, s), a * p)
        return tuple(carry)

    def phase(n, carry):
        gather(jnp.minimum(n + 2, n_gather - 1))
        src = n * PHASE_ROWS
        dst = pl.ds(pl.multiple_of(src, PHASE_ROWS), PHASE_ROWS)
        y = cb
        for tap in range(CONV_WIDTH):
            rows = pl.ds(pl.multiple_of(src + tap * STEP_ROWS, STEP_ROWS), PHASE_ROWS)
            y = y + ext[rows, :] * cw[tap:tap + 1, :]
        lhs = jnp.concatenate([y.astype(BF16), ones16], axis=1)
        for d in range(2):
            g = jnp.dot(lhs, wg_ref[d], preferred_element_type=F32)
            ta = jnp.tanh(g[:, :RNN_BLOCK])
            ti = jnp.tanh(g[:, RNN_BLOCK:])
            k = kexp[d:d + 1, :]
            a = jnp.exp2(k * ta + k)
            z = 1.0 - a * a
            m = z * lax.rsqrt(jnp.maximum(z, TINY))
            b = m * ((ti + 1.0) * y)
            a_scr[d, dst, :] = a
            b_scr[d, dst, :] = b
            if d == 0:
                carry = advance(0, a, b, carry)
        return carry

    ones8 = jnp.ones((SUBLANES, LANES), F32)
    fresh = tuple((zero8, ones8) for _ in range(N_SETS))
    assert PHASE_ROWS == SCAN_STEPS * STEP_ROWS
    ends_f = lax.fori_loop(0, seq // PHASE_ROWS, phase, fresh, unroll=32)

    n_blocks = c_len // SCAN_STEPS
    block_rows = SCAN_STEPS * STEP_ROWS

    def block(d, n):
        blk = (n_blocks - 1 - n) if d else n
        return blk, pl.ds(pl.multiple_of(blk * block_rows, block_rows), block_rows)

    def summarise(d, n, carry):
        _, rows = block(d, n)
        return advance(d, a_scr[d, rows, :], b_scr[d, rows, :], carry)

    def resolve(d, ends):
        edge = zero8
        start = [None] * N_SETS
        for s in (range(N_SETS - 1, -1, -1) if d else range(N_SETS)):
            p_cum, h_cum = _sublane_scan(ends[s][1], ends[s][0], bool(d))
            after = p_cum * edge + h_cum
            if d:
                start[s] = jnp.where(row8 == SUBLANES - 1, edge,
                                     pltpu.roll(after, SUBLANES - 1, axis=0))
                edge = jnp.broadcast_to(after[0:1, :], (SUBLANES, LANES))
            else:
                start[s] = jnp.where(row8 == 0, edge, pltpu.roll(after, 1, axis=0))
                edge = jnp.broadcast_to(after[SUBLANES - 1:SUBLANES, :], (SUBLANES, LANES))
        return tuple(start)

    def rescan(d, n, carry):
        blk, rows = block(d, n)
        av = a_scr[d, rows, :]
        bv = b_scr[d, rows, :]
        hv = hs[rows, :] if d else None
        carry = list(carry)
        out = [None] * (SCAN_STEPS * N_SETS)
        for kk in order(d):
            for s in range(N_SETS):
                h = piece(av, kk, s) * carry[s] + piece(bv, kk, s)
                carry[s] = h
                if d:
                    i = blk * SCAN_STEPS + kk
                    xs[pl.ds(s * SUBLANES * pitch + i, SUBLANES, stride=pitch), :] = (
                        h + piece(hv, kk, s))
                else:
                    out[kk * N_SETS + s] = h
        if not d:
            hs[rows, :] = jnp.concatenate(out, axis=0)
        return tuple(carry)

    _, ends_b = lax.fori_loop(
        0, n_blocks, lambda n, c: (rescan(0, n, c[0]), summarise(1, n, c[1])),
        (resolve(0, ends_f), fresh))
    lax.fori_loop(0, n_blocks, functools.partial(rescan, 1), resolve(1, ends_b))

    for ch in range(N_CHUNKS):
        rows = slice(ch * c_len, (ch + 1) * c_len)
        h = xs[ch * pitch:ch * pitch + c_len, :]
        out_ref[rows, :] = (h * gr_ref[rows, :].astype(F32)).astype(BF16)


def _rglru(rnn_x, gate, conv_w, conv_b, w_gate, lam):
    bsz, seq, _ = gate.shape
    c_len, pitch = _chunking(seq)
    assert rnn_x.shape[1] == N_CHUNKS * pitch

    def col_block(rows):
        return pl.BlockSpec((None, rows, RNN_BLOCK), lambda c, b: (b, 0, c))

    return pl.pallas_call(
        _rglru_kernel,
        grid=(N_RNN_BLOCKS, bsz),
        in_specs=[
            col_block(N_CHUNKS * pitch),
            col_block(seq),
            pl.BlockSpec((CONV_WIDTH, RNN_BLOCK), lambda c, b: (0, c)),
            pl.BlockSpec((1, RNN_BLOCK), lambda c, b: (0, c)),
            pl.BlockSpec((2, None, 2 * RNN_BLOCK, 2 * RNN_BLOCK), lambda c, b: (0, c, 0, 0)),
            pl.BlockSpec((2, RNN_BLOCK), lambda c, b: (0, c)),
        ],
        out_specs=col_block(seq),
        out_shape=jax.ShapeDtypeStruct((bsz, seq, D_RNN), BF16),
        scratch_shapes=[
            pltpu.VMEM((N_CHUNKS * pitch, RNN_BLOCK), F32),
            pltpu.VMEM(((c_len + EXT_SLOTS) * STEP_ROWS, RNN_BLOCK), F32),
            pltpu.VMEM((2, seq, RNN_BLOCK), F32),
            pltpu.VMEM((2, seq, RNN_BLOCK), F32),
            pltpu.VMEM((seq, RNN_BLOCK), F32),
        ],
        compiler_params=_params(2),
        name="rglru",
    )(rnn_x, gate, conv_w, conv_b, w_gate, lam)


def _attn_out_kernel(sink_ref, a_ref, kvn_ref, rnn_ref, x_ref, mod_ref,
                     wa_ref, wr_ref, wo_ref, ng_ref, nmod_ref, *rest, final):
    q_ref, k_ref, v_ref, ga_ref, mg_ref = _attn_in_views(a_ref)
    if final:
        o_ref, kbd, vbd, attn_scr, kv_carry = rest
    else:
        o_ref, h_ref, kbd, vbd, attn_scr, kv_carry = rest
    kp_ref, vp_ref = kv_carry.at[:, 0:KV_DIM], kv_carry.at[:, KV_DIM:2 * KV_DIM]
    kn_ref, vn_ref = kvn_ref.at[:, 0:KV_DIM], kvn_ref.at[:, KV_DIM:2 * KV_DIM]
    tq = q_ref.shape[0]
    t = pl.program_id(1)
    nt = pl.num_programs(1)
    n_blk = tq // BLOCK

    @pl.when(t == 0)
    def _():
        kv_carry[...] = jnp.zeros_like(kv_carry)

    def band_piece(tile_ref, prev_ref, next_ref, j, cols):
        if j < 0:
            return prev_ref[:, cols]
        if j >= n_blk:
            return next_ref[:, cols]
        return tile_ref[j * BLOCK:(j + 1) * BLOCK, cols]

    band = 3 * BLOCK
    qi = lax.broadcasted_iota(jnp.int32, (BLOCK, band), 0)
    kj = lax.broadcasted_iota(jnp.int32, (BLOCK, band), 1)
    in_window = (kj >= qi) & (kj <= qi + 2 * BLOCK)
    head = lax.broadcasted_iota(jnp.int32, (GROUP, BLOCK, 1), 0)

    for blk in range(n_blk):
        rows = slice(blk * BLOCK, (blk + 1) * BLOCK)
        valid = in_window
        if blk == 0:
            valid = valid & (kj >= jnp.where(t == 0, BLOCK, 0))
        if blk == n_blk - 1:
            valid = valid & (kj < jnp.where(t == nt - 1, 2 * BLOCK, band))
        zeros = jnp.zeros((BLOCK, HEAD_DIM), BF16)
        for g in range(N_KV_HEADS):
            for src, dst in (((k_ref, kp_ref, kn_ref), kbd), ((v_ref, vp_ref, vn_ref), vbd)):
                for gg in range(N_KV_HEADS):
                    cols = slice(gg * HEAD_DIM, (gg + 1) * HEAD_DIM)
                    for j in range(3):
                        r0 = g * band + j * BLOCK
                        dst[blk, r0:r0 + BLOCK, cols] = (
                            band_piece(*src, blk - 1 + j, cols) if gg == g else zeros)
        qcat = jnp.concatenate(
            [jnp.concatenate(
                [q_ref[rows, (g * GROUP + h) * HEAD_DIM:(g * GROUP + h + 1) * HEAD_DIM]
                 for g in range(N_KV_HEADS)], axis=1) for h in range(GROUP)], axis=0)
        s_all = lax.dot_general(qcat, kbd[blk], (((1,), (1,)), ((), ())),
                                preferred_element_type=F32)
        probs, denoms = [], []
        for g in range(N_KV_HEADS):
            s = s_all[:, g * band:(g + 1) * band].reshape(GROUP, BLOCK, band)
            s = jnp.where(valid[None], s, MASK_VALUE)
            sink = jnp.full((GROUP, BLOCK, 1), sink_ref[g * GROUP] * LOG2E, F32)
            for h in range(1, GROUP):
                sink = jnp.where(head == h, sink_ref[g * GROUP + h] * LOG2E, sink)
            m = jnp.maximum(jnp.max(s, axis=-1, keepdims=True), sink)
            p = jnp.exp2(s - m)
            denoms.append(jnp.sum(p, axis=-1, keepdims=True) + jnp.exp2(sink - m))
            probs.append(p.reshape(GROUP * BLOCK, band).astype(BF16))
        o_all = jnp.dot(jnp.concatenate(probs, axis=1), vbd[blk],
                        preferred_element_type=F32)
        for g in range(N_KV_HEADS):
            o = o_all[:, g * HEAD_DIM:(g + 1) * HEAD_DIM].reshape(GROUP, BLOCK, HEAD_DIM)
            o = o / denoms[g]
            for h in range(GROUP):
                cols = slice((g * GROUP + h) * HEAD_DIM, (g * GROUP + h + 1) * HEAD_DIM)
                attn_scr[rows, cols] = (o[h] * ga_ref[rows, cols].astype(F32)).astype(BF16)

    a = jnp.dot(attn_scr[...], wa_ref[...], preferred_element_type=F32)
    r = jnp.dot(rnn_ref[...], wr_ref[...], preferred_element_type=F32)
    mixed = mg_ref[:, 0:D_MODEL].astype(F32) * a + mg_ref[:, D_MODEL:].astype(F32) * r
    y = jnp.dot(mixed.astype(BF16), wo_ref[...], preferred_element_type=F32)
    out = x_ref[...] + mod_ref[:, 2 * D_MODEL:] * y
    if final:
        ms = jnp.mean(out * out, axis=-1, keepdims=True)
        o_ref[...] = out * lax.rsqrt(ms + EPS) * ng_ref[...]
    else:
        o_ref[...] = out
        h_ref[...] = _adaln(out, ng_ref[...], nmod_ref).astype(BF16)
    kv_carry[...] = a_ref[tq - BLOCK:tq, K_COL:K_COL + 2 * KV_DIM]


def _attend_and_mix(sink, attn_in, rnn, x, mod, w_attn, w_rnn, w_out, next_gain,
                    next_mod, batch_off, final):
    bsz, seq, _ = x.shape
    tq = _seq_tile(seq, 512)
    per = tq // BLOCK
    n_blocks = seq // BLOCK

    def tile(width):
        return pl.BlockSpec((None, tq, width), lambda b, t: (b, t, 0))

    kv_blk = K_COL // (2 * KV_DIM)
    assert K_COL % (2 * KV_DIM) == 0 and V_COL == K_COL + KV_DIM
    next_kv = pl.BlockSpec((None, BLOCK, 2 * KV_DIM),
                           lambda b, t: (b, jnp.minimum((t + 1) * per, n_blocks - 1), kv_blk))
    mod_spec = pl.BlockSpec((None, 1, 3 * D_MODEL), lambda b, t: (b + batch_off, 0, 0))
    out_specs = [tile(D_MODEL)]
    out_shape = [jax.ShapeDtypeStruct((bsz, seq, D_MODEL), F32)]
    if not final:
        out_specs.append(tile(D_MODEL))
        out_shape.append(jax.ShapeDtypeStruct((bsz, seq, D_MODEL), BF16))
    return pl.pallas_call(
        functools.partial(_attn_out_kernel, final=final),
        grid=(bsz, seq // tq),
        in_specs=[
            pl.BlockSpec(memory_space=pltpu.SMEM),
            tile(ATTN_IN_COLS), next_kv,
            tile(D_RNN),
            tile(D_MODEL),
            mod_spec,
            _resident((ATTN_DIM, D_MODEL)),
            _resident((D_RNN, D_MODEL)),
            _resident((D_MODEL, D_MODEL)),
            _resident((1, D_MODEL)),
            mod_spec,
        ],
        out_specs=out_specs,
        out_shape=out_shape,
        scratch_shapes=[
            pltpu.VMEM((per, N_KV_HEADS * 3 * BLOCK, KV_DIM), BF16),
            pltpu.VMEM((per, N_KV_HEADS * 3 * BLOCK, KV_DIM), BF16),
            pltpu.VMEM((tq, ATTN_DIM), BF16),
            pltpu.VMEM((BLOCK, 2 * KV_DIM), BF16),
        ],
        compiler_params=_params(2),
        name="attn_out",
    )(sink, attn_in, attn_in, rnn, x, mod, w_attn, w_rnn, w_out, next_gain, next_mod)


def _trunk(x, batch_off, mods, tables, layers, final_gain):
    seq = x.shape[1]
    rope = tables[:seq]
    depth = len(layers)
    h = _prenorm(x, mods[0], layers[0]["norm_gain"], batch_off)
    for l, p in enumerate(layers):
        final = l == depth - 1
        attn_in, rx, gr = _project(h, p["w_in"], p["w_merge"], p["b_merge"], rope)
        rnn = _rglru(rx, gr, p["conv_w"], p["conv_b"], p["w_gate"], p["lam"])
        next_gain = final_gain if final else layers[l + 1]["norm_gain"]
        next_mod = mods[l] if final else mods[l + 1]
        res = _attend_and_mix(p["sink"], attn_in, rnn, x, mods[l], p["w_attn"],
                              p["w_rnn"], p["w_out"], next_gain, next_mod, batch_off, final)
        x = res[0]
        h = None if final else res[1]
    return x


def kernel(x_prompt, x_sample, c_prompt, c_sample, norm_gain, w_ada, b_ada, w_in, attn_sink, conv_w, conv_b, rg_w_a, rg_b_a, rg_w_x, rg_b_x, rg_lambda, w_attn_proj, w_rnn_proj, w_merge, b_merge, w_out, final_gain):
    depth = w_in.shape[0]
    n_prompt = x_prompt.shape[0]
    cond = jnp.concatenate([c_prompt, c_sample], axis=0)
    mods = _modulation(cond, w_ada, b_ada)
    mods = mods.reshape(depth, cond.shape[0], 1, 3 * D_MODEL)
    tables = _rope_tables(max(x_prompt.shape[1], x_sample.shape[1]))

    w_gate = jnp.concatenate([rg_w_a, rg_w_x], axis=-1).astype(BF16)
    b_half = 0.5 * jnp.concatenate(
        [rg_b_a.reshape(depth, 2, N_RNN_BLOCKS, 1, RNN_BLOCK),
         rg_b_x.reshape(depth, 2, N_RNN_BLOCKS, 1, RNN_BLOCK)], axis=-1)
    b_hi = b_half.astype(BF16)
    b_lo = (b_half - b_hi.astype(F32)).astype(BF16)
    w_gate = jnp.concatenate(
        [w_gate, b_hi, b_lo,
         jnp.zeros((depth, 2, N_RNN_BLOCKS, RNN_BLOCK - 2, 2 * RNN_BLOCK), BF16)], axis=-2)
    gate_cols = jnp.zeros((w_in.shape[-1],), bool)
    gate_cols = gate_cols.at[GA_COL:GA_COL + ATTN_DIM].set(True)
    gate_cols = gate_cols.at[w_in.shape[-1] - D_RNN:].set(True)
    in_col_scale = jnp.where(gate_cols, 0.5, 1.0).astype(F32)
    layers = []
    for l in range(depth):
        layers.append(dict(
            norm_gain=norm_gain[l].reshape(1, D_MODEL),
            w_in=(w_in[l] * in_col_scale).astype(BF16),
            w_merge=(0.5 * w_merge[l]).astype(BF16),
            b_merge=0.5 * b_merge[l].reshape(1, 2 * D_MODEL),
            sink=attn_sink[l],
            conv_w=0.5 * conv_w[l],
            conv_b=0.5 * conv_b[l].reshape(1, D_RNN),
            w_gate=w_gate[l], lam=rg_lambda[l],
            w_attn=w_attn_proj[l].astype(BF16),
            w_rnn=w_rnn_proj[l].astype(BF16),
            w_out=w_out[l].astype(BF16),
        ))
    fg = final_gain.reshape(1, D_MODEL)
    y_prompt = _trunk(x_prompt, 0, mods, tables, layers, fg)
    y_sample = _trunk(x_sample, n_prompt, mods, tables, layers, fg)
    return (y_prompt, y_sample)
```
